```python
import math
import jax, jax.numpy as jnp
from jax import lax
import numpy as np

D_MODEL = 1024
BATCH = 8
SEQ = 2048
DEPTH = 2
DEC_BATCH = 128
DEC_SEQ = 4
PAST_LEN = 16384
PAGE_SIZE = 128

D_MIX = D_MODEL
S5_WIDTH = D_MIX // 4
S5_GROUP = 16
S5_GROUPS = S5_WIDTH // S5_GROUP
S5_STATE = 64
RG_WIDTH = D_MIX // 4
RG_BLOCKS = 8
RG_BLOCK = RG_WIDTH // RG_BLOCKS
RG_CONV = 4
RG_C = 8.0
ML_WIDTH = D_MIX // 2
ML_HEADS = 4
ML_DK = ML_WIDTH // ML_HEADS
ML_DV = ML_WIDTH // ML_HEADS
ML_CHUNK = 64
N_EXPERTS = 32
TOP_K = 4
D_FF = D_MODEL
SWIGLU_LIMIT = 7.0
SWIGLU_ALPHA = 1.702
MOE_BLOCK = 128
EPS = 1e-5

IN_SIZES = (S5_WIDTH, RG_WIDTH, RG_WIDTH, ML_WIDTH, ML_WIDTH, ML_WIDTH, ML_WIDTH, ML_HEADS, ML_HEADS)
N_IN = sum(IN_SIZES)
IN_SPLITS = tuple(int(s) for s in np.cumsum(IN_SIZES)[:-1])

kernel_name = "hybrid_s5_rglru_mlstm_moe_step"

F32 = jnp.float32


def rmsnorm(x, g):
    xf = x.astype(F32)
    y = xf * lax.rsqrt(jnp.mean(xf * xf, axis=-1, keepdims=True) + EPS)
    return (y * g.astype(F32)).astype(x.dtype)


def cmul(ar, ai, br, bi):
    return ar * br - ai * bi, ar * bi + ai * br


def s5_mixer(u, s0_re, s0_im, lam_re, lam_im, log_step, b_re, b_im, c_re, c_im, d, glu_w, glu_b):
    bsz, t_len, _ = u.shape
    ug = u.reshape(bsz, t_len, S5_GROUPS, S5_GROUP).astype(F32)
    step = jnp.exp(log_step.astype(F32))[:, None]
    lr = lam_re.astype(F32)
    li = lam_im.astype(F32)
    mag = jnp.exp(lr * step)
    ab_re = mag * jnp.cos(li * step)
    ab_im = mag * jnp.sin(li * step)
    den = lr * lr + li * li
    num_re = ab_re - 1.0
    z_re = (num_re * lr + ab_im * li) / den
    z_im = (ab_im * lr - num_re * li) / den
    bb_re, bb_im = cmul(z_re[..., None], z_im[..., None], b_re.astype(F32), b_im.astype(F32))
    bu_re = jnp.einsum('gpc,btgc->btgp', bb_re, ug)
    bu_im = jnp.einsum('gpc,btgc->btgp', bb_im, ug)
    a_re = jnp.broadcast_to(ab_re, bu_re.shape)
    a_im = jnp.broadcast_to(ab_im, bu_im.shape)

    def combine(e1, e2):
        a1r, a1i, b1r, b1i = e1
        a2r, a2i, b2r, b2i = e2
        ar, ai = cmul(a2r, a2i, a1r, a1i)
        br, bi = cmul(a2r, a2i, b1r, b1i)
        return ar, ai, br + b2r, bi + b2i

    acr, aci, sr, si = lax.associative_scan(combine, (a_re, a_im, bu_re, bu_im), axis=1)
    pr, pi = cmul(acr, aci, s0_re.astype(F32)[:, None], s0_im.astype(F32)[:, None])
    sr = sr + pr
    si = si + pi
    y = jnp.einsum('gcp,btgp->btgc', c_re.astype(F32), sr) - jnp.einsum('gcp,btgp->btgc', c_im.astype(F32), si)
    y = y.reshape(bsz, t_len, S5_WIDTH) + d.astype(F32) * u.astype(F32)
    g = jax.nn.gelu(y)
    out = g * jax.nn.sigmoid(g @ glu_w.astype(F32) + glu_b.astype(F32))
    return out.astype(u.dtype), sr[:, -1], si[:, -1]


def causal_conv(x, buf, w, b):
    t_len = x.shape[1]
    xp = jnp.concatenate([buf.astype(x.dtype), x], axis=1)
    y = sum(xp[:, j:j + t_len] * w[j] for j in range(RG_CONV)) + b
    return y, xp[:, -(RG_CONV - 1):]


def rglru_mixer(x, gate, h0, conv_buf, conv_w, conv_b, wa, ba, wx, bx, lam):
    bsz, t_len, _ = x.shape
    xc, new_buf = causal_conv(x, conv_buf, conv_w, conv_b)
    xc = xc.astype(F32)
    xb = xc.reshape(bsz, t_len, RG_BLOCKS, RG_BLOCK)
    r = jax.nn.sigmoid(jnp.einsum('btni,nij->btnj', xb, wa.astype(F32)).reshape(bsz, t_len, RG_WIDTH) + ba)
    i = jax.nn.sigmoid(jnp.einsum('btni,nij->btnj', xb, wx.astype(F32)).reshape(bsz, t_len, RG_WIDTH) + bx)
    log_a = -RG_C * r * jax.nn.softplus(-lam.astype(F32))
    a = jnp.exp(log_a)
    bt = jnp.sqrt(-jnp.expm1(2.0 * log_a)) * (i * xc)

    def combine(e1, e2):
        a1, b1 = e1
        a2, b2 = e2
        return a1 * a2, a2 * b1 + b2

    acum, hs = lax.associative_scan(combine, (a, bt), axis=1)
    h = hs + acum * h0.astype(F32)[:, None]
    y = h * jax.nn.gelu(gate.astype(F32))
    return y.astype(x.dtype), h[:, -1], new_buf


def mlstm_mixer(q, k, v, ig, fg, o, c0, n0, m0, norm_g):
    bsz, t_len, _ = q.shape
    L = math.gcd(t_len, ML_CHUNK)
    nc = t_len // L

    def heads(z):
        return z.reshape(bsz, nc, L, ML_HEADS, -1).transpose(1, 0, 3, 2, 4).astype(F32)

    def gates(z):
        return z.reshape(bsz, nc, L, ML_HEADS).transpose(1, 0, 3, 2).astype(F32)

    qh = heads(q)
    kh = heads(k) * (ML_DK ** -0.5)
    vh = heads(v)
    igh = gates(ig)
    lfh = jax.nn.log_sigmoid(gates(fg))
    causal = jnp.tril(jnp.ones((L, L), bool))

    def chunk(carry, inp):
        cm, nv, m = carry
        qc, kc, vc, ic, fc = inp
        b = jnp.cumsum(fc, axis=-1)
        dmat = b[..., :, None] - b[..., None, :] + ic[..., None, :]
        dmat = jnp.where(causal, dmat, -jnp.inf)
        inter = b + m[..., None]
        mt = jnp.maximum(inter, jnp.max(dmat, axis=-1))
        s = jnp.einsum('bhtd,bhsd->bhts', qc, kc) * jnp.exp(dmat - mt[..., None])
        sc = jnp.exp(inter - mt)
        num = jnp.einsum('bhts,bhsd->bhtd', s, vc) + sc[..., None] * jnp.einsum('bhtk,bhkv->bhtv', qc, cm)
        den = jnp.sum(s, axis=-1) + sc * jnp.einsum('bhtk,bhk->bht', qc, nv)
        h = num / jnp.maximum(jnp.abs(den), jnp.exp(-mt))[..., None]
        bl = b[..., -1]
        m_new = mt[..., -1]
        ws = jnp.exp(bl[..., None] - b + ic - m_new[..., None])
        decay = jnp.exp(bl + m - m_new)
        c_new = decay[..., None, None] * cm + jnp.einsum('bhs,bhsk,bhsv->bhkv', ws, kc, vc)
        n_new = decay[..., None] * nv + jnp.einsum('bhs,bhsk->bhk', ws, kc)
        return (c_new, n_new, m_new), h

    (cf, nf, mf), hs = lax.scan(chunk, (c0.astype(F32), n0.astype(F32), m0.astype(F32)), (qh, kh, vh, igh, lfh))
    h = hs.transpose(1, 0, 3, 2, 4).reshape(bsz, t_len, ML_HEADS, ML_DV)
    h = h * lax.rsqrt(jnp.mean(h * h, axis=-1, keepdims=True) + EPS) * norm_g.astype(F32).reshape(ML_HEADS, ML_DV)
    y = h.reshape(bsz, t_len, ML_WIDTH) * jax.nn.sigmoid(o.astype(F32))
    return y.astype(q.dtype), cf, nf, mf


def moe(x, router_w, router_b, w1, b1, w2, b2):
    bsz, t_len, dm = x.shape
    xf = x.reshape(-1, dm)
    ntok = xf.shape[0]
    logits = (xf @ router_w + router_b).astype(F32)
    top_val, top_idx = lax.top_k(logits, TOP_K)
    gates = jax.nn.softmax(top_val, axis=-1)
    n_assign = ntok * TOP_K
    flat_e = top_idx.reshape(-1)
    order = jnp.argsort(flat_e)
    sorted_e = flat_e[order]
    counts = jnp.bincount(flat_e, length=N_EXPERTS)
    padded = (counts + MOE_BLOCK - 1) // MOE_BLOCK * MOE_BLOCK
    pad_end = jnp.cumsum(padded)
    pad_start = pad_end - padded
    start = jnp.cumsum(counts) - counts
    dest = pad_start[sorted_e] + jnp.arange(n_assign) - start[sorted_e]
    n_blocks = -(-n_assign // MOE_BLOCK) + N_EXPERTS
    n_slots = n_blocks * MOE_BLOCK
    slot_tok = jnp.full((n_slots,), ntok, jnp.int32).at[dest].set((order // TOP_K).astype(jnp.int32))
    slot_gate = jnp.zeros((n_slots,), F32).at[dest].set(gates.reshape(-1)[order])
    block_e = jnp.minimum(jnp.searchsorted(pad_end, jnp.arange(n_blocks) * MOE_BLOCK, side='right'), N_EXPERTS - 1)
    x_pad = jnp.concatenate([xf, jnp.zeros((1, dm), xf.dtype)], axis=0)

    def expert_block(args):
        tok, e = args
        hb = x_pad[tok] @ w1[e] + b1[e]
        g = jnp.minimum(hb[:, :D_FF], SWIGLU_LIMIT)
        u = jnp.clip(hb[:, D_FF:], -SWIGLU_LIMIT, SWIGLU_LIMIT)
        act = g * jax.nn.sigmoid(SWIGLU_ALPHA * g) * (u + 1.0)
        return act @ w2[e] + b2[e]

    yb = lax.map(expert_block, (slot_tok.reshape(n_blocks, MOE_BLOCK), block_e))
    y = jnp.zeros((ntok + 1, dm), F32).at[slot_tok].add(yb.reshape(n_slots, dm).astype(F32) * slot_gate[:, None])
    return y[:ntok].reshape(bsz, t_len, dm).astype(x.dtype)


def setup_inputs(seed: int = 0) -> dict:
    key = jax.random.key(seed)
    ks = iter(jax.random.split(key, 48))
    nrm = lambda shape, s: jax.random.normal(next(ks), shape, F32) * s
    uni = lambda shape, lo, hi: jax.random.uniform(next(ks), shape, F32, lo, hi)
    G, P = S5_GROUPS, S5_STATE
    inp = {}
    inp['x_prompt'] = nrm((BATCH, SEQ, D_MODEL), 1.0)
    inp['x_sample'] = nrm((DEC_BATCH, DEC_SEQ, D_MODEL), 1.0)
    inp['state_s5_re'] = nrm((DEPTH, DEC_BATCH, G, P), 0.3)
    inp['state_s5_im'] = nrm((DEPTH, DEC_BATCH, G, P), 0.3)
    inp['state_rg_h'] = nrm((DEPTH, DEC_BATCH, RG_WIDTH), 0.5)
    inp['state_rg_conv'] = nrm((DEPTH, DEC_BATCH, RG_CONV - 1, RG_WIDTH), 1.0)
    inp['state_ml_c'] = nrm((DEPTH, DEC_BATCH, ML_HEADS, ML_DK, ML_DV), 0.3)
    inp['state_ml_n'] = nrm((DEPTH, DEC_BATCH, ML_HEADS, ML_DK), 0.3)
    inp['state_ml_m'] = uni((DEPTH, DEC_BATCH, ML_HEADS), 0.0, 4.0)
    inp['norm1_g'] = 1.0 + nrm((DEPTH, D_MODEL), 0.02)
    inp['w_in'] = nrm((DEPTH, D_MODEL, N_IN), D_MODEL ** -0.5)
    inp['s5_lambda_re'] = -0.5 + nrm((DEPTH, G, P), 0.01)
    inp['s5_lambda_im'] = jnp.pi * jnp.arange(P, dtype=F32) + nrm((DEPTH, G, P), 0.01)
    inp['s5_log_step'] = uni((DEPTH, G), math.log(1e-3), math.log(1e-1))
    inp['s5_b_re'] = nrm((DEPTH, G, P, S5_GROUP), (2 * S5_GROUP) ** -0.5)
    inp['s5_b_im'] = nrm((DEPTH, G, P, S5_GROUP), (2 * S5_GROUP) ** -0.5)
    inp['s5_c_re'] = nrm((DEPTH, G, S5_GROUP, P), (2 * P) ** -0.5)
    inp['s5_c_im'] = nrm((DEPTH, G, S5_GROUP, P), (2 * P) ** -0.5)
    inp['s5_d'] = nrm((DEPTH, S5_WIDTH), 0.5)
    inp['s5_glu_w'] = nrm((DEPTH, S5_WIDTH, S5_WIDTH), S5_WIDTH ** -0.5)
    inp['s5_glu_b'] = nrm((DEPTH, S5_WIDTH), 0.02)
    inp['rg_conv_w'] = nrm((DEPTH, RG_CONV, RG_WIDTH), RG_CONV ** -0.5)
    inp['rg_conv_b'] = nrm((DEPTH, RG_WIDTH), 0.02)
    inp['rg_wa'] = nrm((DEPTH, RG_BLOCKS, RG_BLOCK, RG_BLOCK), RG_BLOCK ** -0.5)
    inp['rg_ba'] = nrm((DEPTH, RG_WIDTH), 0.02)
    inp['rg_wx'] = nrm((DEPTH, RG_BLOCKS, RG_BLOCK, RG_BLOCK), RG_BLOCK ** -0.5)
    inp['rg_bx'] = nrm((DEPTH, RG_WIDTH), 0.02)
    a_c = uni((DEPTH, RG_WIDTH), 0.9, 0.999)
    p_a = a_c ** (1.0 / RG_C)
    inp['rg_lambda'] = jnp.log(p_a) - jnp.log1p(-p_a)
    f_bias = jnp.broadcast_to(jnp.linspace(3.0, 6.0, ML_HEADS, dtype=F32), (DEPTH, ML_HEADS))
    inp['ml_gate_bias'] = jnp.concatenate([jnp.zeros((DEPTH, ML_HEADS), F32), f_bias], axis=-1) + nrm((DEPTH, 2 * ML_HEADS), 0.1)
    inp['ml_norm_g'] = 1.0 + nrm((DEPTH, ML_WIDTH), 0.02)
    inp['w_out'] = nrm((DEPTH, D_MIX, D_MODEL), (2 * D_MIX) ** -0.5)
    inp['norm2_g'] = 1.0 + nrm((DEPTH, D_MODEL), 0.02)
    inp['router_w'] = nrm((DEPTH, D_MODEL, N_EXPERTS), D_MODEL ** -0.5)
    inp['router_b'] = nrm((DEPTH, N_EXPERTS), 0.01)
    inp['exp_w1'] = nrm((DEPTH, N_EXPERTS, D_MODEL, 2 * D_FF), D_MODEL ** -0.5)
    inp['exp_b1'] = nrm((DEPTH, N_EXPERTS, 2 * D_FF), 0.02)
    inp['exp_w2'] = nrm((DEPTH, N_EXPERTS, D_FF, D_MODEL), (2 * D_FF) ** -0.5)
    inp['exp_b2'] = nrm((DEPTH, N_EXPERTS, D_MODEL), 0.02)
    inp['final_norm_g'] = 1.0 + nrm((D_MODEL,), 0.02)
    return inp


def reference(x_prompt, x_sample, state_s5_re, state_s5_im, state_rg_h, state_rg_conv, state_ml_c, state_ml_n, state_ml_m,
              norm1_g, w_in, s5_lambda_re, s5_lambda_im, s5_log_step, s5_b_re, s5_b_im, s5_c_re, s5_c_im, s5_d,
              s5_glu_w, s5_glu_b, rg_conv_w, rg_conv_b, rg_wa, rg_ba, rg_wx, rg_bx, rg_lambda, ml_gate_bias, ml_norm_g,
              w_out, norm2_g, router_w, router_b, exp_w1, exp_b1, exp_w2, exp_b2, final_norm_g):

    def trunk(x, s5r, s5i, rgh, rgc, mlc, mln, mlm):
        outs = [[] for _ in range(7)]
        for l in range(DEPTH):
            h = rmsnorm(x, norm1_g[l])
            z = h @ w_in[l]
            u_s5, x_rg, g_rg, q, k, v, o, gi, gf = jnp.split(z, IN_SPLITS, axis=-1)
            y1, nr, ni = s5_mixer(u_s5, s5r[l], s5i[l], s5_lambda_re[l], s5_lambda_im[l], s5_log_step[l],
                                  s5_b_re[l], s5_b_im[l], s5_c_re[l], s5_c_im[l], s5_d[l], s5_glu_w[l], s5_glu_b[l])
            y2, nh, ncv = rglru_mixer(x_rg, g_rg, rgh[l], rgc[l], rg_conv_w[l], rg_conv_b[l], rg_wa[l], rg_ba[l],
                                      rg_wx[l], rg_bx[l], rg_lambda[l])
            y3, nC, nn, nm = mlstm_mixer(q, k, v, gi + ml_gate_bias[l, :ML_HEADS], gf + ml_gate_bias[l, ML_HEADS:], o,
                                         mlc[l], mln[l], mlm[l], ml_norm_g[l])
            mix = jnp.concatenate([y1.astype(x.dtype), y2.astype(x.dtype), y3.astype(x.dtype)], axis=-1)
            x = x + mix @ w_out[l]
            x = x + moe(rmsnorm(x, norm2_g[l]), router_w[l], router_b[l], exp_w1[l], exp_b1[l], exp_w2[l], exp_b2[l])
            for lst, val in zip(outs, (nr, ni, nh, ncv, nC, nn, nm)):
                lst.append(val)
        return rmsnorm(x, final_norm_g), [jnp.stack(s) for s in outs]

    bp = x_prompt.shape[0]
    zeros = lambda *shape: jnp.zeros((DEPTH, bp) + shape, F32)
    y_prompt, sp = trunk(x_prompt, zeros(S5_GROUPS, S5_STATE), zeros(S5_GROUPS, S5_STATE), zeros(RG_WIDTH),
                         zeros(RG_CONV - 1, RG_WIDTH), zeros(ML_HEADS, ML_DK, ML_DV), zeros(ML_HEADS, ML_DK), zeros(ML_HEADS))
    y_sample, ss = trunk(x_sample, state_s5_re, state_s5_im, state_rg_h, state_rg_conv, state_ml_c, state_ml_n, state_ml_m)
    return (y_prompt, y_sample, sp[0], sp[1], sp[2], sp[3], sp[4], sp[5], sp[6], ss[0], ss[1], ss[2], ss[3], ss[4], ss[5], ss[6])
```

```python
import functools
import math

import jax
import jax.numpy as jnp
from jax import lax
from jax.experimental import pallas as pl
from jax.experimental.pallas import tpu as pltpu

F32 = jnp.float32
BF16 = jnp.bfloat16

SUBLANES = 8
LANES = 128
VMEM_LIMIT_BYTES = 56 * 1024 * 1024

D_MODEL = 1024
DEPTH = 2
S5_WIDTH = 256
S5_GROUP = 16
S5_GROUPS = 16
S5_STATE = 64
S5_FLAT = S5_GROUPS * S5_STATE
RG_WIDTH = 256
RG_BLOCKS = 8
RG_CONV = 4
RG_C = 8.0
ML_WIDTH = 512
ML_HEADS = 4
ML_DK = 128
ML_DV = 128
ML_CHUNK = 64
N_EXPERTS = 32
TOP_K = 4
D_FF = 1024
SWIGLU_LIMIT = 7.0
SWIGLU_ALPHA = 1.702
EPS = 1e-5

IN_S5 = (0, 256)
IN_RG = (256, 768)
IN_ML = (768, 2816)
IN_GATE = (2816, 2944)
N_IN = 2824
N_IN_PAD = 2944

ROW_TILE = 512
ROUTE_TILE = 128
MOE_TILE = 256
NEG_BIG = -1e30


def _cparams(sem):
    return pltpu.CompilerParams(dimension_semantics=sem, vmem_limit_bytes=VMEM_LIMIT_BYTES)


def _full(shape):
    n = len(shape)
    return pl.BlockSpec(shape, lambda *_: (0,) * n)


def _inproj_body(x_ref, g_ref, w_ref, zs5_ref, zrg_ref, zml_ref, zgt_ref):
    x = x_ref[...]
    h = x * lax.rsqrt(jnp.mean(x * x, axis=-1, keepdims=True) + EPS) * g_ref[...]
    hb = h.astype(BF16)
    for ref, (lo, hi) in ((zs5_ref, IN_S5), (zrg_ref, IN_RG), (zml_ref, IN_ML), (zgt_ref, IN_GATE)):
        ref[...] = jnp.dot(hb, w_ref[:, lo:hi], preferred_element_type=F32)


def _inproj(x, g, w):
    n = x.shape[0]
    tm = ROW_TILE
    widths = [hi - lo for lo, hi in (IN_S5, IN_RG, IN_ML, IN_GATE)]
    return pl.pallas_call(
        _inproj_body,
        grid=(n // tm,),
        in_specs=[pl.BlockSpec((tm, D_MODEL), lambda i: (i, 0)), _full((1, D_MODEL)), _full((D_MODEL, N_IN_PAD))],
        out_specs=[pl.BlockSpec((tm, w_), lambda i: (i, 0)) for w_ in widths],
        out_shape=[jax.ShapeDtypeStruct((n, w_), F32) for w_ in widths],
        compiler_params=_cparams(("arbitrary",)),
        name="inproj",
    )(x, g, w)


def _s5_body(u_ref, s0r_ref, s0i_ref, lam_ref, bre_ref, bim_ref, cre_ref, cim_ref, d_ref, gw_ref, gb_ref,
             y_ref, sr_ref, si_ref, bur, bui, ab, bbr, bbi, tm_s, bm_s, *, nb, tc, nc, t_valid):
    c = pl.program_id(0)

    @pl.when(c == 0)
    def _():
        sr_ref[...] = s0r_ref[...]
        si_ref[...] = s0i_ref[...]
        lr = lam_ref[0:1, :]
        li = lam_ref[1:2, :]
        step = jnp.exp(lam_ref[2:3, :])
        mag = jnp.exp(lr * step)
        ab_re = mag * jnp.cos(li * step)
        ab_im = mag * jnp.sin(li * step)
        den = lr * lr + li * li
        num_re = ab_re - 1.0
        z_re = (num_re * lr + ab_im * li) / den
        z_im = (ab_im * lr - num_re * li) / den
        ab[0:1, :] = ab_re
        ab[1:2, :] = ab_im
        bbr[...] = (z_re * bre_ref[...] - z_im * bim_ref[...]).astype(BF16)
        bbi[...] = (z_re * bim_ref[...] + z_im * bre_ref[...]).astype(BF16)

    _to_time_major(u_ref, tm_s, bm_s, nb, tc)
    u = jnp.concatenate([tm_s[j] for j in range(S5_WIDTH // LANES)], axis=-1)
    ub = u.astype(BF16)
    bur[...] = jnp.dot(ub, bbr[...], preferred_element_type=F32)
    bui[...] = jnp.dot(ub, bbi[...], preferred_element_type=F32)

    a_re = jnp.broadcast_to(ab[0:1, :], (SUBLANES, S5_FLAT))
    a_im = jnp.broadcast_to(ab[1:2, :], (SUBLANES, S5_FLAT))
    last_in = t_valid - (nc - 1) * tc
    steps = tc if last_in == tc else jnp.where(c == nc - 1, last_in, tc)

    def group(gi, _):
        g8 = gi * SUBLANES
        rows8 = pl.ds(pl.multiple_of(g8, SUBLANES), SUBLANES)

        def step_fn(t, carry):
            s_re, s_im = carry
            rows = pl.ds(pl.multiple_of(t * nb + g8, SUBLANES), SUBLANES)
            n_re = a_re * s_re - a_im * s_im + bur[rows, :]
            n_im = a_re * s_im + a_im * s_re + bui[rows, :]
            bur[rows, :] = n_re
            bui[rows, :] = n_im
            return n_re, n_im

        s_re, s_im = lax.fori_loop(0, steps, step_fn, (sr_ref[rows8, :], si_ref[rows8, :]))
        sr_ref[rows8, :] = s_re
        si_ref[rows8, :] = s_im
        return 0

    lax.fori_loop(0, nb // SUBLANES, group, 0)

    y = (jnp.dot(bur[...].astype(BF16), cre_ref[...], preferred_element_type=F32)
         - jnp.dot(bui[...].astype(BF16), cim_ref[...], preferred_element_type=F32))
    y = y + d_ref[...] * u
    g = jax.nn.gelu(y)
    gate = jax.nn.sigmoid(jnp.dot(g.astype(BF16), gw_ref[...], preferred_element_type=F32) + gb_ref[...])
    out = g * gate
    for j in range(S5_WIDTH // LANES):
        tm_s[j] = out[:, j * LANES:(j + 1) * LANES]
    _to_batch_major(tm_s, bm_s, y_ref, nb, tc)


def _to_time_major(src_ref, tm_s, bm_s, nb, tc):
    for j in range(tm_s.shape[0]):
        lanes = slice(j * LANES, (j + 1) * LANES)
        if nb <= tc:
            for b in range(nb):
                tm_s[j, pl.ds(b, tc, stride=nb), :] = src_ref[b, :, lanes]
        else:
            bm_s[j] = src_ref[:, :, lanes].reshape(nb * tc, LANES)
            for t in range(tc):
                tm_s[j, t * nb:(t + 1) * nb, :] = bm_s[j, pl.ds(t, nb, stride=tc), :]


def _to_batch_major(tm_s, bm_s, dst_ref, nb, tc):
    for j in range(tm_s.shape[0]):
        lanes = slice(j * LANES, (j + 1) * LANES)
        if nb <= tc:
            for b in range(nb):
                dst_ref[b, :, lanes] = tm_s[j, pl.ds(b, tc, stride=nb), :]
        else:
            for t in range(tc):
                bm_s[j, pl.ds(t, nb, stride=tc), :] = tm_s[j, t * nb:(t + 1) * nb, :]
            dst_ref[:, :, lanes] = bm_s[j].reshape(nb, tc, LANES)


def _s5(u, s0r, s0i, lam, bre, bim, cre, cim, d, gw, gb, *, tc, t_valid):
    nb, t, _ = u.shape
    nc = t // tc
    assert (nc - 1) * tc < t_valid <= t
    body = functools.partial(_s5_body, nb=nb, tc=tc, nc=nc, t_valid=t_valid)
    return pl.pallas_call(
        body,
        grid=(t // tc,),
        in_specs=[pl.BlockSpec((nb, tc, S5_WIDTH), lambda c: (0, c, 0)),
                  _full((nb, S5_FLAT)), _full((nb, S5_FLAT)), _full((SUBLANES, S5_FLAT)),
                  _full((S5_WIDTH, S5_FLAT)), _full((S5_WIDTH, S5_FLAT)),
                  _full((S5_FLAT, S5_WIDTH)), _full((S5_FLAT, S5_WIDTH)),
                  _full((1, S5_WIDTH)), _full((S5_WIDTH, S5_WIDTH)), _full((1, S5_WIDTH))],
        out_specs=[pl.BlockSpec((nb, tc, S5_WIDTH), lambda c: (0, c, 0)),
                   _full((nb, S5_FLAT)), _full((nb, S5_FLAT))],
        out_shape=[jax.ShapeDtypeStruct((nb, t, S5_WIDTH), F32),
                   jax.ShapeDtypeStruct((nb, S5_FLAT), F32), jax.ShapeDtypeStruct((nb, S5_FLAT), F32)],
        scratch_shapes=[pltpu.VMEM((nb * tc, S5_FLAT), F32), pltpu.VMEM((nb * tc, S5_FLAT), F32),
                        pltpu.VMEM((SUBLANES, S5_FLAT), F32),
                        pltpu.VMEM((S5_WIDTH, S5_FLAT), BF16), pltpu.VMEM((S5_WIDTH, S5_FLAT), BF16),
                        pltpu.VMEM((S5_WIDTH // LANES, nb * tc, LANES), F32),
                        pltpu.VMEM((S5_WIDTH // LANES, nb * tc, LANES), F32)],
        compiler_params=_cparams(("arbitrary",)),
        name="s5_scan",
    )(u, s0r, s0i, lam, bre, bim, cre, cim, d, gw, gb)


def _rglru_body(z_ref, h0_ref, cb0_ref, cw_ref, cb_ref, wa_ref, ba_ref, wx_ref, bx_ref, lam_ref,
                y_ref, h_ref, cbn_ref, xs, a_s, b_s, tm_s, bm_s, *, nb, tc, nc, t_valid):
    c = pl.program_id(0)
    hist = RG_CONV - 1

    @pl.when(c == 0)
    def _():
        h_ref[...] = h0_ref[...]
        xs[0:hist * nb, :] = cb0_ref[...]

    _to_time_major(z_ref, tm_s, bm_s, nb, tc)
    xs[hist * nb:(hist + tc) * nb, :] = jnp.concatenate([tm_s[j] for j in range(RG_WIDTH // LANES)], axis=-1)

    xc = jnp.zeros((tc * nb, RG_WIDTH), F32) + cb_ref[...]
    for j in range(RG_CONV):
        xc = xc + xs[j * nb:(j + tc) * nb, :] * cw_ref[j:j + 1, :]

    last_in = t_valid - (nc - 1) * tc
    last = tc if last_in == tc else jnp.where(c == nc - 1, last_in, tc)

    @pl.when(c == nc - 1)
    def _():
        cbn_ref[...] = xs[last_in * nb:(last_in + hist) * nb, :]

    xs[0:hist * nb, :] = xs[tc * nb:(tc + hist) * nb, :]

    xb = xc.astype(BF16)
    r = jax.nn.sigmoid(jnp.dot(xb, wa_ref[...], preferred_element_type=F32) + ba_ref[...])
    i = jax.nn.sigmoid(jnp.dot(xb, wx_ref[...], preferred_element_type=F32) + bx_ref[...])
    log_a = -RG_C * r * jax.nn.softplus(-lam_ref[...])
    a_s[...] = jnp.exp(log_a)
    th = jnp.tanh(log_a)
    b_s[...] = jnp.sqrt(-2.0 * th / (1.0 - th)) * (i * xc)

    def group(gi, _):
        g8 = gi * SUBLANES
        rows8 = pl.ds(pl.multiple_of(g8, SUBLANES), SUBLANES)

        def step_fn(t, h):
            rows = pl.ds(pl.multiple_of(t * nb + g8, SUBLANES), SUBLANES)
            h = a_s[rows, :] * h + b_s[rows, :]
            b_s[rows, :] = h
            return h

        h_ref[rows8, :] = lax.fori_loop(0, last, step_fn, h_ref[rows8, :])
        return 0

    lax.fori_loop(0, nb // SUBLANES, group, 0)
    for j in range(RG_WIDTH // LANES):
        tm_s[j] = b_s[:, j * LANES:(j + 1) * LANES]
    _to_batch_major(tm_s, bm_s, y_ref, nb, tc)
    y_ref[...] = y_ref[...] * jax.nn.gelu(z_ref[:, :, RG_WIDTH:2 * RG_WIDTH])


def _rglru(z, h0, cb0, cw, cb, wa, ba, wx, bx, lam, *, tc, t_valid):
    nb, t, _ = z.shape
    hist = RG_CONV - 1
    nc = t // tc
    assert (nc - 1) * tc + hist <= t_valid <= t
    body = functools.partial(_rglru_body, nb=nb, tc=tc, nc=nc, t_valid=t_valid)
    y, h, cbn = pl.pallas_call(
        body,
        grid=(t // tc,),
        in_specs=[pl.BlockSpec((nb, tc, 2 * RG_WIDTH), lambda c: (0, c, 0)),
                  _full((nb, RG_WIDTH)), _full((hist * nb, RG_WIDTH)),
                  _full((RG_CONV, RG_WIDTH)), _full((1, RG_WIDTH)),
                  _full((RG_WIDTH, RG_WIDTH)), _full((1, RG_WIDTH)),
                  _full((RG_WIDTH, RG_WIDTH)), _full((1, RG_WIDTH)), _full((1, RG_WIDTH))],
        out_specs=[pl.BlockSpec((nb, tc, RG_WIDTH), lambda c: (0, c, 0)),
                   _full((nb, RG_WIDTH)), _full((hist * nb, RG_WIDTH))],
        out_shape=[jax.ShapeDtypeStruct((nb, t, RG_WIDTH), F32),
                   jax.ShapeDtypeStruct((nb, RG_WIDTH), F32), jax.ShapeDtypeStruct((hist * nb, RG_WIDTH), F32)],
        scratch_shapes=[pltpu.VMEM(((tc + hist) * nb, RG_WIDTH), F32),
                        pltpu.VMEM((nb * tc, RG_WIDTH), F32), pltpu.VMEM((nb * tc, RG_WIDTH), F32),
                        pltpu.VMEM((RG_WIDTH // LANES, nb * tc, LANES), F32),
                        pltpu.VMEM((RG_WIDTH // LANES, nb * tc, LANES), F32)],
        compiler_params=_cparams(("arbitrary",)),
        name="rglru_scan",
    )(z, h0, cb0.transpose(1, 0, 2).reshape(hist * nb, RG_WIDTH), cw, cb, wa, ba, wx, bx, lam)
    return y, h, cbn.reshape(hist, nb, RG_WIDTH).transpose(1, 0, 2)


def _dot_nt(a, b, **kw):
    return lax.dot_general(a, b, (((1,), (1,)), ((), ())), preferred_element_type=F32, **kw)


def _dot_tn(a, b, **kw):
    return lax.dot_general(a, b, (((0,), (0,)), ((), ())), preferred_element_type=F32, **kw)


def _mlstm_body(q_ref, k_ref, v_ref, o_ref, g_ref, gb_ref, ng_ref, c0_ref, n0_ref, m0_ref,
                y_ref, cn_ref, nn_ref, mn_ref, *, bb, cl, t_valid):
    c = pl.program_id(1)

    @pl.when(c == 0)
    def _():
        cn_ref[...] = c0_ref[...]
        nn_ref[...] = n0_ref[...]
        mn_ref[...] = m0_ref[...]

    row = lax.broadcasted_iota(jnp.int32, (cl, cl), 0)
    col = lax.broadcasted_iota(jnp.int32, (cl, cl), 1)
    tril = (col <= row).astype(F32)
    keep = (col <= row) & (col + c * cl < t_valid)
    t_ok = (lax.broadcasted_iota(jnp.int32, (cl, LANES), 0) + c * cl) < t_valid
    lane = lax.broadcasted_iota(jnp.int32, (cl, LANES), 1)
    ones = jnp.ones((cl, LANES), F32)

    def one_batch(bi, _):
        graw = g_ref[bi] + gb_ref[...]
        lf = jnp.where(t_ok, jax.nn.log_sigmoid(graw), 0.0)
        bcum = jnp.dot(tril, lf, preferred_element_type=F32, precision=lax.Precision.HIGHEST)
        ig = jnp.where(t_ok, graw, NEG_BIG)
        for h in range(ML_HEADS):
            hs = slice(h * ML_DK, (h + 1) * ML_DK)
            qf = q_ref[bi, :, hs]
            qh = qf.astype(BF16)
            kh = (k_ref[bi, :, hs] * (ML_DK ** -0.5))
            vh = v_ref[bi, :, hs].astype(BF16)
            b_col = bcum[:, ML_HEADS + h:ML_HEADS + h + 1]
            i_col = ig[:, h:h + 1]
            w_col = i_col - b_col
            w_row = _dot_nt(ones, jnp.where(lane == h, w_col, 0.0), precision=lax.Precision.HIGHEST)
            m_old = mn_ref[bi, :, h:h + 1]
            dmat = jnp.where(keep, b_col + w_row, -jnp.inf)
            inter = b_col + m_old
            mt = jnp.maximum(inter, jnp.max(dmat, axis=-1, keepdims=True))
            s = _dot_nt(qh, kh.astype(BF16)) * jnp.exp(dmat - mt)
            sc = jnp.exp(inter - mt)
            cm = cn_ref[bi, h]
            nv = nn_ref[bi, h:h + 1, :]
            num = (jnp.dot(s.astype(BF16), vh, preferred_element_type=F32)
                   + sc * jnp.dot(qh, cm.astype(BF16), preferred_element_type=F32))
            qn = jnp.sum(qf * nv, axis=-1, keepdims=True)
            den = jnp.sum(s, axis=-1, keepdims=True) + sc * qn
            hh = num / jnp.maximum(jnp.abs(den), jnp.exp(-mt))
            bl = b_col[cl - 1:cl, :]
            m_new = mt[cl - 1:cl, :]
            ws = jnp.exp(bl - b_col + i_col - m_new)
            decay = jnp.exp(bl + m_old - m_new)
            wk = ws * kh
            cn_ref[bi, h] = decay * cm + _dot_tn(wk.astype(BF16), vh)
            nn_ref[bi, h:h + 1, :] = decay * nv + jnp.sum(wk, axis=0, keepdims=True)
            mn_ref[bi, :, h:h + 1] = m_new
            hn = hh * lax.rsqrt(jnp.mean(hh * hh, axis=-1, keepdims=True) + EPS) * ng_ref[:, hs]
            y_ref[bi, :, hs] = hn * jax.nn.sigmoid(o_ref[bi, :, hs])
        return 0

    lax.fori_loop(0, bb, one_batch, 0)


def _mlstm(zml, zgt, gbias, ng, c0, n0, m0, *, bb, cl, t_valid):
    nb, t, _ = zml.shape
    nc = t // cl
    body = functools.partial(_mlstm_body, bb=bb, cl=cl, t_valid=t_valid)
    zspec = lambda j: pl.BlockSpec((bb, cl, ML_WIDTH), lambda i, c, j=j: (i, c, j))
    return pl.pallas_call(
        body,
        grid=(nb // bb, nc),
        in_specs=[zspec(0), zspec(1), zspec(2), zspec(3),
                  pl.BlockSpec((bb, cl, LANES), lambda i, c: (i, c, 0)),
                  _full((1, LANES)), _full((1, ML_WIDTH)),
                  pl.BlockSpec((bb, ML_HEADS, ML_DK, ML_DV), lambda i, c: (i, 0, 0, 0)),
                  pl.BlockSpec((bb, ML_HEADS, ML_DK), lambda i, c: (i, 0, 0)),
                  pl.BlockSpec((bb, 1, ML_HEADS), lambda i, c: (i, 0, 0))],
        out_specs=[pl.BlockSpec((bb, cl, ML_WIDTH), lambda i, c: (i, c, 0)),
                   pl.BlockSpec((bb, ML_HEADS, ML_DK, ML_DV), lambda i, c: (i, 0, 0, 0)),
                   pl.BlockSpec((bb, ML_HEADS, ML_DK), lambda i, c: (i, 0, 0)),
                   pl.BlockSpec((bb, 1, ML_HEADS), lambda i, c: (i, 0, 0))],
        out_shape=[jax.ShapeDtypeStruct((nb, t, ML_WIDTH), F32),
                   jax.ShapeDtypeStruct((nb, ML_HEADS, ML_DK, ML_DV), F32),
                   jax.ShapeDtypeStruct((nb, ML_HEADS, ML_DK), F32),
                   jax.ShapeDtypeStruct((nb, 1, ML_HEADS), F32)],
        compiler_params=_cparams(("arbitrary", "arbitrary")),
        name="mlstm_chunks",
    )(zml, zml, zml, zml, zgt, gbias, ng, c0, n0, m0)


def _outproj_body(x_ref, y1_ref, y2_ref, y3_ref, wo_ref, g2_ref, rw_ref, rb_ref,
                  xn_ref, hn_ref, ridx_ref, rgt_ref):
    acc = jnp.dot(y1_ref[...].astype(BF16), wo_ref[0:256, :], preferred_element_type=F32)
    acc = acc + jnp.dot(y2_ref[...].astype(BF16), wo_ref[256:512, :], preferred_element_type=F32)
    acc = acc + jnp.dot(y3_ref[...].astype(BF16), wo_ref[512:1024, :], preferred_element_type=F32)
    x = x_ref[...] + acc
    xn_ref[...] = x
    h = x * lax.rsqrt(jnp.mean(x * x, axis=-1, keepdims=True) + EPS) * g2_ref[...]
    hn_ref[...] = h
    hb = h.astype(BF16)
    tm = x.shape[0]
    eidx = lax.broadcasted_iota(jnp.int32, (N_EXPERTS, ROUTE_TILE), 0)
    sub = lax.broadcasted_iota(jnp.int32, (LANES, ROUTE_TILE), 0)
    for j in range(tm // ROUTE_TILE):
        rows = slice(j * ROUTE_TILE, (j + 1) * ROUTE_TILE)
        vals = _dot_nt(rw_ref[...], hb[rows, :]) + rb_ref[...]
        tops, idxs = [], []
        for _ in range(TOP_K):
            m = jnp.max(vals, axis=0, keepdims=True)
            idx = jnp.min(jnp.where(vals == m, eidx, N_EXPERTS), axis=0, keepdims=True)
            vals = jnp.where(eidx == idx, -jnp.inf, vals)
            tops.append(m)
            idxs.append(idx)
        exps = [jnp.exp(t - tops[0]) for t in tops]
        inv = 1.0 / functools.reduce(lambda a, b: a + b, exps)
        ridx_ref[:, rows] = jnp.concatenate(idxs + idxs, axis=0)
        gt = jnp.zeros((LANES, ROUTE_TILE), F32)
        for kk in range(TOP_K):
            gt = jnp.where(sub == kk, exps[kk] * inv, gt)
        rgt_ref[rows, :] = gt.T


def _outproj(x, y1, y2, y3, wo, g2, rw, rb):
    n = x.shape[0]
    tm = ROW_TILE
    rowspec = lambda w_: pl.BlockSpec((tm, w_), lambda i: (i, 0))
    return pl.pallas_call(
        _outproj_body,
        grid=(n // tm,),
        in_specs=[rowspec(D_MODEL), rowspec(S5_WIDTH), rowspec(RG_WIDTH), rowspec(ML_WIDTH),
                  _full((D_MODEL, D_MODEL)), _full((1, D_MODEL)),
                  _full((N_EXPERTS, D_MODEL)), _full((N_EXPERTS, 1))],
        out_specs=[rowspec(D_MODEL), rowspec(D_MODEL),
                   pl.BlockSpec((2 * TOP_K, tm), lambda i: (0, i)), rowspec(LANES)],
        out_shape=[jax.ShapeDtypeStruct((n, D_MODEL), F32), jax.ShapeDtypeStruct((n, D_MODEL), F32),
                   jax.ShapeDtypeStruct((2 * TOP_K, n), jnp.int32), jax.ShapeDtypeStruct((n, LANES), F32)],
        compiler_params=_cparams(("arbitrary",)),
        name="outproj_route",
    )(x, y1, y2, y3, wo, g2, rw, rb)


def _moe_body(be_ref, bv_ref, tok_ref, pos_ref, hn_ref, w1_ref, b1_ref, w2_ref, b2_ref, y4_ref,
              w1b, w2b, xbuf, obuf, gsem, ssem):
    i = pl.program_id(0)
    prev = be_ref[jnp.maximum(i - 1, 0)]
    fresh = jnp.logical_or(i == 0, be_ref[i] != prev)

    @pl.when(jnp.logical_and(fresh, bv_ref[i] > 0))
    def _():
        w1b[...] = w1_ref[0].astype(BF16)
        w2b[...] = w2_ref[0].astype(BF16)

    @pl.when(bv_ref[i] > 0)
    def _():
        def gather(r, _):
            t = tok_ref[0, 0, r]
            pltpu.make_async_copy(hn_ref.at[pl.ds(t, 1)], xbuf.at[pl.ds(r, 1)], gsem).start()
            return 0

        lax.fori_loop(0, MOE_TILE, gather, 0)
        pltpu.make_async_copy(hn_ref.at[pl.ds(0, MOE_TILE)], xbuf, gsem).wait()

        xb = xbuf[...].astype(BF16)
        hb = jnp.dot(xb, w1b[...], preferred_element_type=F32) + b1_ref[0]
        g = jnp.minimum(hb[:, :D_FF], SWIGLU_LIMIT)
        u = jnp.clip(hb[:, D_FF:], -SWIGLU_LIMIT, SWIGLU_LIMIT)
        act = g * jax.nn.sigmoid(SWIGLU_ALPHA * g) * (u + 1.0)
        obuf[...] = jnp.dot(act.astype(BF16), w2b[...], preferred_element_type=F32) + b2_ref[0]

    def scatter(r, _):
        p = pos_ref[0, 0, r]
        pltpu.make_async_copy(obuf.at[pl.ds(r, 1)], y4_ref.at[pl.ds(p, 1)], ssem).start()
        return 0

    lax.fori_loop(0, MOE_TILE, scatter, 0)
    pltpu.make_async_copy(obuf, y4_ref.at[pl.ds(0, MOE_TILE)], ssem).wait()


def _moe(block_e, block_v, slot_tok, slot_pos, hn, w1, b1, w2, b2):
    n_blocks = block_e.shape[0]
    tm = MOE_TILE
    n_rows = n_blocks * tm
    grid_spec = pltpu.PrefetchScalarGridSpec(
        num_scalar_prefetch=2,
        grid=(n_blocks,),
        in_specs=[pl.BlockSpec((1, 1, tm), lambda i, be, bv: (i, 0, 0), memory_space=pltpu.SMEM),
                  pl.BlockSpec((1, 1, tm), lambda i, be, bv: (i, 0, 0), memory_space=pltpu.SMEM),
                  pl.BlockSpec(memory_space=pl.ANY),
                  pl.BlockSpec((1, D_MODEL, 2 * D_FF), lambda i, be, bv: (be[i], 0, 0)),
                  pl.BlockSpec((1, 1, 2 * D_FF), lambda i, be, bv: (be[i], 0, 0)),
                  pl.BlockSpec((1, D_FF, D_MODEL), lambda i, be, bv: (be[i], 0, 0)),
                  pl.BlockSpec((1, 1, D_MODEL), lambda i, be, bv: (be[i], 0, 0))],
        out_specs=pl.BlockSpec(memory_space=pl.ANY),
        scratch_shapes=[pltpu.VMEM((D_MODEL, 2 * D_FF), BF16), pltpu.VMEM((D_FF, D_MODEL), BF16),
                        pltpu.VMEM((tm, D_MODEL), F32), pltpu.VMEM((tm, D_MODEL), F32),
                        pltpu.SemaphoreType.DMA, pltpu.SemaphoreType.DMA],
    )
    return pl.pallas_call(
        _moe_body,
        grid_spec=grid_spec,
        out_shape=jax.ShapeDtypeStruct((n_rows, D_MODEL), F32),
        compiler_params=_cparams(("arbitrary",)),
        name="moe_experts",
    )(block_e, block_v, slot_tok, slot_pos, hn, w1, b1, w2, b2)


def _combine_body(x_ref, y4_ref, rgt_ref, fg_ref, o_ref, *, final):
    x = x_ref[...]
    for kk in range(TOP_K):
        x = x + rgt_ref[:, kk:kk + 1] * y4_ref[:, kk * D_MODEL:(kk + 1) * D_MODEL]
    if final:
        x = x * lax.rsqrt(jnp.mean(x * x, axis=-1, keepdims=True) + EPS) * fg_ref[...]
    o_ref[...] = x


def _combine(x, y4, rgt, fg, *, final):
    n = x.shape[0]
    tm = ROW_TILE
    return pl.pallas_call(
        functools.partial(_combine_body, final=final),
        grid=(n // tm,),
        in_specs=[pl.BlockSpec((tm, D_MODEL), lambda i: (i, 0)),
                  pl.BlockSpec((tm, TOP_K * D_MODEL), lambda i: (i, 0)),
                  pl.BlockSpec((tm, LANES), lambda i: (i, 0)), _full((1, D_MODEL))],
        out_specs=pl.BlockSpec((tm, D_MODEL), lambda i: (i, 0)),
        out_shape=jax.ShapeDtypeStruct((n, D_MODEL), F32),
        compiler_params=_cparams(("arbitrary",)),
        name="moe_combine",
    )(x, y4, rgt, fg)


def _routing_tables(ridx, n):
    na = n * TOP_K
    tm = MOE_TILE
    n_blocks = -(-na // tm) + N_EXPERTS
    flat_e = ridx[:TOP_K, :].T.reshape(-1)
    order = jnp.argsort(flat_e).astype(jnp.int32)
    counts = jnp.sum(flat_e[None, :] == jnp.arange(N_EXPERTS, dtype=jnp.int32)[:, None], axis=1).astype(jnp.int32)
    padded = (counts + tm - 1) // tm * tm
    pad_end = jnp.cumsum(padded)
    pad_start = pad_end - padded
    start = jnp.cumsum(counts) - counts
    blk = jnp.arange(n_blocks, dtype=jnp.int32) * tm
    block_v = (blk < pad_end[-1]).astype(jnp.int32)
    block_e = jnp.minimum(jnp.searchsorted(pad_end, jnp.minimum(blk, pad_end[-1] - 1), side='right'),
                          N_EXPERTS - 1).astype(jnp.int32)
    slot = jnp.arange(n_blocks * tm, dtype=jnp.int32)
    se = jnp.repeat(block_e, tm)
    rank = slot - pad_start[se]
    real = (rank < counts[se]) & (jnp.repeat(block_v, tm) > 0)
    src = order[jnp.clip(start[se] + rank, 0, na - 1)]
    slot_tok = jnp.where(real, src // TOP_K, 0).astype(jnp.int32)
    spare = na + jnp.cumsum(jnp.logical_not(real).astype(jnp.int32)) - 1
    slot_pos = jnp.where(real, src, spare).astype(jnp.int32)
    return block_e, block_v, slot_tok.reshape(n_blocks, 1, tm), slot_pos.reshape(n_blocks, 1, tm)


def _blockdiag(m):
    g, a, b = m.shape
    eye = jnp.eye(g, dtype=m.dtype)
    return (eye[:, None, :, None] * m[:, :, None, :]).reshape(g * a, g * b)


def _layer_params(l, p):
    lam = jnp.zeros((SUBLANES, S5_FLAT), F32)
    lam = lam.at[0].set(p['s5_lambda_re'][l].reshape(-1)).at[1].set(p['s5_lambda_im'][l].reshape(-1))
    lam = lam.at[2].set(jnp.repeat(p['s5_log_step'][l], S5_STATE))
    gbias = jnp.zeros((1, LANES), F32).at[0, :2 * ML_HEADS].set(p['ml_gate_bias'][l])
    w_in = jnp.pad(p['w_in'][l], ((0, 0), (0, N_IN_PAD - N_IN))).astype(BF16)
    return dict(
        norm1_g=p['norm1_g'][l].reshape(1, -1), w_in=w_in, s5_lam=lam,
        s5_bre=_blockdiag(p['s5_b_re'][l].transpose(0, 2, 1)), s5_bim=_blockdiag(p['s5_b_im'][l].transpose(0, 2, 1)),
        s5_cre=_blockdiag(p['s5_c_re'][l].transpose(0, 2, 1)).astype(BF16),
        s5_cim=_blockdiag(p['s5_c_im'][l].transpose(0, 2, 1)).astype(BF16),
        s5_d=p['s5_d'][l].reshape(1, -1), s5_gw=p['s5_glu_w'][l].astype(BF16), s5_gb=p['s5_glu_b'][l].reshape(1, -1),
        rg_cw=p['rg_conv_w'][l], rg_cb=p['rg_conv_b'][l].reshape(1, -1),
        rg_wa=_blockdiag(p['rg_wa'][l]).astype(BF16), rg_ba=p['rg_ba'][l].reshape(1, -1),
        rg_wx=_blockdiag(p['rg_wx'][l]).astype(BF16), rg_bx=p['rg_bx'][l].reshape(1, -1),
        rg_lam=p['rg_lambda'][l].reshape(1, -1),
        ml_gb=gbias, ml_ng=p['ml_norm_g'][l].reshape(1, -1),
        w_out=p['w_out'][l].astype(BF16), norm2_g=p['norm2_g'][l].reshape(1, -1),
        router_w=p['router_w'][l].T.astype(BF16), router_b=p['router_b'][l].reshape(-1, 1),
        w1=p['exp_w1'][l], b1=p['exp_b1'][l].reshape(N_EXPERTS, 1, -1),
        w2=p['exp_w2'][l], b2=p['exp_b2'][l].reshape(N_EXPERTS, 1, -1),
    )


def _trunk(x, states, params, final_g, *, t_valid, s5_tc, ml_bb, ml_cl):
    nb, t, _ = x.shape
    n = nb * t
    s5r, s5i, rgh, rgc, mlc, mln, mlm = states
    outs = [[] for _ in range(7)]
    xf = x.reshape(n, D_MODEL)
    for l in range(DEPTH):
        lp = _layer_params(l, params)
        zs5, zrg, zml, zgt = _inproj(xf, lp['norm1_g'], lp['w_in'])
        y1, nr, ni = _s5(zs5.reshape(nb, t, -1), s5r[l].reshape(nb, -1), s5i[l].reshape(nb, -1), lp['s5_lam'],
                         lp['s5_bre'], lp['s5_bim'], lp['s5_cre'], lp['s5_cim'], lp['s5_d'], lp['s5_gw'], lp['s5_gb'],
                         tc=s5_tc, t_valid=t_valid)
        y2, nh, ncv = _rglru(zrg.reshape(nb, t, -1), rgh[l], rgc[l], lp['rg_cw'], lp['rg_cb'], lp['rg_wa'],
                             lp['rg_ba'], lp['rg_wx'], lp['rg_bx'], lp['rg_lam'], tc=s5_tc, t_valid=t_valid)
        y3, nc_, nn_, nm_ = _mlstm(zml.reshape(nb, t, -1), zgt.reshape(nb, t, -1), lp['ml_gb'], lp['ml_ng'],
                                   mlc[l], mln[l], mlm[l].reshape(nb, 1, ML_HEADS),
                                   bb=ml_bb, cl=ml_cl, t_valid=t_valid)
        xn, hn, ridx, rgt = _outproj(xf, y1.reshape(n, -1), y2.reshape(n, -1), y3.reshape(n, -1),
                                     lp['w_out'], lp['norm2_g'], lp['router_w'], lp['router_b'])
        block_e, block_v, slot_tok, slot_pos = _routing_tables(ridx, n)
        y4 = _moe(block_e, block_v, slot_tok, slot_pos, hn, lp['w1'], lp['b1'], lp['w2'], lp['b2'])
        xf = _combine(xn, y4.reshape(-1, TOP_K * D_MODEL), rgt, final_g.reshape(1, -1), final=(l == DEPTH - 1))
        for lst, val in zip(outs, (nr.reshape(nb, S5_GROUPS, S5_STATE), ni.reshape(nb, S5_GROUPS, S5_STATE),
                                   nh, ncv, nc_, nn_, nm_.reshape(nb, ML_HEADS))):
            lst.append(val)
    return xf.reshape(nb, t, D_MODEL), [jnp.stack(s) for s in outs]


def kernel(x_prompt, x_sample, state_s5_re, state_s5_im, state_rg_h, state_rg_conv, state_ml_c, state_ml_n, state_ml_m, norm1_g, w_in, s5_lambda_re, s5_lambda_im, s5_log_step, s5_b_re, s5_b_im, s5_c_re, s5_c_im, s5_d, s5_glu_w, s5_glu_b, rg_conv_w, rg_conv_b, rg_wa, rg_ba, rg_wx, rg_bx, rg_lambda, ml_gate_bias, ml_norm_g, w_out, norm2_g, router_w, router_b, exp_w1, exp_b1, exp_w2, exp_b2, final_norm_g):
    params = dict(norm1_g=norm1_g, w_in=w_in, s5_lambda_re=s5_lambda_re, s5_lambda_im=s5_lambda_im,
                  s5_log_step=s5_log_step, s5_b_re=s5_b_re, s5_b_im=s5_b_im, s5_c_re=s5_c_re, s5_c_im=s5_c_im,
                  s5_d=s5_d, s5_glu_w=s5_glu_w, s5_glu_b=s5_glu_b, rg_conv_w=rg_conv_w, rg_conv_b=rg_conv_b,
                  rg_wa=rg_wa, rg_ba=rg_ba, rg_wx=rg_wx, rg_bx=rg_bx, rg_lambda=rg_lambda,
                  ml_gate_bias=ml_gate_bias, ml_norm_g=ml_norm_g, w_out=w_out, norm2_g=norm2_g,
                  router_w=router_w, router_b=router_b, exp_w1=exp_w1, exp_b1=exp_b1, exp_w2=exp_w2, exp_b2=exp_b2)
    bp, tp, _ = x_prompt.shape
    bs, ts, _ = x_sample.shape
    assert tp >= RG_CONV - 1 and ts >= RG_CONV - 1
    zeros = lambda *shape: jnp.zeros((DEPTH, bp) + shape, F32)
    prompt_states = (zeros(S5_GROUPS, S5_STATE), zeros(S5_GROUPS, S5_STATE), zeros(RG_WIDTH),
                     zeros(RG_CONV - 1, RG_WIDTH), zeros(ML_HEADS, ML_DK, ML_DV), zeros(ML_HEADS, ML_DK),
                     zeros(ML_HEADS))
    y_prompt, sp = _trunk(x_prompt, prompt_states, params, final_norm_g,
                          t_valid=tp, s5_tc=128, ml_bb=1, ml_cl=math.gcd(tp, ML_CHUNK))
    ts_pad = -(-ts // SUBLANES) * SUBLANES
    xs_pad = jnp.pad(x_sample, ((0, 0), (0, ts_pad - ts), (0, 0)))
    sample_states = (state_s5_re, state_s5_im, state_rg_h, state_rg_conv, state_ml_c, state_ml_n, state_ml_m)
    y_sample, ss = _trunk(xs_pad, sample_states, params, final_norm_g,
                          t_valid=ts, s5_tc=ts_pad, ml_bb=8, ml_cl=ts_pad)
    return (y_prompt, y_sample[:, :ts], *sp, *ss)
```

```python
import functools
import math

import jax
import jax.numpy as jnp
from jax import lax
from jax.experimental import pallas as pl
from jax.experimental.pallas import tpu as pltpu

F32 = jnp.float32
BF16 = jnp.bfloat16

SUBLANES = 8
LANES = 128
VMEM_LIMIT_BYTES = 56 * 1024 * 1024

D_MODEL = 1024
DEPTH = 2
S5_WIDTH = 256
S5_GROUP = 16
S5_GROUPS = 16
S5_STATE = 64
S5_FLAT = S5_GROUPS * S5_STATE
RG_WIDTH = 256
RG_BLOCKS = 8
RG_CONV = 4
RG_C = 8.0
ML_WIDTH = 512
ML_HEADS = 4
ML_DK = 128
ML_DV = 128
ML_CHUNK = 64
N_EXPERTS = 32
TOP_K = 4
D_FF = 1024
SWIGLU_LIMIT = 7.0
SWIGLU_ALPHA = 1.702
EPS = 1e-5

IN_S5 = (0, 256)
IN_RG = (256, 768)
IN_ML = (768, 2816)
IN_GATE = (2816, 2944)
N_IN = 2824
N_IN_PAD = 2944

ROW_TILE = 512
ROUTE_TILE = 128
MOE_TILE = 256
NEG_BIG = -1e30


def _cparams(sem):
    return pltpu.CompilerParams(dimension_semantics=sem, vmem_limit_bytes=VMEM_LIMIT_BYTES)


def _full(shape):
    n = len(shape)
    return pl.BlockSpec(shape, lambda *_: (0,) * n)


ROW_TILES = D_MODEL // LANES
assert ROW_TILES == SUBLANES


def _rows_to_tiles(x, dst_ref):
    m = x.shape[0]
    for j in range(ROW_TILES):
        dst_ref[pl.ds(j, m, stride=ROW_TILES), :] = x[:, j * LANES:(j + 1) * LANES]


def _tiles_to_rows(src_ref):
    m = src_ref.shape[0] // ROW_TILES
    return jnp.concatenate([src_ref[pl.ds(j, m, stride=ROW_TILES), :] for j in range(ROW_TILES)], axis=-1)


def _inproj_body(x_ref, g_ref, w_ref, zs5_ref, zrg_ref, zml_ref, zgt_ref):
    x = x_ref[...]
    h = x * lax.rsqrt(jnp.mean(x * x, axis=-1, keepdims=True) + EPS) * g_ref[...]
    hb = h.astype(BF16)
    for ref, (lo, hi) in ((zs5_ref, IN_S5), (zrg_ref, IN_RG), (zml_ref, IN_ML), (zgt_ref, IN_GATE)):
        ref[...] = jnp.dot(hb, w_ref[:, lo:hi], preferred_element_type=F32)


def _inproj(x, g, w):
    n = x.shape[0]
    tm = ROW_TILE
    widths = [hi - lo for lo, hi in (IN_S5, IN_RG, IN_ML, IN_GATE)]
    return pl.pallas_call(
        _inproj_body,
        grid=(n // tm,),
        in_specs=[pl.BlockSpec((tm, D_MODEL), lambda i: (i, 0)), _full((1, D_MODEL)), _full((D_MODEL, N_IN_PAD))],
        out_specs=[pl.BlockSpec((tm, w_), lambda i: (i, 0)) for w_ in widths],
        out_shape=[jax.ShapeDtypeStruct((n, w_), F32) for w_ in widths],
        compiler_params=_cparams(("arbitrary",)),
        name="inproj",
    )(x, g, w)


def _s5_body(u_ref, s0r_ref, s0i_ref, lam_ref, bre_ref, bim_ref, cre_ref, cim_ref, d_ref, gw_ref, gb_ref,
             y_ref, sr_ref, si_ref, bur, bui, ab, bbr, bbi, tm_s, bm_s, *, nb, tc, nc, t_valid):
    c = pl.program_id(0)

    @pl.when(c == 0)
    def _():
        sr_ref[...] = s0r_ref[...]
        si_ref[...] = s0i_ref[...]
        lr = lam_ref[0:1, :]
        li = lam_ref[1:2, :]
        step = jnp.exp(lam_ref[2:3, :])
        mag = jnp.exp(lr * step)
        ab_re = mag * jnp.cos(li * step)
        ab_im = mag * jnp.sin(li * step)
        den = lr * lr + li * li
        num_re = ab_re - 1.0
        z_re = (num_re * lr + ab_im * li) / den
        z_im = (ab_im * lr - num_re * li) / den
        ab[0:1, :] = ab_re
        ab[1:2, :] = ab_im
        bbr[...] = (z_re * bre_ref[...] - z_im * bim_ref[...]).astype(BF16)
        bbi[...] = (z_re * bim_ref[...] + z_im * bre_ref[...]).astype(BF16)

    _to_time_major(u_ref, tm_s, bm_s, nb, tc)
    u = jnp.concatenate([tm_s[j] for j in range(S5_WIDTH // LANES)], axis=-1)
    ub = u.astype(BF16)
    bur[...] = jnp.dot(ub, bbr[...], preferred_element_type=F32)
    bui[...] = jnp.dot(ub, bbi[...], preferred_element_type=F32)

    a_re = jnp.broadcast_to(ab[0:1, :], (SUBLANES, S5_FLAT))
    a_im = jnp.broadcast_to(ab[1:2, :], (SUBLANES, S5_FLAT))
    last_in = t_valid - (nc - 1) * tc
    steps = tc if last_in == tc else jnp.where(c == nc - 1, last_in, tc)

    def group(gi, _):
        g8 = gi * SUBLANES
        rows8 = pl.ds(pl.multiple_of(g8, SUBLANES), SUBLANES)

        def step_fn(t, carry):
            s_re, s_im = carry
            rows = pl.ds(pl.multiple_of(t * nb + g8, SUBLANES), SUBLANES)
            n_re = a_re * s_re - a_im * s_im + bur[rows, :]
            n_im = a_re * s_im + a_im * s_re + bui[rows, :]
            bur[rows, :] = n_re
            bui[rows, :] = n_im
            return n_re, n_im

        s_re, s_im = lax.fori_loop(0, steps, step_fn, (sr_ref[rows8, :], si_ref[rows8, :]))
        sr_ref[rows8, :] = s_re
        si_ref[rows8, :] = s_im
        return 0

    lax.fori_loop(0, nb // SUBLANES, group, 0)

    y = (jnp.dot(bur[...].astype(BF16), cre_ref[...], preferred_element_type=F32)
         - jnp.dot(bui[...].astype(BF16), cim_ref[...], preferred_element_type=F32))
    y = y + d_ref[...] * u
    g = jax.nn.gelu(y)
    gate = jax.nn.sigmoid(jnp.dot(g.astype(BF16), gw_ref[...], preferred_element_type=F32) + gb_ref[...])
    out = g * gate
    for j in range(S5_WIDTH // LANES):
        tm_s[j] = out[:, j * LANES:(j + 1) * LANES]
    _to_batch_major(tm_s, bm_s, y_ref, nb, tc)


def _to_time_major(src_ref, tm_s, bm_s, nb, tc):
    for j in range(tm_s.shape[0]):
        lanes = slice(j * LANES, (j + 1) * LANES)
        if nb <= tc:
            for b in range(nb):
                tm_s[j, pl.ds(b, tc, stride=nb), :] = src_ref[b, :, lanes]
        else:
            bm_s[j] = src_ref[:, :, lanes].reshape(nb * tc, LANES)
            for t in range(tc):
                tm_s[j, t * nb:(t + 1) * nb, :] = bm_s[j, pl.ds(t, nb, stride=tc), :]


def _to_batch_major(tm_s, bm_s, dst_ref, nb, tc):
    for j in range(tm_s.shape[0]):
        lanes = slice(j * LANES, (j + 1) * LANES)
        if nb <= tc:
            for b in range(nb):
                dst_ref[b, :, lanes] = tm_s[j, pl.ds(b, tc, stride=nb), :]
        else:
            for t in range(tc):
                bm_s[j, pl.ds(t, nb, stride=tc), :] = tm_s[j, t * nb:(t + 1) * nb, :]
            dst_ref[:, :, lanes] = bm_s[j].reshape(nb, tc, LANES)


def _s5(u, s0r, s0i, lam, bre, bim, cre, cim, d, gw, gb, *, tc, t_valid):
    nb, t, _ = u.shape
    nc = t // tc
    assert (nc - 1) * tc < t_valid <= t
    body = functools.partial(_s5_body, nb=nb, tc=tc, nc=nc, t_valid=t_valid)
    return pl.pallas_call(
        body,
        grid=(t // tc,),
        in_specs=[pl.BlockSpec((nb, tc, S5_WIDTH), lambda c: (0, c, 0)),
                  _full((nb, S5_FLAT)), _full((nb, S5_FLAT)), _full((SUBLANES, S5_FLAT)),
                  _full((S5_WIDTH, S5_FLAT)), _full((S5_WIDTH, S5_FLAT)),
                  _full((S5_FLAT, S5_WIDTH)), _full((S5_FLAT, S5_WIDTH)),
                  _full((1, S5_WIDTH)), _full((S5_WIDTH, S5_WIDTH)), _full((1, S5_WIDTH))],
        out_specs=[pl.BlockSpec((nb, tc, S5_WIDTH), lambda c: (0, c, 0)),
                   _full((nb, S5_FLAT)), _full((nb, S5_FLAT))],
        out_shape=[jax.ShapeDtypeStruct((nb, t, S5_WIDTH), F32),
                   jax.ShapeDtypeStruct((nb, S5_FLAT), F32), jax.ShapeDtypeStruct((nb, S5_FLAT), F32)],
        scratch_shapes=[pltpu.VMEM((nb * tc, S5_FLAT), F32), pltpu.VMEM((nb * tc, S5_FLAT), F32),
                        pltpu.VMEM((SUBLANES, S5_FLAT), F32),
                        pltpu.VMEM((S5_WIDTH, S5_FLAT), BF16), pltpu.VMEM((S5_WIDTH, S5_FLAT), BF16),
                        pltpu.VMEM((S5_WIDTH // LANES, nb * tc, LANES), F32),
                        pltpu.VMEM((S5_WIDTH // LANES, nb * tc, LANES), F32)],
        compiler_params=_cparams(("arbitrary",)),
        name="s5_scan",
    )(u, s0r, s0i, lam, bre, bim, cre, cim, d, gw, gb)


def _rglru_body(z_ref, h0_ref, cb0_ref, cw_ref, cb_ref, wa_ref, ba_ref, wx_ref, bx_ref, lam_ref,
                y_ref, h_ref, cbn_ref, xs, a_s, b_s, tm_s, bm_s, *, nb, tc, nc, t_valid):
    c = pl.program_id(0)
    hist = RG_CONV - 1

    @pl.when(c == 0)
    def _():
        h_ref[...] = h0_ref[...]
        xs[0:hist * nb, :] = cb0_ref[...]

    _to_time_major(z_ref, tm_s, bm_s, nb, tc)
    xs[hist * nb:(hist + tc) * nb, :] = jnp.concatenate([tm_s[j] for j in range(RG_WIDTH // LANES)], axis=-1)

    xc = jnp.zeros((tc * nb, RG_WIDTH), F32) + cb_ref[...]
    for j in range(RG_CONV):
        xc = xc + xs[j * nb:(j + tc) * nb, :] * cw_ref[j:j + 1, :]

    last_in = t_valid - (nc - 1) * tc
    last = tc if last_in == tc else jnp.where(c == nc - 1, last_in, tc)

    @pl.when(c == nc - 1)
    def _():
        cbn_ref[...] = xs[last_in * nb:(last_in + hist) * nb, :]

    xs[0:hist * nb, :] = xs[tc * nb:(tc + hist) * nb, :]

    xb = xc.astype(BF16)
    r = jax.nn.sigmoid(jnp.dot(xb, wa_ref[...], preferred_element_type=F32) + ba_ref[...])
    i = jax.nn.sigmoid(jnp.dot(xb, wx_ref[...], preferred_element_type=F32) + bx_ref[...])
    log_a = -RG_C * r * jax.nn.softplus(-lam_ref[...])
    a_s[...] = jnp.exp(log_a)
    th = jnp.tanh(log_a)
    b_s[...] = jnp.sqrt(-2.0 * th / (1.0 - th)) * (i * xc)

    def group(gi, _):
        g8 = gi * SUBLANES
        rows8 = pl.ds(pl.multiple_of(g8, SUBLANES), SUBLANES)

        def step_fn(t, h):
            rows = pl.ds(pl.multiple_of(t * nb + g8, SUBLANES), SUBLANES)
            h = a_s[rows, :] * h + b_s[rows, :]
            b_s[rows, :] = h
            return h

        h_ref[rows8, :] = lax.fori_loop(0, last, step_fn, h_ref[rows8, :])
        return 0

    lax.fori_loop(0, nb // SUBLANES, group, 0)
    for j in range(RG_WIDTH // LANES):
        tm_s[j] = b_s[:, j * LANES:(j + 1) * LANES]
    _to_batch_major(tm_s, bm_s, y_ref, nb, tc)
    y_ref[...] = y_ref[...] * jax.nn.gelu(z_ref[:, :, RG_WIDTH:2 * RG_WIDTH])


def _rglru(z, h0, cb0, cw, cb, wa, ba, wx, bx, lam, *, tc, t_valid):
    nb, t, _ = z.shape
    hist = RG_CONV - 1
    nc = t // tc
    assert (nc - 1) * tc + hist <= t_valid <= t
    body = functools.partial(_rglru_body, nb=nb, tc=tc, nc=nc, t_valid=t_valid)
    y, h, cbn = pl.pallas_call(
        body,
        grid=(t // tc,),
        in_specs=[pl.BlockSpec((nb, tc, 2 * RG_WIDTH), lambda c: (0, c, 0)),
                  _full((nb, RG_WIDTH)), _full((hist * nb, RG_WIDTH)),
                  _full((RG_CONV, RG_WIDTH)), _full((1, RG_WIDTH)),
                  _full((RG_WIDTH, RG_WIDTH)), _full((1, RG_WIDTH)),
                  _full((RG_WIDTH, RG_WIDTH)), _full((1, RG_WIDTH)), _full((1, RG_WIDTH))],
        out_specs=[pl.BlockSpec((nb, tc, RG_WIDTH), lambda c: (0, c, 0)),
                   _full((nb, RG_WIDTH)), _full((hist * nb, RG_WIDTH))],
        out_shape=[jax.ShapeDtypeStruct((nb, t, RG_WIDTH), F32),
                   jax.ShapeDtypeStruct((nb, RG_WIDTH), F32), jax.ShapeDtypeStruct((hist * nb, RG_WIDTH), F32)],
        scratch_shapes=[pltpu.VMEM(((tc + hist) * nb, RG_WIDTH), F32),
                        pltpu.VMEM((nb * tc, RG_WIDTH), F32), pltpu.VMEM((nb * tc, RG_WIDTH), F32),
                        pltpu.VMEM((RG_WIDTH // LANES, nb * tc, LANES), F32),
                        pltpu.VMEM((RG_WIDTH // LANES, nb * tc, LANES), F32)],
        compiler_params=_cparams(("arbitrary",)),
        name="rglru_scan",
    )(z, h0, cb0.transpose(1, 0, 2).reshape(hist * nb, RG_WIDTH), cw, cb, wa, ba, wx, bx, lam)
    return y, h, cbn.reshape(hist, nb, RG_WIDTH).transpose(1, 0, 2)


def _dot_nt(a, b, **kw):
    return lax.dot_general(a, b, (((1,), (1,)), ((), ())), preferred_element_type=F32, **kw)


def _dot_tn(a, b, **kw):
    return lax.dot_general(a, b, (((0,), (0,)), ((), ())), preferred_element_type=F32, **kw)


def _mlstm_body(q_ref, k_ref, v_ref, o_ref, g_ref, gb_ref, ng_ref, c0_ref, n0_ref, m0_ref,
                y_ref, cn_ref, nn_ref, mn_ref, *, bb, cl, t_valid):
    c = pl.program_id(1)

    @pl.when(c == 0)
    def _():
        cn_ref[...] = c0_ref[...]
        nn_ref[...] = n0_ref[...]
        mn_ref[...] = m0_ref[...]

    row = lax.broadcasted_iota(jnp.int32, (cl, cl), 0)
    col = lax.broadcasted_iota(jnp.int32, (cl, cl), 1)
    tril = (col <= row).astype(F32)
    keep = (col <= row) & (col + c * cl < t_valid)
    t_ok = (lax.broadcasted_iota(jnp.int32, (cl, LANES), 0) + c * cl) < t_valid
    lane = lax.broadcasted_iota(jnp.int32, (cl, LANES), 1)
    ones = jnp.ones((cl, LANES), F32)

    def one_batch(bi, _):
        graw = g_ref[bi] + gb_ref[...]
        lf = jnp.where(t_ok, jax.nn.log_sigmoid(graw), 0.0)
        bcum = jnp.dot(tril, lf, preferred_element_type=F32, precision=lax.Precision.HIGHEST)
        ig = jnp.where(t_ok, graw, NEG_BIG)
        q_all, k_all, v_all, o_all = q_ref[bi], k_ref[bi], v_ref[bi], o_ref[bi]
        c_all, n_all, m_all = cn_ref[bi], nn_ref[bi], mn_ref[bi]
        results = []
        for h in range(ML_HEADS):
            hs = slice(h * ML_DK, (h + 1) * ML_DK)
            qf = q_all[:, hs]
            qh = qf.astype(BF16)
            kh = k_all[:, hs] * (ML_DK ** -0.5)
            vh = v_all[:, hs].astype(BF16)
            b_col = bcum[:, ML_HEADS + h:ML_HEADS + h + 1]
            i_col = ig[:, h:h + 1]
            w_col = i_col - b_col
            w_row = _dot_nt(ones, jnp.where(lane == h, w_col, 0.0), precision=lax.Precision.HIGHEST)
            m_old = m_all[:, h:h + 1]
            dmat = jnp.where(keep, b_col + w_row, -jnp.inf)
            inter = b_col + m_old
            mt = jnp.maximum(inter, jnp.max(dmat, axis=-1, keepdims=True))
            s = _dot_nt(qh, kh.astype(BF16)) * jnp.exp(dmat - mt)
            sc = jnp.exp(inter - mt)
            cm = c_all[h]
            nv = n_all[h:h + 1, :]
            num = (jnp.dot(s.astype(BF16), vh, preferred_element_type=F32)
                   + sc * jnp.dot(qh, cm.astype(BF16), preferred_element_type=F32))
            qn = jnp.sum(qf * nv, axis=-1, keepdims=True)
            den = jnp.sum(s, axis=-1, keepdims=True) + sc * qn
            hh = num / jnp.maximum(jnp.abs(den), jnp.exp(-mt))
            bl = b_col[cl - 1:cl, :]
            m_new = mt[cl - 1:cl, :]
            ws = jnp.exp(bl - b_col + i_col - m_new)
            decay = jnp.exp(bl + m_old - m_new)
            wk = ws * kh
            c_new = decay * cm + _dot_tn(wk.astype(BF16), vh)
            n_new = decay * nv + jnp.sum(wk, axis=0, keepdims=True)
            hn = hh * lax.rsqrt(jnp.mean(hh * hh, axis=-1, keepdims=True) + EPS) * ng_ref[:, hs]
            results.append((c_new, n_new, m_new, hn * jax.nn.sigmoid(o_all[:, hs])))
        for h, (c_new, n_new, m_new, y_h) in enumerate(results):
            hs = slice(h * ML_DK, (h + 1) * ML_DK)
            cn_ref[bi, h] = c_new
            nn_ref[bi, h:h + 1, :] = n_new
            mn_ref[bi, :, h:h + 1] = m_new
            y_ref[bi, :, hs] = y_h
        return 0

    if bb == 1:
        one_batch(0, 0)
    else:
        lax.fori_loop(0, bb, one_batch, 0)


def _mlstm(zml, zgt, gbias, ng, c0, n0, m0, *, bb, cl, t_valid):
    nb, t, _ = zml.shape
    nc = t // cl
    body = functools.partial(_mlstm_body, bb=bb, cl=cl, t_valid=t_valid)
    zspec = lambda j: pl.BlockSpec((bb, cl, ML_WIDTH), lambda i, c, j=j: (i, c, j))
    return pl.pallas_call(
        body,
        grid=(nb // bb, nc),
        in_specs=[zspec(0), zspec(1), zspec(2), zspec(3),
                  pl.BlockSpec((bb, cl, LANES), lambda i, c: (i, c, 0)),
                  _full((1, LANES)), _full((1, ML_WIDTH)),
                  pl.BlockSpec((bb, ML_HEADS, ML_DK, ML_DV), lambda i, c: (i, 0, 0, 0)),
                  pl.BlockSpec((bb, ML_HEADS, ML_DK), lambda i, c: (i, 0, 0)),
                  pl.BlockSpec((bb, 1, ML_HEADS), lambda i, c: (i, 0, 0))],
        out_specs=[pl.BlockSpec((bb, cl, ML_WIDTH), lambda i, c: (i, c, 0)),
                   pl.BlockSpec((bb, ML_HEADS, ML_DK, ML_DV), lambda i, c: (i, 0, 0, 0)),
                   pl.BlockSpec((bb, ML_HEADS, ML_DK), lambda i, c: (i, 0, 0)),
                   pl.BlockSpec((bb, 1, ML_HEADS), lambda i, c: (i, 0, 0))],
        out_shape=[jax.ShapeDtypeStruct((nb, t, ML_WIDTH), F32),
                   jax.ShapeDtypeStruct((nb, ML_HEADS, ML_DK, ML_DV), F32),
                   jax.ShapeDtypeStruct((nb, ML_HEADS, ML_DK), F32),
                   jax.ShapeDtypeStruct((nb, 1, ML_HEADS), F32)],
        compiler_params=_cparams(("arbitrary", "arbitrary")),
        name="mlstm_chunks",
    )(zml, zml, zml, zml, zgt, gbias, ng, c0, n0, m0)


def _outproj_body(x_ref, y1_ref, y2_ref, y3_ref, wo_ref, g2_ref, rw_ref, rb_ref,
                  xn_ref, hn_ref, ridx_ref, rgt_ref):
    acc = jnp.dot(y1_ref[...].astype(BF16), wo_ref[0:256, :], preferred_element_type=F32)
    acc = acc + jnp.dot(y2_ref[...].astype(BF16), wo_ref[256:512, :], preferred_element_type=F32)
    acc = acc + jnp.dot(y3_ref[...].astype(BF16), wo_ref[512:1024, :], preferred_element_type=F32)
    x = x_ref[...] + acc
    xn_ref[...] = x
    h = x * lax.rsqrt(jnp.mean(x * x, axis=-1, keepdims=True) + EPS) * g2_ref[...]
    _rows_to_tiles(h, hn_ref)
    hb = h.astype(BF16)
    tm = x.shape[0]
    eidx = lax.broadcasted_iota(jnp.int32, (N_EXPERTS, ROUTE_TILE), 0)
    sub = lax.broadcasted_iota(jnp.int32, (LANES, ROUTE_TILE), 0)
    for j in range(tm // ROUTE_TILE):
        rows = slice(j * ROUTE_TILE, (j + 1) * ROUTE_TILE)
        vals = _dot_nt(rw_ref[...], hb[rows, :]) + rb_ref[...]
        tops, idxs = [], []
        for _ in range(TOP_K):
            m = jnp.max(vals, axis=0, keepdims=True)
            idx = jnp.min(jnp.where(vals == m, eidx, N_EXPERTS), axis=0, keepdims=True)
            vals = jnp.where(eidx == idx, -jnp.inf, vals)
            tops.append(m)
            idxs.append(idx)
        exps = [jnp.exp(t - tops[0]) for t in tops]
        inv = 1.0 / functools.reduce(lambda a, b: a + b, exps)
        ridx_ref[:, rows] = jnp.concatenate(idxs + idxs, axis=0)
        gt = jnp.zeros((LANES, ROUTE_TILE), F32)
        for kk in range(TOP_K):
            gt = jnp.where(sub == kk, exps[kk] * inv, gt)
        rgt_ref[rows, :] = gt.T


def _outproj(x, y1, y2, y3, wo, g2, rw, rb):
    n = x.shape[0]
    tm = ROW_TILE
    rowspec = lambda w_: pl.BlockSpec((tm, w_), lambda i: (i, 0))
    return pl.pallas_call(
        _outproj_body,
        grid=(n // tm,),
        in_specs=[rowspec(D_MODEL), rowspec(S5_WIDTH), rowspec(RG_WIDTH), rowspec(ML_WIDTH),
                  _full((D_MODEL, D_MODEL)), _full((1, D_MODEL)),
                  _full((N_EXPERTS, D_MODEL)), _full((N_EXPERTS, 1))],
        out_specs=[rowspec(D_MODEL), pl.BlockSpec((tm * ROW_TILES, LANES), lambda i: (i, 0)),
                   pl.BlockSpec((2 * TOP_K, tm), lambda i: (0, i)), rowspec(LANES)],
        out_shape=[jax.ShapeDtypeStruct((n, D_MODEL), F32), jax.ShapeDtypeStruct((n * ROW_TILES, LANES), F32),
                   jax.ShapeDtypeStruct((2 * TOP_K, n), jnp.int32), jax.ShapeDtypeStruct((n, LANES), F32)],
        compiler_params=_cparams(("arbitrary",)),
        name="outproj_route",
    )(x, y1, y2, y3, wo, g2, rw, rb)


def _moe_body(be_ref, bv_ref, tokn_ref, posp_ref, posc_ref, hn_ref, w1_ref, b1_ref, w2_ref, b2_ref, y4_ref,
              w1b, w2b, xbuf0, xbuf1, obuf0, obuf1, gsem, ssem):
    i = pl.program_id(0)
    last = pl.num_programs(0) - 1
    real = bv_ref[i] > 0
    prev_real = bv_ref[jnp.maximum(i - 1, 0)] > 0
    fresh = jnp.logical_or(i == 1, be_ref[i] != be_ref[jnp.maximum(i - 1, 0)])
    xbufs, obufs = (xbuf0, xbuf1), (obuf0, obuf1)

    def step(s):
        x_cur, x_nxt, o_cur, o_prev = xbufs[s], xbufs[1 - s], obufs[s], obufs[1 - s]

        def tile(ref, row):
            return ref.at[pl.ds(pl.multiple_of(row, ROW_TILES), ROW_TILES)]

        def start_gathers():
            for r in range(MOE_TILE):
                pltpu.make_async_copy(tile(hn_ref, tokn_ref[0, 0, r]), tile(x_nxt, r * ROW_TILES),
                                      gsem.at[1 - s]).start()

        def start_scatters(pos_ref, o_buf, sem):
            for r in range(MOE_TILE):
                pltpu.make_async_copy(tile(o_buf, r * ROW_TILES), tile(y4_ref, pos_ref[0, 0, r]), sem).start()

        def wait_scatters(o_buf, sem):
            pltpu.make_async_copy(o_buf, y4_ref.at[pl.ds(0, MOE_TILE * ROW_TILES)], sem).wait()

        @pl.when(jnp.logical_or(i == 1, prev_real))
        def _():
            pltpu.make_async_copy(hn_ref.at[pl.ds(0, MOE_TILE * ROW_TILES)], x_cur, gsem.at[s]).wait()

        @pl.when(i >= 2)
        def _():
            wait_scatters(o_cur, ssem.at[s])

        @pl.when(jnp.logical_and(fresh, real))
        def _():
            w1b[...] = w1_ref[0, 0].astype(BF16)
            w2b[...] = w2_ref[0, 0].astype(BF16)

        @pl.when(real)
        def _():
            start_gathers()
            start_scatters(posp_ref, o_prev, ssem.at[1 - s])
            xb = _tiles_to_rows(x_cur).astype(BF16)
            hb = jnp.dot(xb, w1b[...], preferred_element_type=F32) + b1_ref[0, 0]
            g = jnp.minimum(hb[:, :D_FF], SWIGLU_LIMIT)
            u = jnp.clip(hb[:, D_FF:], -SWIGLU_LIMIT, SWIGLU_LIMIT)
            act = g * jax.nn.sigmoid(SWIGLU_ALPHA * g) * (u + 1.0)
            _rows_to_tiles(jnp.dot(act.astype(BF16), w2b[...], preferred_element_type=F32) + b2_ref[0, 0], o_cur)

        @pl.when(jnp.logical_not(real))
        def _():
            if s == 0:
                @pl.when(i == 0)
                def _():
                    obuf0[...] = jnp.zeros_like(obuf0)
                    obuf1[...] = jnp.zeros_like(obuf1)
                    start_gathers()

            @pl.when(i >= 1)
            def _():
                start_scatters(posp_ref, o_prev, ssem.at[1 - s])

        @pl.when(i == last)
        def _():
            start_scatters(posc_ref, o_cur, ssem.at[s])
            wait_scatters(o_prev, ssem.at[1 - s])
            wait_scatters(o_cur, ssem.at[s])

    @pl.when(lax.rem(i, 2) == 0)
    def _():
        step(0)

    @pl.when(lax.rem(i, 2) == 1)
    def _():
        step(1)


def _moe(layer, block_e, block_v, slot_tok, slot_pos, hn, w1, b1, w2, b2):
    n_blocks = block_e.shape[0]
    tm = MOE_TILE
    n_rows = n_blocks * tm
    assert n_blocks >= 3
    idx_spec = lambda f: pl.BlockSpec((1, 1, tm), f, memory_space=pltpu.SMEM)
    grid_spec = pltpu.PrefetchScalarGridSpec(
        num_scalar_prefetch=2,
        grid=(n_blocks,),
        in_specs=[idx_spec(lambda i, be, bv: (jnp.minimum(i + 1, n_blocks - 1), 0, 0)),
                  idx_spec(lambda i, be, bv: (jnp.maximum(i - 1, 0), 0, 0)),
                  idx_spec(lambda i, be, bv: (i, 0, 0)),
                  pl.BlockSpec(memory_space=pl.ANY),
                  pl.BlockSpec((1, 1, D_MODEL, 2 * D_FF), lambda i, be, bv: (layer, be[i], 0, 0)),
                  pl.BlockSpec((1, 1, 1, 2 * D_FF), lambda i, be, bv: (layer, be[i], 0, 0)),
                  pl.BlockSpec((1, 1, D_FF, D_MODEL), lambda i, be, bv: (layer, be[i], 0, 0)),
                  pl.BlockSpec((1, 1, 1, D_MODEL), lambda i, be, bv: (layer, be[i], 0, 0))],
        out_specs=pl.BlockSpec(memory_space=pl.ANY),
        scratch_shapes=[pltpu.VMEM((D_MODEL, 2 * D_FF), BF16), pltpu.VMEM((D_FF, D_MODEL), BF16)]
                       + [pltpu.VMEM((tm * ROW_TILES, LANES), F32)] * 4
                       + [pltpu.SemaphoreType.DMA((2,)), pltpu.SemaphoreType.DMA((2,))],
    )
    return pl.pallas_call(
        _moe_body,
        grid_spec=grid_spec,
        out_shape=jax.ShapeDtypeStruct((n_rows * ROW_TILES, LANES), F32),
        compiler_params=_cparams(("arbitrary",)),
        name="moe_experts",
    )(block_e, block_v, slot_tok, slot_pos, slot_pos, hn, w1, b1, w2, b2)


def _combine_body(x_ref, rgt_ref, fg_ref, *rest, final):
    y_refs, o_ref = rest[:TOP_K], rest[TOP_K]
    x = x_ref[...]
    for kk in range(TOP_K):
        x = x + rgt_ref[:, kk:kk + 1] * _tiles_to_rows(y_refs[kk])
    if final:
        x = x * lax.rsqrt(jnp.mean(x * x, axis=-1, keepdims=True) + EPS) * fg_ref[...]
    o_ref[...] = x


def _combine(x, y4, rgt, fg, *, tok0, n_all, final):
    n = x.shape[0]
    tm = ROW_TILE
    assert tok0 % tm == 0 and n_all % tm == 0
    yspec = lambda kk: pl.BlockSpec((tm * ROW_TILES, LANES), lambda i, kk=kk: ((kk * n_all + tok0) // tm + i, 0))
    return pl.pallas_call(
        functools.partial(_combine_body, final=final),
        grid=(n // tm,),
        in_specs=[pl.BlockSpec((tm, D_MODEL), lambda i: (i, 0)),
                  pl.BlockSpec((tm, LANES), lambda i: (i, 0)), _full((1, D_MODEL))]
                 + [yspec(kk) for kk in range(TOP_K)],
        out_specs=pl.BlockSpec((tm, D_MODEL), lambda i: (i, 0)),
        out_shape=jax.ShapeDtypeStruct((n, D_MODEL), F32),
        compiler_params=_cparams(("arbitrary",)),
        name="moe_combine",
    )(x, rgt, fg, *([y4] * TOP_K))


def _routing_tables(ridx, n):
    na = n * TOP_K
    tm = MOE_TILE
    assert na % tm == 0
    n_blocks = 1 + na // tm + N_EXPERTS
    n_slots = n_blocks * tm
    experts = jnp.arange(N_EXPERTS, dtype=jnp.int32)
    flat_e = ridx[:TOP_K, :].reshape(-1)
    order = jnp.argsort(flat_e).astype(jnp.int32)
    counts = jnp.sum(flat_e[None, :] == experts[:, None], axis=1).astype(jnp.int32)
    padded = (counts + tm - 1) // tm * tm
    pad_end = jnp.cumsum(padded)
    pad_start = pad_end - padded
    start = jnp.cumsum(counts) - counts
    spare_start = na + pad_start - start
    total = pad_end[-1]
    blk = (jnp.arange(n_blocks, dtype=jnp.int32) - 1) * tm
    block_v = ((blk >= 0) & (blk < total)).astype(jnp.int32)
    last_e = jnp.max(jnp.where(counts > 0, experts, 0))
    block_e = jnp.minimum(jnp.sum(blk[:, None] >= pad_end[None, :], axis=1), last_e).astype(jnp.int32)
    slot = jnp.arange(n_slots, dtype=jnp.int32) - tm
    se = jnp.repeat(block_e, tm)
    rank = slot - pad_start[se]
    used = (slot >= 0) & (slot < total)
    real = (rank < counts[se]) & used
    src = order[jnp.clip(start[se] + rank, 0, na - 1)]
    slot_tok = (jnp.where(real, src % n, 0) * ROW_TILES).astype(jnp.int32)
    spare = jnp.where(used, spare_start[se] + rank - counts[se], jnp.where(slot >= 0, slot, n_slots + slot))
    slot_pos = (jnp.where(real, src, spare) * ROW_TILES).astype(jnp.int32)
    return block_e, block_v, slot_tok.reshape(n_blocks, 1, tm), slot_pos.reshape(n_blocks, 1, tm)


def _blockdiag(m):
    g, a, b = m.shape
    eye = jnp.eye(g, dtype=m.dtype)
    return (eye[:, None, :, None] * m[:, :, None, :]).reshape(g * a, g * b)


def _layer_params(l, p):
    lam = jnp.zeros((SUBLANES, S5_FLAT), F32)
    lam = lam.at[0].set(p['s5_lambda_re'][l].reshape(-1)).at[1].set(p['s5_lambda_im'][l].reshape(-1))
    lam = lam.at[2].set(jnp.repeat(p['s5_log_step'][l], S5_STATE))
    gbias = jnp.zeros((1, LANES), F32).at[0, :2 * ML_HEADS].set(p['ml_gate_bias'][l])
    w_in = jnp.pad(p['w_in'][l], ((0, 0), (0, N_IN_PAD - N_IN))).astype(BF16)
    return dict(
        norm1_g=p['norm1_g'][l].reshape(1, -1), w_in=w_in, s5_lam=lam,
        s5_bre=_blockdiag(p['s5_b_re'][l].transpose(0, 2, 1)), s5_bim=_blockdiag(p['s5_b_im'][l].transpose(0, 2, 1)),
        s5_cre=_blockdiag(p['s5_c_re'][l].transpose(0, 2, 1)).astype(BF16),
        s5_cim=_blockdiag(p['s5_c_im'][l].transpose(0, 2, 1)).astype(BF16),
        s5_d=p['s5_d'][l].reshape(1, -1), s5_gw=p['s5_glu_w'][l].astype(BF16), s5_gb=p['s5_glu_b'][l].reshape(1, -1),
        rg_cw=p['rg_conv_w'][l], rg_cb=p['rg_conv_b'][l].reshape(1, -1),
        rg_wa=_blockdiag(p['rg_wa'][l]).astype(BF16), rg_ba=p['rg_ba'][l].reshape(1, -1),
        rg_wx=_blockdiag(p['rg_wx'][l]).astype(BF16), rg_bx=p['rg_bx'][l].reshape(1, -1),
        rg_lam=p['rg_lambda'][l].reshape(1, -1),
        ml_gb=gbias, ml_ng=p['ml_norm_g'][l].reshape(1, -1),
        w_out=p['w_out'][l].astype(BF16), norm2_g=p['norm2_g'][l].reshape(1, -1),
        router_w=p['router_w'][l].T.astype(BF16), router_b=p['router_b'][l].reshape(-1, 1),
    )


def _mix_and_project(l, lp, xf, states, cfg):
    nb, t, t_valid = cfg['nb'], cfg['t'], cfg['t_valid']
    n = nb * t
    s5r, s5i, rgh, rgc, mlc, mln, mlm = states
    zs5, zrg, zml, zgt = _inproj(xf, lp['norm1_g'], lp['w_in'])
    y1, nr, ni = _s5(zs5.reshape(nb, t, -1), s5r[l].reshape(nb, -1), s5i[l].reshape(nb, -1), lp['s5_lam'],
                     lp['s5_bre'], lp['s5_bim'], lp['s5_cre'], lp['s5_cim'], lp['s5_d'], lp['s5_gw'], lp['s5_gb'],
                     tc=cfg['s5_tc'], t_valid=t_valid)
    y2, nh, ncv = _rglru(zrg.reshape(nb, t, -1), rgh[l], rgc[l], lp['rg_cw'], lp['rg_cb'], lp['rg_wa'],
                         lp['rg_ba'], lp['rg_wx'], lp['rg_bx'], lp['rg_lam'], tc=cfg['s5_tc'], t_valid=t_valid)
    y3, nc_, nn_, nm_ = _mlstm(zml.reshape(nb, t, -1), zgt.reshape(nb, t, -1), lp['ml_gb'], lp['ml_ng'],
                               mlc[l], mln[l], mlm[l].reshape(nb, 1, ML_HEADS),
                               bb=cfg['ml_bb'], cl=cfg['ml_cl'], t_valid=t_valid)
    xn, hn, ridx, rgt = _outproj(xf, y1.reshape(n, -1), y2.reshape(n, -1), y3.reshape(n, -1),
                                 lp['w_out'], lp['norm2_g'], lp['router_w'], lp['router_b'])
    new_states = (nr.reshape(nb, S5_GROUPS, S5_STATE), ni.reshape(nb, S5_GROUPS, S5_STATE), nh, ncv, nc_, nn_,
                  nm_.reshape(nb, ML_HEADS))
    return xn, hn, ridx, rgt, new_states


def _trunks(xs, states, cfgs, params, final_g):
    n_all = sum(x.shape[0] for x in xs)
    tok0 = [sum(x.shape[0] for x in xs[:s]) for s in range(len(xs))]
    outs = [[[] for _ in range(7)] for _ in xs]
    b1 = params['exp_b1'].reshape(DEPTH, N_EXPERTS, 1, -1)
    b2 = params['exp_b2'].reshape(DEPTH, N_EXPERTS, 1, -1)
    for l in range(DEPTH):
        lp = _layer_params(l, params)
        mixed = [_mix_and_project(l, lp, xs[s], states[s], cfgs[s]) for s in range(len(xs))]
        hn_all = jnp.concatenate([m[1] for m in mixed], axis=0)
        ridx_all = jnp.concatenate([m[2] for m in mixed], axis=1)
        block_e, block_v, slot_tok, slot_pos = _routing_tables(ridx_all, n_all)
        y4 = _moe(l, block_e, block_v, slot_tok, slot_pos, hn_all, params['exp_w1'], b1, params['exp_w2'], b2)
        xs = [_combine(mixed[s][0], y4, mixed[s][3], final_g.reshape(1, -1), tok0=tok0[s], n_all=n_all,
                       final=(l == DEPTH - 1)) for s in range(len(xs))]
        for s in range(len(xs)):
            for lst, val in zip(outs[s], mixed[s][4]):
                lst.append(val)
    return xs, [[jnp.stack(v) for v in o] for o in outs]


def kernel(x_prompt, x_sample, state_s5_re, state_s5_im, state_rg_h, state_rg_conv, state_ml_c, state_ml_n, state_ml_m, norm1_g, w_in, s5_lambda_re, s5_lambda_im, s5_log_step, s5_b_re, s5_b_im, s5_c_re, s5_c_im, s5_d, s5_glu_w, s5_glu_b, rg_conv_w, rg_conv_b, rg_wa, rg_ba, rg_wx, rg_bx, rg_lambda, ml_gate_bias, ml_norm_g, w_out, norm2_g, router_w, router_b, exp_w1, exp_b1, exp_w2, exp_b2, final_norm_g):
    params = dict(norm1_g=norm1_g, w_in=w_in, s5_lambda_re=s5_lambda_re, s5_lambda_im=s5_lambda_im,
                  s5_log_step=s5_log_step, s5_b_re=s5_b_re, s5_b_im=s5_b_im, s5_c_re=s5_c_re, s5_c_im=s5_c_im,
                  s5_d=s5_d, s5_glu_w=s5_glu_w, s5_glu_b=s5_glu_b, rg_conv_w=rg_conv_w, rg_conv_b=rg_conv_b,
                  rg_wa=rg_wa, rg_ba=rg_ba, rg_wx=rg_wx, rg_bx=rg_bx, rg_lambda=rg_lambda,
                  ml_gate_bias=ml_gate_bias, ml_norm_g=ml_norm_g, w_out=w_out, norm2_g=norm2_g,
                  router_w=router_w, router_b=router_b, exp_w1=exp_w1, exp_b1=exp_b1, exp_w2=exp_w2, exp_b2=exp_b2)
    bp, tp, _ = x_prompt.shape
    bs, ts, _ = x_sample.shape
    assert tp >= RG_CONV - 1 and ts >= RG_CONV - 1
    zeros = lambda *shape: jnp.zeros((DEPTH, bp) + shape, F32)
    prompt_states = (zeros(S5_GROUPS, S5_STATE), zeros(S5_GROUPS, S5_STATE), zeros(RG_WIDTH),
                     zeros(RG_CONV - 1, RG_WIDTH), zeros(ML_HEADS, ML_DK, ML_DV), zeros(ML_HEADS, ML_DK),
                     zeros(ML_HEADS))
    ts_pad = -(-ts // SUBLANES) * SUBLANES
    xs_pad = jnp.pad(x_sample, ((0, 0), (0, ts_pad - ts), (0, 0)))
    sample_states = (state_s5_re, state_s5_im, state_rg_h, state_rg_conv, state_ml_c, state_ml_n, state_ml_m)
    cfgs = [dict(nb=bp, t=tp, t_valid=tp, s5_tc=128, ml_bb=1, ml_cl=math.gcd(tp, ML_CHUNK)),
            dict(nb=bs, t=ts_pad, t_valid=ts, s5_tc=ts_pad, ml_bb=8, ml_cl=ts_pad)]
    (yp, ys), (sp, ss) = _trunks([x_prompt.reshape(bp * tp, D_MODEL), xs_pad.reshape(bs * ts_pad, D_MODEL)],
                                 [prompt_states, sample_states], cfgs, params, final_norm_g)
    return (yp.reshape(bp, tp, D_MODEL), ys.reshape(bs, ts_pad, D_MODEL)[:, :ts], *sp, *ss)
```

```python
import functools
import math

import jax
import jax.numpy as jnp
from jax import lax
from jax.experimental import pallas as pl
from jax.experimental.pallas import tpu as pltpu

F32 = jnp.float32
BF16 = jnp.bfloat16

SUBLANES = 8
LANES = 128
VMEM_LIMIT_BYTES = 56 * 1024 * 1024

D_MODEL = 1024
DEPTH = 2
S5_WIDTH = 256
S5_GROUP = 16
S5_GROUPS = 16
S5_STATE = 64
S5_FLAT = S5_GROUPS * S5_STATE
RG_WIDTH = 256
RG_BLOCKS = 8
RG_CONV = 4
RG_C = 8.0
ML_WIDTH = 512
ML_HEADS = 4
ML_DK = 128
ML_DV = 128
ML_CHUNK = 64
N_EXPERTS = 32
TOP_K = 4
D_FF = 1024
SWIGLU_LIMIT = 7.0
SWIGLU_ALPHA = 1.702
EPS = 1e-5

IN_S5 = (0, 256)
IN_RG = (256, 768)
IN_ML = (768, 2816)
IN_GATE = (2816, 2944)
N_IN = 2824
N_IN_PAD = 2944

ROW_TILE = 512
ROUTE_TILE = 128
MOE_TILE = 256
NEG_BIG = -1e30


def _cparams(sem):
    return pltpu.CompilerParams(dimension_semantics=sem, vmem_limit_bytes=VMEM_LIMIT_BYTES)


def _full(shape):
    n = len(shape)
    return pl.BlockSpec(shape, lambda *_: (0,) * n)


ROW_TILES = D_MODEL // LANES
assert ROW_TILES == SUBLANES


def _rows_to_tiles(x, dst_ref):
    m = x.shape[0]
    for j in range(ROW_TILES):
        dst_ref[pl.ds(j, m, stride=ROW_TILES), :] = x[:, j * LANES:(j + 1) * LANES]


def _tiles_to_rows(src_ref):
    m = src_ref.shape[0] // ROW_TILES
    return jnp.concatenate([src_ref[pl.ds(j, m, stride=ROW_TILES), :] for j in range(ROW_TILES)], axis=-1)


def _inproj_body(x_ref, g_ref, w_ref, zs5_ref, zrg_ref, zml_ref, zgt_ref):
    x = x_ref[...]
    h = x * lax.rsqrt(jnp.mean(x * x, axis=-1, keepdims=True) + EPS) * g_ref[...]
    hb = h.astype(BF16)
    for ref, (lo, hi) in ((zs5_ref, IN_S5), (zrg_ref, IN_RG), (zml_ref, IN_ML), (zgt_ref, IN_GATE)):
        ref[...] = jnp.dot(hb, w_ref[:, lo:hi], preferred_element_type=F32)


def _inproj(x, g, w):
    n = x.shape[0]
    tm = ROW_TILE
    widths = [hi - lo for lo, hi in (IN_S5, IN_RG, IN_ML, IN_GATE)]
    return pl.pallas_call(
        _inproj_body,
        grid=(n // tm,),
        in_specs=[pl.BlockSpec((tm, D_MODEL), lambda i: (i, 0)), _full((1, D_MODEL)), _full((D_MODEL, N_IN_PAD))],
        out_specs=[pl.BlockSpec((tm, w_), lambda i: (i, 0)) for w_ in widths],
        out_shape=[jax.ShapeDtypeStruct((n, w_), F32) for w_ in widths],
        compiler_params=_cparams(("arbitrary",)),
        name="inproj",
    )(x, g, w)


def _s5_body(u_ref, s0r_ref, s0i_ref, lam_ref, bre_ref, bim_ref, cre_ref, cim_ref, d_ref, gw_ref, gb_ref,
             y_ref, sr_ref, si_ref, bur, bui, ab, bbr, bbi, tm_s, bm_s, *, nb, tc, nc, t_valid):
    c = pl.program_id(0)

    @pl.when(c == 0)
    def _():
        sr_ref[...] = s0r_ref[...]
        si_ref[...] = s0i_ref[...]
        lr = lam_ref[0:1, :]
        li = lam_ref[1:2, :]
        step = jnp.exp(lam_ref[2:3, :])
        mag = jnp.exp(lr * step)
        ab_re = mag * jnp.cos(li * step)
        ab_im = mag * jnp.sin(li * step)
        den = lr * lr + li * li
        num_re = ab_re - 1.0
        z_re = (num_re * lr + ab_im * li) / den
        z_im = (ab_im * lr - num_re * li) / den
        ab[0:1, :] = ab_re
        ab[1:2, :] = ab_im
        bbr[...] = (z_re * bre_ref[...] - z_im * bim_ref[...]).astype(BF16)
        bbi[...] = (z_re * bim_ref[...] + z_im * bre_ref[...]).astype(BF16)

    _to_time_major(u_ref, tm_s, bm_s, nb, tc)
    u = jnp.concatenate([tm_s[j] for j in range(S5_WIDTH // LANES)], axis=-1)
    ub = u.astype(BF16)
    bur[...] = jnp.dot(ub, bbr[...], preferred_element_type=F32)
    bui[...] = jnp.dot(ub, bbi[...], preferred_element_type=F32)

    a_re = jnp.broadcast_to(ab[0:1, :], (SUBLANES, S5_FLAT))
    a_im = jnp.broadcast_to(ab[1:2, :], (SUBLANES, S5_FLAT))
    last_in = t_valid - (nc - 1) * tc
    steps = tc if last_in == tc else jnp.where(c == nc - 1, last_in, tc)

    def group(gi, _):
        g8 = gi * SUBLANES
        rows8 = pl.ds(pl.multiple_of(g8, SUBLANES), SUBLANES)

        def step_fn(t, carry):
            s_re, s_im = carry
            rows = pl.ds(pl.multiple_of(t * nb + g8, SUBLANES), SUBLANES)
            n_re = a_re * s_re - a_im * s_im + bur[rows, :]
            n_im = a_re * s_im + a_im * s_re + bui[rows, :]
            bur[rows, :] = n_re
            bui[rows, :] = n_im
            return n_re, n_im

        s_re, s_im = lax.fori_loop(0, steps, step_fn, (sr_ref[rows8, :], si_ref[rows8, :]))
        sr_ref[rows8, :] = s_re
        si_ref[rows8, :] = s_im
        return 0

    lax.fori_loop(0, nb // SUBLANES, group, 0)

    y = (jnp.dot(bur[...].astype(BF16), cre_ref[...], preferred_element_type=F32)
         - jnp.dot(bui[...].astype(BF16), cim_ref[...], preferred_element_type=F32))
    y = y + d_ref[...] * u
    g = jax.nn.gelu(y)
    gate = jax.nn.sigmoid(jnp.dot(g.astype(BF16), gw_ref[...], preferred_element_type=F32) + gb_ref[...])
    out = g * gate
    for j in range(S5_WIDTH // LANES):
        tm_s[j] = out[:, j * LANES:(j + 1) * LANES]
    _to_batch_major(tm_s, bm_s, y_ref, nb, tc)


def _to_time_major(src_ref, tm_s, bm_s, nb, tc):
    for j in range(tm_s.shape[0]):
        lanes = slice(j * LANES, (j + 1) * LANES)
        if nb <= tc:
            for b in range(nb):
                tm_s[j, pl.ds(b, tc, stride=nb), :] = src_ref[b, :, lanes]
        else:
            bm_s[j] = src_ref[:, :, lanes].reshape(nb * tc, LANES)
            for t in range(tc):
                tm_s[j, t * nb:(t + 1) * nb, :] = bm_s[j, pl.ds(t, nb, stride=tc), :]


def _to_batch_major(tm_s, bm_s, dst_ref, nb, tc):
    for j in range(tm_s.shape[0]):
        lanes = slice(j * LANES, (j + 1) * LANES)
        if nb <= tc:
            for b in range(nb):
                dst_ref[b, :, lanes] = tm_s[j, pl.ds(b, tc, stride=nb), :]
        else:
            for t in range(tc):
                bm_s[j, pl.ds(t, nb, stride=tc), :] = tm_s[j, t * nb:(t + 1) * nb, :]
            dst_ref[:, :, lanes] = bm_s[j].reshape(nb, tc, LANES)


def _s5(u, s0r, s0i, lam, bre, bim, cre, cim, d, gw, gb, *, tc, t_valid):
    nb, t, _ = u.shape
    nc = t // tc
    assert (nc - 1) * tc < t_valid <= t
    body = functools.partial(_s5_body, nb=nb, tc=tc, nc=nc, t_valid=t_valid)
    return pl.pallas_call(
        body,
        grid=(t // tc,),
        in_specs=[pl.BlockSpec((nb, tc, S5_WIDTH), lambda c: (0, c, 0)),
                  _full((nb, S5_FLAT)), _full((nb, S5_FLAT)), _full((SUBLANES, S5_FLAT)),
                  _full((S5_WIDTH, S5_FLAT)), _full((S5_WIDTH, S5_FLAT)),
                  _full((S5_FLAT, S5_WIDTH)), _full((S5_FLAT, S5_WIDTH)),
                  _full((1, S5_WIDTH)), _full((S5_WIDTH, S5_WIDTH)), _full((1, S5_WIDTH))],
        out_specs=[pl.BlockSpec((nb, tc, S5_WIDTH), lambda c: (0, c, 0)),
                   _full((nb, S5_FLAT)), _full((nb, S5_FLAT))],
        out_shape=[jax.ShapeDtypeStruct((nb, t, S5_WIDTH), F32),
                   jax.ShapeDtypeStruct((nb, S5_FLAT), F32), jax.ShapeDtypeStruct((nb, S5_FLAT), F32)],
        scratch_shapes=[pltpu.VMEM((nb * tc, S5_FLAT), F32), pltpu.VMEM((nb * tc, S5_FLAT), F32),
                        pltpu.VMEM((SUBLANES, S5_FLAT), F32),
                        pltpu.VMEM((S5_WIDTH, S5_FLAT), BF16), pltpu.VMEM((S5_WIDTH, S5_FLAT), BF16),
                        pltpu.VMEM((S5_WIDTH // LANES, nb * tc, LANES), F32),
                        pltpu.VMEM((S5_WIDTH // LANES, nb * tc, LANES), F32)],
        compiler_params=_cparams(("arbitrary",)),
        name="s5_scan",
    )(u, s0r, s0i, lam, bre, bim, cre, cim, d, gw, gb)


def _rglru_body(z_ref, h0_ref, cb0_ref, cw_ref, cb_ref, wa_ref, ba_ref, wx_ref, bx_ref, lam_ref,
                y_ref, h_ref, cbn_ref, xs, a_s, b_s, tm_s, bm_s, *, nb, tc, nc, t_valid):
    c = pl.program_id(0)
    hist = RG_CONV - 1

    @pl.when(c == 0)
    def _():
        h_ref[...] = h0_ref[...]
        xs[0:hist * nb, :] = cb0_ref[...]

    _to_time_major(z_ref, tm_s, bm_s, nb, tc)
    xs[hist * nb:(hist + tc) * nb, :] = jnp.concatenate([tm_s[j] for j in range(RG_WIDTH // LANES)], axis=-1)

    xc = jnp.zeros((tc * nb, RG_WIDTH), F32) + cb_ref[...]
    for j in range(RG_CONV):
        xc = xc + xs[j * nb:(j + tc) * nb, :] * cw_ref[j:j + 1, :]

    last_in = t_valid - (nc - 1) * tc
    last = tc if last_in == tc else jnp.where(c == nc - 1, last_in, tc)

    @pl.when(c == nc - 1)
    def _():
        cbn_ref[...] = xs[last_in * nb:(last_in + hist) * nb, :]

    xs[0:hist * nb, :] = xs[tc * nb:(tc + hist) * nb, :]

    xb = xc.astype(BF16)
    r = jax.nn.sigmoid(jnp.dot(xb, wa_ref[...], preferred_element_type=F32) + ba_ref[...])
    i = jax.nn.sigmoid(jnp.dot(xb, wx_ref[...], preferred_element_type=F32) + bx_ref[...])
    log_a = -RG_C * r * jax.nn.softplus(-lam_ref[...])
    a_s[...] = jnp.exp(log_a)
    th = jnp.tanh(log_a)
    b_s[...] = jnp.sqrt(-2.0 * th / (1.0 - th)) * (i * xc)

    def group(gi, _):
        g8 = gi * SUBLANES
        rows8 = pl.ds(pl.multiple_of(g8, SUBLANES), SUBLANES)

        def step_fn(t, h):
            rows = pl.ds(pl.multiple_of(t * nb + g8, SUBLANES), SUBLANES)
            h = a_s[rows, :] * h + b_s[rows, :]
            b_s[rows, :] = h
            return h

        h_ref[rows8, :] = lax.fori_loop(0, last, step_fn, h_ref[rows8, :])
        return 0

    lax.fori_loop(0, nb // SUBLANES, group, 0)
    for j in range(RG_WIDTH // LANES):
        tm_s[j] = b_s[:, j * LANES:(j + 1) * LANES]
    _to_batch_major(tm_s, bm_s, y_ref, nb, tc)
    y_ref[...] = y_ref[...] * jax.nn.gelu(z_ref[:, :, RG_WIDTH:2 * RG_WIDTH])


def _rglru(z, h0, cb0, cw, cb, wa, ba, wx, bx, lam, *, tc, t_valid):
    nb, t, _ = z.shape
    hist = RG_CONV - 1
    nc = t // tc
    assert (nc - 1) * tc + hist <= t_valid <= t
    body = functools.partial(_rglru_body, nb=nb, tc=tc, nc=nc, t_valid=t_valid)
    y, h, cbn = pl.pallas_call(
        body,
        grid=(t // tc,),
        in_specs=[pl.BlockSpec((nb, tc, 2 * RG_WIDTH), lambda c: (0, c, 0)),
                  _full((nb, RG_WIDTH)), _full((hist * nb, RG_WIDTH)),
                  _full((RG_CONV, RG_WIDTH)), _full((1, RG_WIDTH)),
                  _full((RG_WIDTH, RG_WIDTH)), _full((1, RG_WIDTH)),
                  _full((RG_WIDTH, RG_WIDTH)), _full((1, RG_WIDTH)), _full((1, RG_WIDTH))],
        out_specs=[pl.BlockSpec((nb, tc, RG_WIDTH), lambda c: (0, c, 0)),
                   _full((nb, RG_WIDTH)), _full((hist * nb, RG_WIDTH))],
        out_shape=[jax.ShapeDtypeStruct((nb, t, RG_WIDTH), F32),
                   jax.ShapeDtypeStruct((nb, RG_WIDTH), F32), jax.ShapeDtypeStruct((hist * nb, RG_WIDTH), F32)],
        scratch_shapes=[pltpu.VMEM(((tc + hist) * nb, RG_WIDTH), F32),
                        pltpu.VMEM((nb * tc, RG_WIDTH), F32), pltpu.VMEM((nb * tc, RG_WIDTH), F32),
                        pltpu.VMEM((RG_WIDTH // LANES, nb * tc, LANES), F32),
                        pltpu.VMEM((RG_WIDTH // LANES, nb * tc, LANES), F32)],
        compiler_params=_cparams(("arbitrary",)),
        name="rglru_scan",
    )(z, h0, cb0.transpose(1, 0, 2).reshape(hist * nb, RG_WIDTH), cw, cb, wa, ba, wx, bx, lam)
    return y, h, cbn.reshape(hist, nb, RG_WIDTH).transpose(1, 0, 2)


def _dot_nt(a, b, **kw):
    return lax.dot_general(a, b, (((1,), (1,)), ((), ())), preferred_element_type=F32, **kw)


def _dot_tn(a, b, **kw):
    return lax.dot_general(a, b, (((0,), (0,)), ((), ())), preferred_element_type=F32, **kw)


def _mlstm_body(q_ref, k_ref, v_ref, o_ref, g_ref, gb_ref, ng_ref, c0_ref, n0_ref, m0_ref,
                y_ref, cn_ref, nn_ref, mn_ref, *, bb, cl, t_valid):
    c = pl.program_id(1)

    @pl.when(c == 0)
    def _():
        cn_ref[...] = c0_ref[0]
        nn_ref[...] = n0_ref[...]
        mn_ref[...] = m0_ref[...]

    row = lax.broadcasted_iota(jnp.int32, (cl, cl), 0)
    col = lax.broadcasted_iota(jnp.int32, (cl, cl), 1)
    tril = (col <= row).astype(F32)
    keep = (col <= row) & (col + c * cl < t_valid)
    t_ok = (lax.broadcasted_iota(jnp.int32, (cl, LANES), 0) + c * cl) < t_valid
    lane = lax.broadcasted_iota(jnp.int32, (cl, LANES), 1)
    ones = jnp.ones((cl, LANES), F32)

    chains = [(bi, h) for bi in range(bb) for h in range(ML_HEADS)]
    hsl = lambda h: slice(h * ML_DK, (h + 1) * ML_DK)
    hi = lax.Precision.HIGHEST

    bcum, ig = [], []
    for bi in range(bb):
        graw = g_ref[bi] + gb_ref[...]
        lf = jnp.where(t_ok, jax.nn.log_sigmoid(graw), 0.0)
        bcum.append(jnp.dot(tril, lf, preferred_element_type=F32, precision=hi))
        ig.append(jnp.where(t_ok, graw, NEG_BIG))

    qf = {ch: q_ref[ch[0], :, hsl(ch[1])] for ch in chains}
    qh = {ch: qf[ch].astype(BF16) for ch in chains}
    kh = {ch: k_ref[ch[0], :, hsl(ch[1])] * (ML_DK ** -0.5) for ch in chains}
    vh = {ch: v_ref[ch[0], :, hsl(ch[1])].astype(BF16) for ch in chains}
    cm = {ch: cn_ref[ch[0], ch[1]] for ch in chains}
    nv = {ch: nn_ref[ch[0], ch[1]:ch[1] + 1, :] for ch in chains}
    m_old = {ch: mn_ref[ch[0], :, ch[1]:ch[1] + 1] for ch in chains}
    b_col = {ch: bcum[ch[0]][:, ML_HEADS + ch[1]:ML_HEADS + ch[1] + 1] for ch in chains}
    i_col = {ch: ig[ch[0]][:, ch[1]:ch[1] + 1] for ch in chains}

    w_row = {ch: _dot_nt(ones, jnp.where(lane == ch[1], i_col[ch] - b_col[ch], 0.0), precision=hi) for ch in chains}
    qk = {ch: _dot_nt(qh[ch], kh[ch].astype(BF16)) for ch in chains}
    qc = {ch: jnp.dot(qh[ch], cm[ch].astype(BF16), preferred_element_type=F32) for ch in chains}
    dmat = {ch: jnp.where(keep, b_col[ch] + w_row[ch], -jnp.inf) for ch in chains}
    dmax = {ch: jnp.max(dmat[ch], axis=-1, keepdims=True) for ch in chains}
    inter = {ch: b_col[ch] + m_old[ch] for ch in chains}
    mt = {ch: jnp.maximum(inter[ch], dmax[ch]) for ch in chains}
    s = {ch: qk[ch] * jnp.exp(dmat[ch] - mt[ch]) for ch in chains}
    sc = {ch: jnp.exp(inter[ch] - mt[ch]) for ch in chains}
    sv = {ch: jnp.dot(s[ch].astype(BF16), vh[ch], preferred_element_type=F32) for ch in chains}
    qn = {ch: jnp.sum(qf[ch] * nv[ch], axis=-1, keepdims=True) for ch in chains}
    den = {ch: jnp.sum(s[ch], axis=-1, keepdims=True) + sc[ch] * qn[ch] for ch in chains}
    hh = {ch: (sv[ch] + sc[ch] * qc[ch]) / jnp.maximum(jnp.abs(den[ch]), jnp.exp(-mt[ch])) for ch in chains}
    bl = {ch: b_col[ch][cl - 1:cl, :] for ch in chains}
    m_new = {ch: mt[ch][cl - 1:cl, :] for ch in chains}
    wk = {ch: jnp.exp(bl[ch] - b_col[ch] + i_col[ch] - m_new[ch]) * kh[ch] for ch in chains}
    decay = {ch: jnp.exp(bl[ch] + m_old[ch] - m_new[ch]) for ch in chains}
    kv = {ch: _dot_tn(wk[ch].astype(BF16), vh[ch]) for ch in chains}
    rms = {ch: lax.rsqrt(jnp.mean(hh[ch] * hh[ch], axis=-1, keepdims=True) + EPS) for ch in chains}
    for ch in chains:
        bi, h = ch
        cn_ref[bi, h] = decay[ch] * cm[ch] + kv[ch]
        nn_ref[bi, h:h + 1, :] = decay[ch] * nv[ch] + jnp.sum(wk[ch], axis=0, keepdims=True)
        mn_ref[bi, :, h:h + 1] = m_new[ch]
        y_ref[bi, :, hsl(h)] = hh[ch] * rms[ch] * ng_ref[:, hsl(h)] * jax.nn.sigmoid(o_ref[bi, :, hsl(h)])


def _mlstm(layer, zml, zgt, gbias, ng, c0, n0, m0, *, bb, cl, t_valid):
    nb, t, _ = zml.shape
    nc = t // cl
    body = functools.partial(_mlstm_body, bb=bb, cl=cl, t_valid=t_valid)
    zspec = lambda j: pl.BlockSpec((bb, cl, ML_WIDTH), lambda i, c, j=j: (i, c, j))
    return pl.pallas_call(
        body,
        grid=(nb // bb, nc),
        in_specs=[zspec(0), zspec(1), zspec(2), zspec(3),
                  pl.BlockSpec((bb, cl, LANES), lambda i, c: (i, c, 0)),
                  _full((1, LANES)), _full((1, ML_WIDTH)),
                  pl.BlockSpec((1, bb, ML_HEADS, ML_DK, ML_DV), lambda i, c: (layer, i, 0, 0, 0)),
                  pl.BlockSpec((bb, ML_HEADS, ML_DK), lambda i, c: (i, 0, 0)),
                  pl.BlockSpec((bb, 1, ML_HEADS), lambda i, c: (i, 0, 0))],
        out_specs=[pl.BlockSpec((bb, cl, ML_WIDTH), lambda i, c: (i, c, 0)),
                   pl.BlockSpec((bb, ML_HEADS, ML_DK, ML_DV), lambda i, c: (i, 0, 0, 0)),
                   pl.BlockSpec((bb, ML_HEADS, ML_DK), lambda i, c: (i, 0, 0)),
                   pl.BlockSpec((bb, 1, ML_HEADS), lambda i, c: (i, 0, 0))],
        out_shape=[jax.ShapeDtypeStruct((nb, t, ML_WIDTH), F32),
                   jax.ShapeDtypeStruct((nb, ML_HEADS, ML_DK, ML_DV), F32),
                   jax.ShapeDtypeStruct((nb, ML_HEADS, ML_DK), F32),
                   jax.ShapeDtypeStruct((nb, 1, ML_HEADS), F32)],
        compiler_params=_cparams(("arbitrary", "arbitrary")),
        name="mlstm_chunks",
    )(zml, zml, zml, zml, zgt, gbias, ng, c0, n0, m0)


def _outproj_body(x_ref, y1_ref, y2_ref, y3_ref, wo_ref, g2_ref, rw_ref, rb_ref,
                  xn_ref, hn_ref, ridx_ref, rgt_ref):
    acc = jnp.dot(y1_ref[...].astype(BF16), wo_ref[0:256, :], preferred_element_type=F32)
    acc = acc + jnp.dot(y2_ref[...].astype(BF16), wo_ref[256:512, :], preferred_element_type=F32)
    acc = acc + jnp.dot(y3_ref[...].astype(BF16), wo_ref[512:1024, :], preferred_element_type=F32)
    x = x_ref[...] + acc
    xn_ref[...] = x
    h = x * lax.rsqrt(jnp.mean(x * x, axis=-1, keepdims=True) + EPS) * g2_ref[...]
    _rows_to_tiles(h, hn_ref)
    hb = h.astype(BF16)
    tm = x.shape[0]
    eidx = lax.broadcasted_iota(jnp.int32, (N_EXPERTS, ROUTE_TILE), 0)
    sub = lax.broadcasted_iota(jnp.int32, (LANES, ROUTE_TILE), 0)
    for j in range(tm // ROUTE_TILE):
        rows = slice(j * ROUTE_TILE, (j + 1) * ROUTE_TILE)
        vals = _dot_nt(rw_ref[...], hb[rows, :]) + rb_ref[...]
        tops, idxs = [], []
        for _ in range(TOP_K):
            m = jnp.max(vals, axis=0, keepdims=True)
            idx = jnp.min(jnp.where(vals == m, eidx, N_EXPERTS), axis=0, keepdims=True)
            vals = jnp.where(eidx == idx, -jnp.inf, vals)
            tops.append(m)
            idxs.append(idx)
        exps = [jnp.exp(t - tops[0]) for t in tops]
        inv = 1.0 / functools.reduce(lambda a, b: a + b, exps)
        ridx_ref[:, rows] = jnp.concatenate(idxs + idxs, axis=0)
        gt = jnp.zeros((LANES, ROUTE_TILE), F32)
        for kk in range(TOP_K):
            gt = jnp.where(sub == kk, exps[kk] * inv, gt)
        rgt_ref[rows, :] = gt.T


def _outproj(x, y1, y2, y3, wo, g2, rw, rb):
    n = x.shape[0]
    tm = ROW_TILE
    rowspec = lambda w_: pl.BlockSpec((tm, w_), lambda i: (i, 0))
    return pl.pallas_call(
        _outproj_body,
        grid=(n // tm,),
        in_specs=[rowspec(D_MODEL), rowspec(S5_WIDTH), rowspec(RG_WIDTH), rowspec(ML_WIDTH),
                  _full((D_MODEL, D_MODEL)), _full((1, D_MODEL)),
                  _full((N_EXPERTS, D_MODEL)), _full((N_EXPERTS, 1))],
        out_specs=[rowspec(D_MODEL), pl.BlockSpec((tm * ROW_TILES, LANES), lambda i: (i, 0)),
                   pl.BlockSpec((2 * TOP_K, tm), lambda i: (0, i)), rowspec(LANES)],
        out_shape=[jax.ShapeDtypeStruct((n, D_MODEL), F32), jax.ShapeDtypeStruct((n * ROW_TILES, LANES), F32),
                   jax.ShapeDtypeStruct((2 * TOP_K, n), jnp.int32), jax.ShapeDtypeStruct((n, LANES), F32)],
        compiler_params=_cparams(("arbitrary",)),
        name="outproj_route",
    )(x, y1, y2, y3, wo, g2, rw, rb)


def _moe_body(be_ref, bv_ref, tokn_ref, posp_ref, posc_ref, hn_ref, w1_ref, b1_ref, w2_ref, b2_ref, y4_ref,
              w1b, w2b, xbuf0, xbuf1, obuf0, obuf1, gsem, ssem):
    i = pl.program_id(0)
    last = pl.num_programs(0) - 1
    real = bv_ref[i] > 0
    prev_real = bv_ref[jnp.maximum(i - 1, 0)] > 0
    fresh = jnp.logical_or(i == 1, be_ref[i] != be_ref[jnp.maximum(i - 1, 0)])
    xbufs, obufs = (xbuf0, xbuf1), (obuf0, obuf1)

    def step(s):
        x_cur, x_nxt, o_cur, o_prev = xbufs[s], xbufs[1 - s], obufs[s], obufs[1 - s]

        def tile(ref, row):
            return ref.at[pl.ds(pl.multiple_of(row, ROW_TILES), ROW_TILES)]

        def start_gathers():
            for r in range(MOE_TILE):
                pltpu.make_async_copy(tile(hn_ref, tokn_ref[0, 0, r]), tile(x_nxt, r * ROW_TILES),
                                      gsem.at[1 - s]).start(priority=0)

        def start_scatters(pos_ref, o_buf, sem):
            for r in range(MOE_TILE):
                pltpu.make_async_copy(tile(o_buf, r * ROW_TILES), tile(y4_ref, pos_ref[0, 0, r]),
                                      sem).start(priority=1)

        def wait_scatters(o_buf, sem):
            pltpu.make_async_copy(o_buf, y4_ref.at[pl.ds(0, MOE_TILE * ROW_TILES)], sem).wait()

        @pl.when(jnp.logical_or(i == 1, prev_real))
        def _():
            pltpu.make_async_copy(hn_ref.at[pl.ds(0, MOE_TILE * ROW_TILES)], x_cur, gsem.at[s]).wait()

        @pl.when(i >= 2)
        def _():
            wait_scatters(o_cur, ssem.at[s])

        @pl.when(jnp.logical_and(fresh, real))
        def _():
            w1b[...] = w1_ref[0, 0].astype(BF16)
            w2b[...] = w2_ref[0, 0].astype(BF16)

        @pl.when(real)
        def _():
            start_gathers()
            start_scatters(posp_ref, o_prev, ssem.at[1 - s])
            xb = _tiles_to_rows(x_cur).astype(BF16)
            hb = jnp.dot(xb, w1b[...], preferred_element_type=F32) + b1_ref[0, 0]
            g = jnp.minimum(hb[:, :D_FF], SWIGLU_LIMIT)
            u = jnp.clip(hb[:, D_FF:], -SWIGLU_LIMIT, SWIGLU_LIMIT)
            act = g * jax.nn.sigmoid(SWIGLU_ALPHA * g) * (u + 1.0)
            _rows_to_tiles(jnp.dot(act.astype(BF16), w2b[...], preferred_element_type=F32) + b2_ref[0, 0], o_cur)

        @pl.when(jnp.logical_not(real))
        def _():
            if s == 0:
                @pl.when(i == 0)
                def _():
                    obuf0[...] = jnp.zeros_like(obuf0)
                    obuf1[...] = jnp.zeros_like(obuf1)
                    start_gathers()

            @pl.when(i >= 1)
            def _():
                start_scatters(posp_ref, o_prev, ssem.at[1 - s])

        @pl.when(i == last)
        def _():
            start_scatters(posc_ref, o_cur, ssem.at[s])
            wait_scatters(o_prev, ssem.at[1 - s])
            wait_scatters(o_cur, ssem.at[s])

    @pl.when(lax.rem(i, 2) == 0)
    def _():
        step(0)

    @pl.when(lax.rem(i, 2) == 1)
    def _():
        step(1)


def _moe(layer, block_e, block_v, slot_tok, slot_pos, hn, w1, b1, w2, b2):
    n_blocks = block_e.shape[0]
    tm = MOE_TILE
    n_rows = n_blocks * tm
    assert n_blocks >= 3
    idx_spec = lambda f: pl.BlockSpec((1, 1, tm), f, memory_space=pltpu.SMEM)
    grid_spec = pltpu.PrefetchScalarGridSpec(
        num_scalar_prefetch=2,
        grid=(n_blocks,),
        in_specs=[idx_spec(lambda i, be, bv: (jnp.minimum(i + 1, n_blocks - 1), 0, 0)),
                  idx_spec(lambda i, be, bv: (jnp.maximum(i - 1, 0), 0, 0)),
                  idx_spec(lambda i, be, bv: (i, 0, 0)),
                  pl.BlockSpec(memory_space=pl.ANY),
                  pl.BlockSpec((1, 1, D_MODEL, 2 * D_FF), lambda i, be, bv: (layer, be[i], 0, 0)),
                  pl.BlockSpec((1, 1, 1, 2 * D_FF), lambda i, be, bv: (layer, be[i], 0, 0)),
                  pl.BlockSpec((1, 1, D_FF, D_MODEL), lambda i, be, bv: (layer, be[i], 0, 0)),
                  pl.BlockSpec((1, 1, 1, D_MODEL), lambda i, be, bv: (layer, be[i], 0, 0))],
        out_specs=pl.BlockSpec(memory_space=pl.ANY),
        scratch_shapes=[pltpu.VMEM((D_MODEL, 2 * D_FF), BF16), pltpu.VMEM((D_FF, D_MODEL), BF16)]
                       + [pltpu.VMEM((tm * ROW_TILES, LANES), F32)] * 4
                       + [pltpu.SemaphoreType.DMA((2,)), pltpu.SemaphoreType.DMA((2,))],
    )
    return pl.pallas_call(
        _moe_body,
        grid_spec=grid_spec,
        out_shape=jax.ShapeDtypeStruct((n_rows * ROW_TILES, LANES), F32),
        compiler_params=_cparams(("arbitrary",)),
        name="moe_experts",
    )(block_e, block_v, slot_tok, slot_pos, slot_pos, hn, w1, b1, w2, b2)


def _combine_body(x_ref, rgt_ref, fg_ref, *rest, final):
    y_refs, o_ref = rest[:TOP_K], rest[TOP_K]
    x = x_ref[...]
    for kk in range(TOP_K):
        x = x + rgt_ref[:, kk:kk + 1] * _tiles_to_rows(y_refs[kk])
    if final:
        x = x * lax.rsqrt(jnp.mean(x * x, axis=-1, keepdims=True) + EPS) * fg_ref[...]
    o_ref[...] = x


def _combine(x, y4, rgt, fg, *, tok0, n_all, final):
    n = x.shape[0]
    tm = ROW_TILE
    assert tok0 % tm == 0 and n_all % tm == 0
    yspec = lambda kk: pl.BlockSpec((tm * ROW_TILES, LANES), lambda i, kk=kk: ((kk * n_all + tok0) // tm + i, 0))
    return pl.pallas_call(
        functools.partial(_combine_body, final=final),
        grid=(n // tm,),
        in_specs=[pl.BlockSpec((tm, D_MODEL), lambda i: (i, 0)),
                  pl.BlockSpec((tm, LANES), lambda i: (i, 0)), _full((1, D_MODEL))]
                 + [yspec(kk) for kk in range(TOP_K)],
        out_specs=pl.BlockSpec((tm, D_MODEL), lambda i: (i, 0)),
        out_shape=jax.ShapeDtypeStruct((n, D_MODEL), F32),
        compiler_params=_cparams(("arbitrary",)),
        name="moe_combine",
    )(x, rgt, fg, *([y4] * TOP_K))


def _routing_tables(ridx, n):
    na = n * TOP_K
    tm = MOE_TILE
    assert na % tm == 0
    n_blocks = 1 + na // tm + N_EXPERTS
    n_slots = n_blocks * tm
    experts = jnp.arange(N_EXPERTS, dtype=jnp.int32)
    flat_e = ridx[:TOP_K, :].reshape(-1)
    order = jnp.argsort(flat_e).astype(jnp.int32)
    counts = jnp.sum(flat_e[None, :] == experts[:, None], axis=1).astype(jnp.int32)
    padded = (counts + tm - 1) // tm * tm
    pad_end = jnp.cumsum(padded)
    pad_start = pad_end - padded
    start = jnp.cumsum(counts) - counts
    spare_start = na + pad_start - start
    total = pad_end[-1]
    blk = (jnp.arange(n_blocks, dtype=jnp.int32) - 1) * tm
    block_v = ((blk >= 0) & (blk < total)).astype(jnp.int32)
    last_e = jnp.max(jnp.where(counts > 0, experts, 0))
    block_e = jnp.minimum(jnp.sum(blk[:, None] >= pad_end[None, :], axis=1), last_e).astype(jnp.int32)
    slot = jnp.arange(n_slots, dtype=jnp.int32) - tm
    se = jnp.repeat(block_e, tm)
    rank = slot - pad_start[se]
    used = (slot >= 0) & (slot < total)
    real = (rank < counts[se]) & used
    src = order[jnp.clip(start[se] + rank, 0, na - 1)]
    slot_tok = (jnp.where(real, src % n, 0) * ROW_TILES).astype(jnp.int32)
    spare = jnp.where(used, spare_start[se] + rank - counts[se], jnp.where(slot >= 0, slot, n_slots + slot))
    slot_pos = (jnp.where(real, src, spare) * ROW_TILES).astype(jnp.int32)
    return block_e, block_v, slot_tok.reshape(n_blocks, 1, tm), slot_pos.reshape(n_blocks, 1, tm)


def _blockdiag(m):
    g, a, b = m.shape
    eye = jnp.eye(g, dtype=m.dtype)
    return (eye[:, None, :, None] * m[:, :, None, :]).reshape(g * a, g * b)


def _layer_params(l, p):
    lam = jnp.zeros((SUBLANES, S5_FLAT), F32)
    lam = lam.at[0].set(p['s5_lambda_re'][l].reshape(-1)).at[1].set(p['s5_lambda_im'][l].reshape(-1))
    lam = lam.at[2].set(jnp.repeat(p['s5_log_step'][l], S5_STATE))
    gbias = jnp.zeros((1, LANES), F32).at[0, :2 * ML_HEADS].set(p['ml_gate_bias'][l])
    w_in = jnp.pad(p['w_in'][l], ((0, 0), (0, N_IN_PAD - N_IN))).astype(BF16)
    return dict(
        norm1_g=p['norm1_g'][l].reshape(1, -1), w_in=w_in, s5_lam=lam,
        s5_bre=_blockdiag(p['s5_b_re'][l].transpose(0, 2, 1)), s5_bim=_blockdiag(p['s5_b_im'][l].transpose(0, 2, 1)),
        s5_cre=_blockdiag(p['s5_c_re'][l].transpose(0, 2, 1)).astype(BF16),
        s5_cim=_blockdiag(p['s5_c_im'][l].transpose(0, 2, 1)).astype(BF16),
        s5_d=p['s5_d'][l].reshape(1, -1), s5_gw=p['s5_glu_w'][l].astype(BF16), s5_gb=p['s5_glu_b'][l].reshape(1, -1),
        rg_cw=p['rg_conv_w'][l], rg_cb=p['rg_conv_b'][l].reshape(1, -1),
        rg_wa=_blockdiag(p['rg_wa'][l]).astype(BF16), rg_ba=p['rg_ba'][l].reshape(1, -1),
        rg_wx=_blockdiag(p['rg_wx'][l]).astype(BF16), rg_bx=p['rg_bx'][l].reshape(1, -1),
        rg_lam=p['rg_lambda'][l].reshape(1, -1),
        ml_gb=gbias, ml_ng=p['ml_norm_g'][l].reshape(1, -1),
        w_out=p['w_out'][l].astype(BF16), norm2_g=p['norm2_g'][l].reshape(1, -1),
        router_w=p['router_w'][l].T.astype(BF16), router_b=p['router_b'][l].reshape(-1, 1),
    )


def _mix_and_project(l, lp, xf, states, cfg):
    nb, t, t_valid = cfg['nb'], cfg['t'], cfg['t_valid']
    n = nb * t
    s5r, s5i, rgh, rgc, mlc, mln, mlm = states
    zs5, zrg, zml, zgt = _inproj(xf, lp['norm1_g'], lp['w_in'])
    y1, nr, ni = _s5(zs5.reshape(nb, t, -1), s5r[l].reshape(nb, -1), s5i[l].reshape(nb, -1), lp['s5_lam'],
                     lp['s5_bre'], lp['s5_bim'], lp['s5_cre'], lp['s5_cim'], lp['s5_d'], lp['s5_gw'], lp['s5_gb'],
                     tc=cfg['s5_tc'], t_valid=t_valid)
    y2, nh, ncv = _rglru(zrg.reshape(nb, t, -1), rgh[l], rgc[l], lp['rg_cw'], lp['rg_cb'], lp['rg_wa'],
                         lp['rg_ba'], lp['rg_wx'], lp['rg_bx'], lp['rg_lam'], tc=cfg['s5_tc'], t_valid=t_valid)
    y3, nc_, nn_, nm_ = _mlstm(l, zml.reshape(nb, t, -1), zgt.reshape(nb, t, -1), lp['ml_gb'], lp['ml_ng'],
                               mlc, mln[l], mlm[l].reshape(nb, 1, ML_HEADS),
                               bb=cfg['ml_bb'], cl=cfg['ml_cl'], t_valid=t_valid)
    xn, hn, ridx, rgt = _outproj(xf, y1.reshape(n, -1), y2.reshape(n, -1), y3.reshape(n, -1),
                                 lp['w_out'], lp['norm2_g'], lp['router_w'], lp['router_b'])
    new_states = (nr.reshape(nb, S5_GROUPS, S5_STATE), ni.reshape(nb, S5_GROUPS, S5_STATE), nh, ncv, nc_, nn_,
                  nm_.reshape(nb, ML_HEADS))
    return xn, hn, ridx, rgt, new_states


def _trunks(xs, states, cfgs, params, final_g):
    n_all = sum(x.shape[0] for x in xs)
    tok0 = [sum(x.shape[0] for x in xs[:s]) for s in range(len(xs))]
    outs = [[[] for _ in range(7)] for _ in xs]
    b1 = params['exp_b1'].reshape(DEPTH, N_EXPERTS, 1, -1)
    b2 = params['exp_b2'].reshape(DEPTH, N_EXPERTS, 1, -1)
    for l in range(DEPTH):
        lp = _layer_params(l, params)
        mixed = [_mix_and_project(l, lp, xs[s], states[s], cfgs[s]) for s in range(len(xs))]
        hn_all = jnp.concatenate([m[1] for m in mixed], axis=0)
        ridx_all = jnp.concatenate([m[2] for m in mixed], axis=1)
        block_e, block_v, slot_tok, slot_pos = _routing_tables(ridx_all, n_all)
        y4 = _moe(l, block_e, block_v, slot_tok, slot_pos, hn_all, params['exp_w1'], b1, params['exp_w2'], b2)
        xs = [_combine(mixed[s][0], y4, mixed[s][3], final_g.reshape(1, -1), tok0=tok0[s], n_all=n_all,
                       final=(l == DEPTH - 1)) for s in range(len(xs))]
        for s in range(len(xs)):
            for lst, val in zip(outs[s], mixed[s][4]):
                lst.append(val)
    return xs, [[jnp.stack(v) for v in o] for o in outs]


def kernel(x_prompt, x_sample, state_s5_re, state_s5_im, state_rg_h, state_rg_conv, state_ml_c, state_ml_n, state_ml_m, norm1_g, w_in, s5_lambda_re, s5_lambda_im, s5_log_step, s5_b_re, s5_b_im, s5_c_re, s5_c_im, s5_d, s5_glu_w, s5_glu_b, rg_conv_w, rg_conv_b, rg_wa, rg_ba, rg_wx, rg_bx, rg_lambda, ml_gate_bias, ml_norm_g, w_out, norm2_g, router_w, router_b, exp_w1, exp_b1, exp_w2, exp_b2, final_norm_g):
    params = dict(norm1_g=norm1_g, w_in=w_in, s5_lambda_re=s5_lambda_re, s5_lambda_im=s5_lambda_im,
                  s5_log_step=s5_log_step, s5_b_re=s5_b_re, s5_b_im=s5_b_im, s5_c_re=s5_c_re, s5_c_im=s5_c_im,
                  s5_d=s5_d, s5_glu_w=s5_glu_w, s5_glu_b=s5_glu_b, rg_conv_w=rg_conv_w, rg_conv_b=rg_conv_b,
                  rg_wa=rg_wa, rg_ba=rg_ba, rg_wx=rg_wx, rg_bx=rg_bx, rg_lambda=rg_lambda,
                  ml_gate_bias=ml_gate_bias, ml_norm_g=ml_norm_g, w_out=w_out, norm2_g=norm2_g,
                  router_w=router_w, router_b=router_b, exp_w1=exp_w1, exp_b1=exp_b1, exp_w2=exp_w2, exp_b2=exp_b2)
    bp, tp, _ = x_prompt.shape
    bs, ts, _ = x_sample.shape
    assert tp >= RG_CONV - 1 and ts >= RG_CONV - 1
    zeros = lambda *shape: jnp.zeros((DEPTH, bp) + shape, F32)
    prompt_states = (zeros(S5_GROUPS, S5_STATE), zeros(S5_GROUPS, S5_STATE), zeros(RG_WIDTH),
                     zeros(RG_CONV - 1, RG_WIDTH), zeros(ML_HEADS, ML_DK, ML_DV), zeros(ML_HEADS, ML_DK),
                     zeros(ML_HEADS))
    ts_pad = -(-ts // SUBLANES) * SUBLANES
    xs_pad = jnp.pad(x_sample, ((0, 0), (0, ts_pad - ts), (0, 0)))
    sample_states = (state_s5_re, state_s5_im, state_rg_h, state_rg_conv, state_ml_c, state_ml_n, state_ml_m)
    cfgs = [dict(nb=bp, t=tp, t_valid=tp, s5_tc=128, ml_bb=8, ml_cl=math.gcd(tp, ML_CHUNK)),
            dict(nb=bs, t=ts_pad, t_valid=ts, s5_tc=ts_pad, ml_bb=8, ml_cl=ts_pad)]
    (yp, ys), (sp, ss) = _trunks([x_prompt.reshape(bp * tp, D_MODEL), xs_pad.reshape(bs * ts_pad, D_MODEL)],
                                 [prompt_states, sample_states], cfgs, params, final_norm_g)
    return (yp.reshape(bp, tp, D_MODEL), ys.reshape(bs, ts_pad, D_MODEL)[:, :ts], *sp, *ss)
```

```python
import functools
import math

import jax
import jax.numpy as jnp
from jax import lax
from jax.experimental import pallas as pl
from jax.experimental.pallas import tpu as pltpu

F32 = jnp.float32
BF16 = jnp.bfloat16

SUBLANES = 8
LANES = 128
VMEM_LIMIT_BYTES = 56 * 1024 * 1024

D_MODEL = 1024
DEPTH = 2
S5_WIDTH = 256
S5_GROUP = 16
S5_GROUPS = 16
S5_STATE = 64
S5_FLAT = S5_GROUPS * S5_STATE
RG_WIDTH = 256
RG_BLOCKS = 8
RG_CONV = 4
RG_C = 8.0
ML_WIDTH = 512
ML_HEADS = 4
ML_DK = 128
ML_DV = 128
ML_CHUNK = 64
N_EXPERTS = 32
TOP_K = 4
D_FF = 1024
SWIGLU_LIMIT = 7.0
SWIGLU_ALPHA = 1.702
EPS = 1e-5

IN_S5 = (0, 256)
IN_RG = (256, 768)
IN_ML = (768, 2816)
IN_GATE = (2816, 2944)
N_IN = 2824
N_IN_PAD = 2944

ROW_TILE = 512
ROUTE_TILE = 128
MOE_TILE = 256
NEG_BIG = -1e30


def _cparams(sem):
    return pltpu.CompilerParams(dimension_semantics=sem, vmem_limit_bytes=VMEM_LIMIT_BYTES)


def _full(shape):
    n = len(shape)
    return pl.BlockSpec(shape, lambda *_: (0,) * n)


ROW_TILES = D_MODEL // LANES
assert ROW_TILES == SUBLANES


def _rows_to_tiles(x, dst_ref):
    m = x.shape[0]
    for j in range(ROW_TILES):
        dst_ref[pl.ds(j, m, stride=ROW_TILES), :] = x[:, j * LANES:(j + 1) * LANES]


def _tiles_to_rows(src_ref):
    m = src_ref.shape[0] // ROW_TILES
    return jnp.concatenate([src_ref[pl.ds(j, m, stride=ROW_TILES), :] for j in range(ROW_TILES)], axis=-1)


def _inproj_body(x_ref, g_ref, w_ref, zs5_ref, zrg_ref, zml_ref, zgt_ref):
    x = x_ref[...]
    h = x * lax.rsqrt(jnp.mean(x * x, axis=-1, keepdims=True) + EPS) * g_ref[...]
    hb = h.astype(BF16)
    for ref, (lo, hi) in ((zs5_ref, IN_S5), (zrg_ref, IN_RG), (zml_ref, IN_ML), (zgt_ref, IN_GATE)):
        ref[...] = jnp.dot(hb, w_ref[:, lo:hi], preferred_element_type=F32)


def _inproj(x, g, w):
    n = x.shape[0]
    tm = ROW_TILE
    widths = [hi - lo for lo, hi in (IN_S5, IN_RG, IN_ML, IN_GATE)]
    return pl.pallas_call(
        _inproj_body,
        grid=(n // tm,),
        in_specs=[pl.BlockSpec((tm, D_MODEL), lambda i: (i, 0)), _full((1, D_MODEL)), _full((D_MODEL, N_IN_PAD))],
        out_specs=[pl.BlockSpec((tm, w_), lambda i: (i, 0)) for w_ in widths],
        out_shape=[jax.ShapeDtypeStruct((n, w_), F32) for w_ in widths],
        compiler_params=_cparams(("arbitrary",)),
        name="inproj",
    )(x, g, w)


def _s5_body(u_ref, s0r_ref, s0i_ref, lam_ref, bre_ref, bim_ref, cre_ref, cim_ref, d_ref, gw_ref, gb_ref,
             y_ref, sr_ref, si_ref, bur, bui, ab, bbr, bbi, tm_s, bm_s, *, nb, tc, nc, t_valid):
    c = pl.program_id(0)

    @pl.when(c == 0)
    def _():
        sr_ref[...] = s0r_ref[...]
        si_ref[...] = s0i_ref[...]
        lr = lam_ref[0:1, :]
        li = lam_ref[1:2, :]
        step = jnp.exp(lam_ref[2:3, :])
        mag = jnp.exp(lr * step)
        ab_re = mag * jnp.cos(li * step)
        ab_im = mag * jnp.sin(li * step)
        den = lr * lr + li * li
        num_re = ab_re - 1.0
        z_re = (num_re * lr + ab_im * li) / den
        z_im = (ab_im * lr - num_re * li) / den
        ab[0:1, :] = ab_re
        ab[1:2, :] = ab_im
        bbr[...] = (z_re * bre_ref[...] - z_im * bim_ref[...]).astype(BF16)
        bbi[...] = (z_re * bim_ref[...] + z_im * bre_ref[...]).astype(BF16)

    _to_time_major(u_ref, tm_s, bm_s, nb, tc)
    u = jnp.concatenate([tm_s[j] for j in range(S5_WIDTH // LANES)], axis=-1)
    ub = u.astype(BF16)
    bur[...] = jnp.dot(ub, bbr[...], preferred_element_type=F32)
    bui[...] = jnp.dot(ub, bbi[...], preferred_element_type=F32)

    a_re = jnp.broadcast_to(ab[0:1, :], (SUBLANES, S5_FLAT))
    a_im = jnp.broadcast_to(ab[1:2, :], (SUBLANES, S5_FLAT))
    last_in = t_valid - (nc - 1) * tc
    steps = tc if last_in == tc else jnp.where(c == nc - 1, last_in, tc)

    def group(gi, _):
        g8 = gi * SUBLANES
        rows8 = pl.ds(pl.multiple_of(g8, SUBLANES), SUBLANES)

        def step_fn(t, carry):
            s_re, s_im = carry
            rows = pl.ds(pl.multiple_of(t * nb + g8, SUBLANES), SUBLANES)
            n_re = a_re * s_re - a_im * s_im + bur[rows, :]
            n_im = a_re * s_im + a_im * s_re + bui[rows, :]
            bur[rows, :] = n_re
            bui[rows, :] = n_im
            return n_re, n_im

        s_re, s_im = lax.fori_loop(0, steps, step_fn, (sr_ref[rows8, :], si_ref[rows8, :]))
        sr_ref[rows8, :] = s_re
        si_ref[rows8, :] = s_im
        return 0

    lax.fori_loop(0, nb // SUBLANES, group, 0)

    y = (jnp.dot(bur[...].astype(BF16), cre_ref[...], preferred_element_type=F32)
         - jnp.dot(bui[...].astype(BF16), cim_ref[...], preferred_element_type=F32))
    y = y + d_ref[...] * u
    g = jax.nn.gelu(y)
    gate = jax.nn.sigmoid(jnp.dot(g.astype(BF16), gw_ref[...], preferred_element_type=F32) + gb_ref[...])
    out = g * gate
    for j in range(S5_WIDTH // LANES):
        tm_s[j] = out[:, j * LANES:(j + 1) * LANES]
    _to_batch_major(tm_s, bm_s, y_ref, nb, tc)


def _to_time_major(src_ref, tm_s, bm_s, nb, tc):
    for j in range(tm_s.shape[0]):
        lanes = slice(j * LANES, (j + 1) * LANES)
        if nb <= tc:
            for b in range(nb):
                tm_s[j, pl.ds(b, tc, stride=nb), :] = src_ref[b, :, lanes]
        else:
            bm_s[j] = src_ref[:, :, lanes].reshape(nb * tc, LANES)
            for t in range(tc):
                tm_s[j, t * nb:(t + 1) * nb, :] = bm_s[j, pl.ds(t, nb, stride=tc), :]


def _to_batch_major(tm_s, bm_s, dst_ref, nb, tc):
    for j in range(tm_s.shape[0]):
        lanes = slice(j * LANES, (j + 1) * LANES)
        if nb <= tc:
            for b in range(nb):
                dst_ref[b, :, lanes] = tm_s[j, pl.ds(b, tc, stride=nb), :]
        else:
            for t in range(tc):
                bm_s[j, pl.ds(t, nb, stride=tc), :] = tm_s[j, t * nb:(t + 1) * nb, :]
            dst_ref[:, :, lanes] = bm_s[j].reshape(nb, tc, LANES)


def _s5(u, s0r, s0i, lam, bre, bim, cre, cim, d, gw, gb, *, tc, t_valid):
    nb, t, _ = u.shape
    nc = t // tc
    assert (nc - 1) * tc < t_valid <= t
    body = functools.partial(_s5_body, nb=nb, tc=tc, nc=nc, t_valid=t_valid)
    return pl.pallas_call(
        body,
        grid=(t // tc,),
        in_specs=[pl.BlockSpec((nb, tc, S5_WIDTH), lambda c: (0, c, 0)),
                  _full((nb, S5_FLAT)), _full((nb, S5_FLAT)), _full((SUBLANES, S5_FLAT)),
                  _full((S5_WIDTH, S5_FLAT)), _full((S5_WIDTH, S5_FLAT)),
                  _full((S5_FLAT, S5_WIDTH)), _full((S5_FLAT, S5_WIDTH)),
                  _full((1, S5_WIDTH)), _full((S5_WIDTH, S5_WIDTH)), _full((1, S5_WIDTH))],
        out_specs=[pl.BlockSpec((nb, tc, S5_WIDTH), lambda c: (0, c, 0)),
                   _full((nb, S5_FLAT)), _full((nb, S5_FLAT))],
        out_shape=[jax.ShapeDtypeStruct((nb, t, S5_WIDTH), F32),
                   jax.ShapeDtypeStruct((nb, S5_FLAT), F32), jax.ShapeDtypeStruct((nb, S5_FLAT), F32)],
        scratch_shapes=[pltpu.VMEM((nb * tc, S5_FLAT), F32), pltpu.VMEM((nb * tc, S5_FLAT), F32),
                        pltpu.VMEM((SUBLANES, S5_FLAT), F32),
                        pltpu.VMEM((S5_WIDTH, S5_FLAT), BF16), pltpu.VMEM((S5_WIDTH, S5_FLAT), BF16),
                        pltpu.VMEM((S5_WIDTH // LANES, nb * tc, LANES), F32),
                        pltpu.VMEM((S5_WIDTH // LANES, nb * tc, LANES), F32)],
        compiler_params=_cparams(("arbitrary",)),
        name="s5_scan",
    )(u, s0r, s0i, lam, bre, bim, cre, cim, d, gw, gb)


def _rglru_body(z_ref, h0_ref, cb0_ref, cw_ref, cb_ref, wa_ref, ba_ref, wx_ref, bx_ref, lam_ref,
                y_ref, h_ref, cbn_ref, xs, a_s, b_s, tm_s, bm_s, *, nb, tc, nc, t_valid):
    c = pl.program_id(0)
    hist = RG_CONV - 1

    @pl.when(c == 0)
    def _():
        h_ref[...] = h0_ref[...]
        xs[0:hist * nb, :] = cb0_ref[...]

    _to_time_major(z_ref, tm_s, bm_s, nb, tc)
    xs[hist * nb:(hist + tc) * nb, :] = jnp.concatenate([tm_s[j] for j in range(RG_WIDTH // LANES)], axis=-1)

    xc = jnp.zeros((tc * nb, RG_WIDTH), F32) + cb_ref[...]
    for j in range(RG_CONV):
        xc = xc + xs[j * nb:(j + tc) * nb, :] * cw_ref[j:j + 1, :]

    last_in = t_valid - (nc - 1) * tc
    last = tc if last_in == tc else jnp.where(c == nc - 1, last_in, tc)

    @pl.when(c == nc - 1)
    def _():
        cbn_ref[...] = xs[last_in * nb:(last_in + hist) * nb, :]

    xs[0:hist * nb, :] = xs[tc * nb:(tc + hist) * nb, :]

    xb = xc.astype(BF16)
    r = jax.nn.sigmoid(jnp.dot(xb, wa_ref[...], preferred_element_type=F32) + ba_ref[...])
    i = jax.nn.sigmoid(jnp.dot(xb, wx_ref[...], preferred_element_type=F32) + bx_ref[...])
    log_a = -RG_C * r * jax.nn.softplus(-lam_ref[...])
    a_s[...] = jnp.exp(log_a)
    th = jnp.tanh(log_a)
    b_s[...] = jnp.sqrt(-2.0 * th / (1.0 - th)) * (i * xc)

    def group(gi, _):
        g8 = gi * SUBLANES
        rows8 = pl.ds(pl.multiple_of(g8, SUBLANES), SUBLANES)

        def step_fn(t, h):
            rows = pl.ds(pl.multiple_of(t * nb + g8, SUBLANES), SUBLANES)
            h = a_s[rows, :] * h + b_s[rows, :]
            b_s[rows, :] = h
            return h

        h_ref[rows8, :] = lax.fori_loop(0, last, step_fn, h_ref[rows8, :])
        return 0

    lax.fori_loop(0, nb // SUBLANES, group, 0)
    for j in range(RG_WIDTH // LANES):
        tm_s[j] = b_s[:, j * LANES:(j + 1) * LANES]
    _to_batch_major(tm_s, bm_s, y_ref, nb, tc)
    y_ref[...] = y_ref[...] * jax.nn.gelu(z_ref[:, :, RG_WIDTH:2 * RG_WIDTH])


def _rglru(z, h0, cb0, cw, cb, wa, ba, wx, bx, lam, *, tc, t_valid):
    nb, t, _ = z.shape
    hist = RG_CONV - 1
    nc = t // tc
    assert (nc - 1) * tc + hist <= t_valid <= t
    body = functools.partial(_rglru_body, nb=nb, tc=tc, nc=nc, t_valid=t_valid)
    y, h, cbn = pl.pallas_call(
        body,
        grid=(t // tc,),
        in_specs=[pl.BlockSpec((nb, tc, 2 * RG_WIDTH), lambda c: (0, c, 0)),
                  _full((nb, RG_WIDTH)), _full((hist * nb, RG_WIDTH)),
                  _full((RG_CONV, RG_WIDTH)), _full((1, RG_WIDTH)),
                  _full((RG_WIDTH, RG_WIDTH)), _full((1, RG_WIDTH)),
                  _full((RG_WIDTH, RG_WIDTH)), _full((1, RG_WIDTH)), _full((1, RG_WIDTH))],
        out_specs=[pl.BlockSpec((nb, tc, RG_WIDTH), lambda c: (0, c, 0)),
                   _full((nb, RG_WIDTH)), _full((hist * nb, RG_WIDTH))],
        out_shape=[jax.ShapeDtypeStruct((nb, t, RG_WIDTH), F32),
                   jax.ShapeDtypeStruct((nb, RG_WIDTH), F32), jax.ShapeDtypeStruct((hist * nb, RG_WIDTH), F32)],
        scratch_shapes=[pltpu.VMEM(((tc + hist) * nb, RG_WIDTH), F32),
                        pltpu.VMEM((nb * tc, RG_WIDTH), F32), pltpu.VMEM((nb * tc, RG_WIDTH), F32),
                        pltpu.VMEM((RG_WIDTH // LANES, nb * tc, LANES), F32),
                        pltpu.VMEM((RG_WIDTH // LANES, nb * tc, LANES), F32)],
        compiler_params=_cparams(("arbitrary",)),
        name="rglru_scan",
    )(z, h0, cb0.transpose(1, 0, 2).reshape(hist * nb, RG_WIDTH), cw, cb, wa, ba, wx, bx, lam)
    return y, h, cbn.reshape(hist, nb, RG_WIDTH).transpose(1, 0, 2)


def _dot_nt(a, b, **kw):
    return lax.dot_general(a, b, (((1,), (1,)), ((), ())), preferred_element_type=F32, **kw)


def _dot_tn(a, b, **kw):
    return lax.dot_general(a, b, (((0,), (0,)), ((), ())), preferred_element_type=F32, **kw)


def _mlstm_body(q_ref, k_ref, v_ref, o_ref, g_ref, gb_ref, ng_ref, c0_ref, n0_ref, m0_ref,
                y_ref, cn_ref, nn_ref, mn_ref, *, bb, cl, t_valid):
    c = pl.program_id(1)

    @pl.when(c == 0)
    def _():
        cn_ref[...] = c0_ref[0]
        nn_ref[...] = n0_ref[...]
        mn_ref[...] = m0_ref[...]

    row = lax.broadcasted_iota(jnp.int32, (cl, cl), 0)
    col = lax.broadcasted_iota(jnp.int32, (cl, cl), 1)
    tril = (col <= row).astype(F32)
    keep = (col <= row) & (col + c * cl < t_valid)
    t_ok = (lax.broadcasted_iota(jnp.int32, (cl, LANES), 0) + c * cl) < t_valid
    lane = lax.broadcasted_iota(jnp.int32, (cl, LANES), 1)
    ones = jnp.ones((cl, LANES), F32)

    chains = [(bi, h) for bi in range(bb) for h in range(ML_HEADS)]
    hsl = lambda h: slice(h * ML_DK, (h + 1) * ML_DK)
    hi = lax.Precision.HIGHEST

    bcum, ig = [], []
    for bi in range(bb):
        graw = g_ref[bi] + gb_ref[...]
        lf = jnp.where(t_ok, jax.nn.log_sigmoid(graw), 0.0)
        bcum.append(jnp.dot(tril, lf, preferred_element_type=F32, precision=hi))
        ig.append(jnp.where(t_ok, graw, NEG_BIG))

    qf = {ch: q_ref[ch[0], :, hsl(ch[1])] for ch in chains}
    qh = {ch: qf[ch].astype(BF16) for ch in chains}
    kh = {ch: k_ref[ch[0], :, hsl(ch[1])] * (ML_DK ** -0.5) for ch in chains}
    vh = {ch: v_ref[ch[0], :, hsl(ch[1])].astype(BF16) for ch in chains}
    cm = {ch: cn_ref[ch[0], ch[1]] for ch in chains}
    nv = {ch: nn_ref[ch[0], ch[1]:ch[1] + 1, :] for ch in chains}
    m_old = {ch: mn_ref[ch[0], :, ch[1]:ch[1] + 1] for ch in chains}
    b_col = {ch: bcum[ch[0]][:, ML_HEADS + ch[1]:ML_HEADS + ch[1] + 1] for ch in chains}
    i_col = {ch: ig[ch[0]][:, ch[1]:ch[1] + 1] for ch in chains}

    w_row = {ch: _dot_nt(ones, jnp.where(lane == ch[1], i_col[ch] - b_col[ch], 0.0), precision=hi) for ch in chains}
    qk = {ch: _dot_nt(qh[ch], kh[ch].astype(BF16)) for ch in chains}
    qc = {ch: jnp.dot(qh[ch], cm[ch].astype(BF16), preferred_element_type=F32) for ch in chains}
    dmat = {ch: jnp.where(keep, b_col[ch] + w_row[ch], -jnp.inf) for ch in chains}
    dmax = {ch: jnp.max(dmat[ch], axis=-1, keepdims=True) for ch in chains}
    inter = {ch: b_col[ch] + m_old[ch] for ch in chains}
    mt = {ch: jnp.maximum(inter[ch], dmax[ch]) for ch in chains}
    s = {ch: qk[ch] * jnp.exp(dmat[ch] - mt[ch]) for ch in chains}
    sc = {ch: jnp.exp(inter[ch] - mt[ch]) for ch in chains}
    sv = {ch: jnp.dot(s[ch].astype(BF16), vh[ch], preferred_element_type=F32) for ch in chains}
    qn = {ch: jnp.sum(qf[ch] * nv[ch], axis=-1, keepdims=True) for ch in chains}
    den = {ch: jnp.sum(s[ch], axis=-1, keepdims=True) + sc[ch] * qn[ch] for ch in chains}
    hh = {ch: (sv[ch] + sc[ch] * qc[ch]) / jnp.maximum(jnp.abs(den[ch]), jnp.exp(-mt[ch])) for ch in chains}
    bl = {ch: b_col[ch][cl - 1:cl, :] for ch in chains}
    m_new = {ch: mt[ch][cl - 1:cl, :] for ch in chains}
    wk = {ch: jnp.exp(bl[ch] - b_col[ch] + i_col[ch] - m_new[ch]) * kh[ch] for ch in chains}
    decay = {ch: jnp.exp(bl[ch] + m_old[ch] - m_new[ch]) for ch in chains}
    kv = {ch: _dot_tn(wk[ch].astype(BF16), vh[ch]) for ch in chains}
    rms = {ch: lax.rsqrt(jnp.mean(hh[ch] * hh[ch], axis=-1, keepdims=True) + EPS) for ch in chains}
    for ch in chains:
        bi, h = ch
        cn_ref[bi, h] = decay[ch] * cm[ch] + kv[ch]
        nn_ref[bi, h:h + 1, :] = decay[ch] * nv[ch] + jnp.sum(wk[ch], axis=0, keepdims=True)
        mn_ref[bi, :, h:h + 1] = m_new[ch]
        y_ref[bi, :, hsl(h)] = hh[ch] * rms[ch] * ng_ref[:, hsl(h)] * jax.nn.sigmoid(o_ref[bi, :, hsl(h)])


def _mlstm(layer, zml, zgt, gbias, ng, c0, n0, m0, *, bb, cl, t_valid):
    nb, t, _ = zml.shape
    nc = t // cl
    body = functools.partial(_mlstm_body, bb=bb, cl=cl, t_valid=t_valid)
    zspec = lambda j: pl.BlockSpec((bb, cl, ML_WIDTH), lambda i, c, j=j: (i, c, j))
    return pl.pallas_call(
        body,
        grid=(nb // bb, nc),
        in_specs=[zspec(0), zspec(1), zspec(2), zspec(3),
                  pl.BlockSpec((bb, cl, LANES), lambda i, c: (i, c, 0)),
                  _full((1, LANES)), _full((1, ML_WIDTH)),
                  pl.BlockSpec((1, bb, ML_HEADS, ML_DK, ML_DV), lambda i, c: (layer, i, 0, 0, 0)),
                  pl.BlockSpec((bb, ML_HEADS, ML_DK), lambda i, c: (i, 0, 0)),
                  pl.BlockSpec((bb, 1, ML_HEADS), lambda i, c: (i, 0, 0))],
        out_specs=[pl.BlockSpec((bb, cl, ML_WIDTH), lambda i, c: (i, c, 0)),
                   pl.BlockSpec((bb, ML_HEADS, ML_DK, ML_DV), lambda i, c: (i, 0, 0, 0)),
                   pl.BlockSpec((bb, ML_HEADS, ML_DK), lambda i, c: (i, 0, 0)),
                   pl.BlockSpec((bb, 1, ML_HEADS), lambda i, c: (i, 0, 0))],
        out_shape=[jax.ShapeDtypeStruct((nb, t, ML_WIDTH), F32),
                   jax.ShapeDtypeStruct((nb, ML_HEADS, ML_DK, ML_DV), F32),
                   jax.ShapeDtypeStruct((nb, ML_HEADS, ML_DK), F32),
                   jax.ShapeDtypeStruct((nb, 1, ML_HEADS), F32)],
        compiler_params=_cparams(("arbitrary", "arbitrary")),
        name="mlstm_chunks",
    )(zml, zml, zml, zml, zgt, gbias, ng, c0, n0, m0)


def _outproj_body(x_ref, y1_ref, y2_ref, y3_ref, wo_ref, g2_ref, rw_ref, rb_ref,
                  xn_ref, hn_ref, ridx_ref, rgt_ref):
    acc = jnp.dot(y1_ref[...].astype(BF16), wo_ref[0:256, :], preferred_element_type=F32)
    acc = acc + jnp.dot(y2_ref[...].astype(BF16), wo_ref[256:512, :], preferred_element_type=F32)
    acc = acc + jnp.dot(y3_ref[...].astype(BF16), wo_ref[512:1024, :], preferred_element_type=F32)
    x = x_ref[...] + acc
    xn_ref[...] = x
    h = x * lax.rsqrt(jnp.mean(x * x, axis=-1, keepdims=True) + EPS) * g2_ref[...]
    _rows_to_tiles(h, hn_ref)
    hb = h.astype(BF16)
    tm = x.shape[0]
    eidx = lax.broadcasted_iota(jnp.int32, (N_EXPERTS, ROUTE_TILE), 0)
    sub = lax.broadcasted_iota(jnp.int32, (LANES, ROUTE_TILE), 0)
    for j in range(tm // ROUTE_TILE):
        rows = slice(j * ROUTE_TILE, (j + 1) * ROUTE_TILE)
        vals = _dot_nt(rw_ref[...], hb[rows, :]) + rb_ref[...]
        tops, idxs = [], []
        for _ in range(TOP_K):
            m = jnp.max(vals, axis=0, keepdims=True)
            idx = jnp.min(jnp.where(vals == m, eidx, N_EXPERTS), axis=0, keepdims=True)
            vals = jnp.where(eidx == idx, -jnp.inf, vals)
            tops.append(m)
            idxs.append(idx)
        exps = [jnp.exp(t - tops[0]) for t in tops]
        inv = 1.0 / functools.reduce(lambda a, b: a + b, exps)
        ridx_ref[:, rows] = jnp.concatenate(idxs + idxs, axis=0)
        gt = jnp.zeros((LANES, ROUTE_TILE), F32)
        for kk in range(TOP_K):
            gt = jnp.where(sub == kk, exps[kk] * inv, gt)
        rgt_ref[rows, :] = gt.T


def _outproj(x, y1, y2, y3, wo, g2, rw, rb):
    n = x.shape[0]
    tm = ROW_TILE
    rowspec = lambda w_: pl.BlockSpec((tm, w_), lambda i: (i, 0))
    return pl.pallas_call(
        _outproj_body,
        grid=(n // tm,),
        in_specs=[rowspec(D_MODEL), rowspec(S5_WIDTH), rowspec(RG_WIDTH), rowspec(ML_WIDTH),
                  _full((D_MODEL, D_MODEL)), _full((1, D_MODEL)),
                  _full((N_EXPERTS, D_MODEL)), _full((N_EXPERTS, 1))],
        out_specs=[rowspec(D_MODEL), pl.BlockSpec((tm * ROW_TILES, LANES), lambda i: (i, 0)),
                   pl.BlockSpec((2 * TOP_K, tm), lambda i: (0, i)), rowspec(LANES)],
        out_shape=[jax.ShapeDtypeStruct((n, D_MODEL), F32), jax.ShapeDtypeStruct((n * ROW_TILES, LANES), F32),
                   jax.ShapeDtypeStruct((2 * TOP_K, n), jnp.int32), jax.ShapeDtypeStruct((n, LANES), F32)],
        compiler_params=_cparams(("arbitrary",)),
        name="outproj_route",
    )(x, y1, y2, y3, wo, g2, rw, rb)


GATHER_AHEAD = 2
N_XBUF = GATHER_AHEAD + 1


def _moe_body(be_ref, bv_ref, tokn_ref, posp_ref, posc_ref, hn_ref, w1_ref, b1_ref, w2_ref, b2_ref, y4_ref,
              w1b, w2b, xbuf, obuf, gsem, ssem):
    i = pl.program_id(0)
    last = pl.num_programs(0) - 1
    real = bv_ref[i] > 0
    fresh = jnp.logical_or(i == GATHER_AHEAD, be_ref[i] != be_ref[jnp.maximum(i - 1, 0)])
    xs, xn = lax.rem(i, N_XBUF), lax.rem(i + GATHER_AHEAD, N_XBUF)
    os_ = lax.rem(i, 2)
    op = 1 - os_
    x_cur, x_nxt, o_cur, o_prev = xbuf.at[xs], xbuf.at[xn], obuf.at[os_], obuf.at[op]

    def tile(ref, row):
        return ref.at[pl.ds(pl.multiple_of(row, ROW_TILES), ROW_TILES)]

    def start_gathers():
        for r in range(MOE_TILE):
            pltpu.make_async_copy(tile(hn_ref, tokn_ref[0, 0, r]), tile(x_nxt, r * ROW_TILES),
                                  gsem.at[xn]).start(priority=0)

    def start_scatters(pos_ref, o_buf, sem):
        for r in range(MOE_TILE):
            pltpu.make_async_copy(tile(o_buf, r * ROW_TILES), tile(y4_ref, pos_ref[0, 0, r]), sem).start(priority=1)

    def wait_scatters(o_buf, sem):
        pltpu.make_async_copy(o_buf, y4_ref.at[pl.ds(0, MOE_TILE * ROW_TILES)], sem).wait()

    gathered = jnp.logical_and(i >= GATHER_AHEAD, jnp.logical_or(
        i < 2 * GATHER_AHEAD, bv_ref[jnp.maximum(i - GATHER_AHEAD, 0)] > 0))

    @pl.when(gathered)
    def _():
        pltpu.make_async_copy(hn_ref.at[pl.ds(0, MOE_TILE * ROW_TILES)], x_cur, gsem.at[xs]).wait()

    @pl.when(i >= 2)
    def _():
        wait_scatters(o_cur, ssem.at[os_])

    @pl.when(jnp.logical_and(fresh, real))
    def _():
        w1b[...] = w1_ref[0, 0].astype(BF16)
        w2b[...] = w2_ref[0, 0].astype(BF16)

    @pl.when(real)
    def _():
        xb = _tiles_to_rows(x_cur).astype(BF16)
        start_gathers()
        start_scatters(posp_ref, o_prev, ssem.at[op])
        hb = jnp.dot(xb, w1b[...], preferred_element_type=F32) + b1_ref[0, 0]
        g = jnp.minimum(hb[:, :D_FF], SWIGLU_LIMIT)
        u = jnp.clip(hb[:, D_FF:], -SWIGLU_LIMIT, SWIGLU_LIMIT)
        act = g * jax.nn.sigmoid(SWIGLU_ALPHA * g) * (u + 1.0)
        _rows_to_tiles(jnp.dot(act.astype(BF16), w2b[...], preferred_element_type=F32) + b2_ref[0, 0], o_cur)

    @pl.when(jnp.logical_not(real))
    def _():
        @pl.when(i == 0)
        def _():
            obuf[...] = jnp.zeros_like(obuf)

        @pl.when(i < GATHER_AHEAD)
        def _():
            start_gathers()

        @pl.when(i >= 1)
        def _():
            start_scatters(posp_ref, o_prev, ssem.at[op])

    @pl.when(i == last)
    def _():
        start_scatters(posc_ref, o_cur, ssem.at[os_])
        wait_scatters(o_prev, ssem.at[op])
        wait_scatters(o_cur, ssem.at[os_])


def _moe(layer, block_e, block_v, slot_tok, slot_pos, hn, w1, b1, w2, b2):
    n_blocks = block_e.shape[0]
    tm = MOE_TILE
    n_rows = n_blocks * tm
    assert n_blocks > 2 * GATHER_AHEAD
    idx_spec = lambda f: pl.BlockSpec((1, 1, tm), f, memory_space=pltpu.SMEM)
    grid_spec = pltpu.PrefetchScalarGridSpec(
        num_scalar_prefetch=2,
        grid=(n_blocks,),
        in_specs=[idx_spec(lambda i, be, bv: (jnp.minimum(i + GATHER_AHEAD, n_blocks - 1), 0, 0)),
                  idx_spec(lambda i, be, bv: (jnp.maximum(i - 1, 0), 0, 0)),
                  idx_spec(lambda i, be, bv: (i, 0, 0)),
                  pl.BlockSpec(memory_space=pl.ANY),
                  pl.BlockSpec((1, 1, D_MODEL, 2 * D_FF), lambda i, be, bv: (layer, be[i], 0, 0)),
                  pl.BlockSpec((1, 1, 1, 2 * D_FF), lambda i, be, bv: (layer, be[i], 0, 0)),
                  pl.BlockSpec((1, 1, D_FF, D_MODEL), lambda i, be, bv: (layer, be[i], 0, 0)),
                  pl.BlockSpec((1, 1, 1, D_MODEL), lambda i, be, bv: (layer, be[i], 0, 0))],
        out_specs=pl.BlockSpec(memory_space=pl.ANY),
        scratch_shapes=[pltpu.VMEM((D_MODEL, 2 * D_FF), BF16), pltpu.VMEM((D_FF, D_MODEL), BF16)]
                       + [pltpu.VMEM((N_XBUF, tm * ROW_TILES, LANES), F32), pltpu.VMEM((2, tm * ROW_TILES, LANES), F32)]
                       + [pltpu.SemaphoreType.DMA((N_XBUF,)), pltpu.SemaphoreType.DMA((2,))],
    )
    return pl.pallas_call(
        _moe_body,
        grid_spec=grid_spec,
        out_shape=jax.ShapeDtypeStruct((n_rows * ROW_TILES, LANES), F32),
        compiler_params=_cparams(("arbitrary",)),
        name="moe_experts",
    )(block_e, block_v, slot_tok, slot_pos, slot_pos, hn, w1, b1, w2, b2)


def _combine_body(x_ref, rgt_ref, fg_ref, *rest, final):
    y_refs, o_ref = rest[:TOP_K], rest[TOP_K]
    x = x_ref[...]
    for kk in range(TOP_K):
        x = x + rgt_ref[:, kk:kk + 1] * _tiles_to_rows(y_refs[kk])
    if final:
        x = x * lax.rsqrt(jnp.mean(x * x, axis=-1, keepdims=True) + EPS) * fg_ref[...]
    o_ref[...] = x


def _combine(x, y4, rgt, fg, *, tok0, n_all, final):
    n = x.shape[0]
    tm = ROW_TILE
    assert tok0 % tm == 0 and n_all % tm == 0
    yspec = lambda kk: pl.BlockSpec((tm * ROW_TILES, LANES), lambda i, kk=kk: ((kk * n_all + tok0) // tm + i, 0))
    return pl.pallas_call(
        functools.partial(_combine_body, final=final),
        grid=(n // tm,),
        in_specs=[pl.BlockSpec((tm, D_MODEL), lambda i: (i, 0)),
                  pl.BlockSpec((tm, LANES), lambda i: (i, 0)), _full((1, D_MODEL))]
                 + [yspec(kk) for kk in range(TOP_K)],
        out_specs=pl.BlockSpec((tm, D_MODEL), lambda i: (i, 0)),
        out_shape=jax.ShapeDtypeStruct((n, D_MODEL), F32),
        compiler_params=_cparams(("arbitrary",)),
        name="moe_combine",
    )(x, rgt, fg, *([y4] * TOP_K))


def _routing_tables(ridx, n):
    na = n * TOP_K
    tm = MOE_TILE
    assert na % tm == 0
    lead = GATHER_AHEAD * tm
    n_blocks = na // tm + N_EXPERTS + 2 * GATHER_AHEAD
    n_slots = n_blocks * tm
    experts = jnp.arange(N_EXPERTS, dtype=jnp.int32)
    flat_e = ridx[:TOP_K, :].reshape(-1)
    order = jnp.argsort(flat_e).astype(jnp.int32)
    counts = jnp.sum(flat_e[None, :] == experts[:, None], axis=1).astype(jnp.int32)
    padded = (counts + tm - 1) // tm * tm
    pad_end = jnp.cumsum(padded)
    pad_start = pad_end - padded
    start = jnp.cumsum(counts) - counts
    spare_start = na + pad_start - start
    total = pad_end[-1]
    blk = jnp.arange(n_blocks, dtype=jnp.int32) * tm - lead
    block_v = ((blk >= 0) & (blk < total)).astype(jnp.int32)
    last_e = jnp.max(jnp.where(counts > 0, experts, 0))
    block_e = jnp.minimum(jnp.sum(blk[:, None] >= pad_end[None, :], axis=1), last_e).astype(jnp.int32)
    slot = jnp.arange(n_slots, dtype=jnp.int32) - lead
    se = jnp.repeat(block_e, tm)
    rank = slot - pad_start[se]
    used = (slot >= 0) & (slot < total)
    real = (rank < counts[se]) & used
    src = order[jnp.clip(start[se] + rank, 0, na - 1)]
    slot_tok = (jnp.where(real, src % n, 0) * ROW_TILES).astype(jnp.int32)
    spare = jnp.where(used, spare_start[se] + rank - counts[se], jnp.where(slot >= 0, slot, n_slots + slot))
    slot_pos = (jnp.where(real, src, spare) * ROW_TILES).astype(jnp.int32)
    return block_e, block_v, slot_tok.reshape(n_blocks, 1, tm), slot_pos.reshape(n_blocks, 1, tm)


def _blockdiag(m):
    g, a, b = m.shape
    eye = jnp.eye(g, dtype=m.dtype)
    return (eye[:, None, :, None] * m[:, :, None, :]).reshape(g * a, g * b)


def _layer_params(l, p):
    lam = jnp.zeros((SUBLANES, S5_FLAT), F32)
    lam = lam.at[0].set(p['s5_lambda_re'][l].reshape(-1)).at[1].set(p['s5_lambda_im'][l].reshape(-1))
    lam = lam.at[2].set(jnp.repeat(p['s5_log_step'][l], S5_STATE))
    gbias = jnp.zeros((1, LANES), F32).at[0, :2 * ML_HEADS].set(p['ml_gate_bias'][l])
    w_in = jnp.pad(p['w_in'][l], ((0, 0), (0, N_IN_PAD - N_IN))).astype(BF16)
    return dict(
        norm1_g=p['norm1_g'][l].reshape(1, -1), w_in=w_in, s5_lam=lam,
        s5_bre=_blockdiag(p['s5_b_re'][l].transpose(0, 2, 1)), s5_bim=_blockdiag(p['s5_b_im'][l].transpose(0, 2, 1)),
        s5_cre=_blockdiag(p['s5_c_re'][l].transpose(0, 2, 1)).astype(BF16),
        s5_cim=_blockdiag(p['s5_c_im'][l].transpose(0, 2, 1)).astype(BF16),
        s5_d=p['s5_d'][l].reshape(1, -1), s5_gw=p['s5_glu_w'][l].astype(BF16), s5_gb=p['s5_glu_b'][l].reshape(1, -1),
        rg_cw=p['rg_conv_w'][l], rg_cb=p['rg_conv_b'][l].reshape(1, -1),
        rg_wa=_blockdiag(p['rg_wa'][l]).astype(BF16), rg_ba=p['rg_ba'][l].reshape(1, -1),
        rg_wx=_blockdiag(p['rg_wx'][l]).astype(BF16), rg_bx=p['rg_bx'][l].reshape(1, -1),
        rg_lam=p['rg_lambda'][l].reshape(1, -1),
        ml_gb=gbias, ml_ng=p['ml_norm_g'][l].reshape(1, -1),
        w_out=p['w_out'][l].astype(BF16), norm2_g=p['norm2_g'][l].reshape(1, -1),
        router_w=p['router_w'][l].T.astype(BF16), router_b=p['router_b'][l].reshape(-1, 1),
    )


def _mix_and_project(l, lp, xf, states, cfg):
    nb, t, t_valid = cfg['nb'], cfg['t'], cfg['t_valid']
    n = nb * t
    s5r, s5i, rgh, rgc, mlc, mln, mlm = states
    zs5, zrg, zml, zgt = _inproj(xf, lp['norm1_g'], lp['w_in'])
    y1, nr, ni = _s5(zs5.reshape(nb, t, -1), s5r[l].reshape(nb, -1), s5i[l].reshape(nb, -1), lp['s5_lam'],
                     lp['s5_bre'], lp['s5_bim'], lp['s5_cre'], lp['s5_cim'], lp['s5_d'], lp['s5_gw'], lp['s5_gb'],
                     tc=cfg['s5_tc'], t_valid=t_valid)
    y2, nh, ncv = _rglru(zrg.reshape(nb, t, -1), rgh[l], rgc[l], lp['rg_cw'], lp['rg_cb'], lp['rg_wa'],
                         lp['rg_ba'], lp['rg_wx'], lp['rg_bx'], lp['rg_lam'], tc=cfg['s5_tc'], t_valid=t_valid)
    y3, nc_, nn_, nm_ = _mlstm(l, zml.reshape(nb, t, -1), zgt.reshape(nb, t, -1), lp['ml_gb'], lp['ml_ng'],
                               mlc, mln[l], mlm[l].reshape(nb, 1, ML_HEADS),
                               bb=cfg['ml_bb'], cl=cfg['ml_cl'], t_valid=t_valid)
    xn, hn, ridx, rgt = _outproj(xf, y1.reshape(n, -1), y2.reshape(n, -1), y3.reshape(n, -1),
                                 lp['w_out'], lp['norm2_g'], lp['router_w'], lp['router_b'])
    new_states = (nr.reshape(nb, S5_GROUPS, S5_STATE), ni.reshape(nb, S5_GROUPS, S5_STATE), nh, ncv, nc_, nn_,
                  nm_.reshape(nb, ML_HEADS))
    return xn, hn, ridx, rgt, new_states


def _trunks(xs, states, cfgs, params, final_g):
    n_all = sum(x.shape[0] for x in xs)
    tok0 = [sum(x.shape[0] for x in xs[:s]) for s in range(len(xs))]
    outs = [[[] for _ in range(7)] for _ in xs]
    b1 = params['exp_b1'].reshape(DEPTH, N_EXPERTS, 1, -1)
    b2 = params['exp_b2'].reshape(DEPTH, N_EXPERTS, 1, -1)
    for l in range(DEPTH):
        lp = _layer_params(l, params)
        mixed = [_mix_and_project(l, lp, xs[s], states[s], cfgs[s]) for s in range(len(xs))]
        hn_all = jnp.concatenate([m[1] for m in mixed], axis=0)
        ridx_all = jnp.concatenate([m[2] for m in mixed], axis=1)
        block_e, block_v, slot_tok, slot_pos = _routing_tables(ridx_all, n_all)
        y4 = _moe(l, block_e, block_v, slot_tok, slot_pos, hn_all, params['exp_w1'], b1, params['exp_w2'], b2)
        xs = [_combine(mixed[s][0], y4, mixed[s][3], final_g.reshape(1, -1), tok0=tok0[s], n_all=n_all,
                       final=(l == DEPTH - 1)) for s in range(len(xs))]
        for s in range(len(xs)):
            for lst, val in zip(outs[s], mixed[s][4]):
                lst.append(val)
    return xs, [[jnp.stack(v) for v in o] for o in outs]


def kernel(x_prompt, x_sample, state_s5_re, state_s5_im, state_rg_h, state_rg_conv, state_ml_c, state_ml_n, state_ml_m, norm1_g, w_in, s5_lambda_re, s5_lambda_im, s5_log_step, s5_b_re, s5_b_im, s5_c_re, s5_c_im, s5_d, s5_glu_w, s5_glu_b, rg_conv_w, rg_conv_b, rg_wa, rg_ba, rg_wx, rg_bx, rg_lambda, ml_gate_bias, ml_norm_g, w_out, norm2_g, router_w, router_b, exp_w1, exp_b1, exp_w2, exp_b2, final_norm_g):
    params = dict(norm1_g=norm1_g, w_in=w_in, s5_lambda_re=s5_lambda_re, s5_lambda_im=s5_lambda_im,
                  s5_log_step=s5_log_step, s5_b_re=s5_b_re, s5_b_im=s5_b_im, s5_c_re=s5_c_re, s5_c_im=s5_c_im,
                  s5_d=s5_d, s5_glu_w=s5_glu_w, s5_glu_b=s5_glu_b, rg_conv_w=rg_conv_w, rg_conv_b=rg_conv_b,
                  rg_wa=rg_wa, rg_ba=rg_ba, rg_wx=rg_wx, rg_bx=rg_bx, rg_lambda=rg_lambda,
                  ml_gate_bias=ml_gate_bias, ml_norm_g=ml_norm_g, w_out=w_out, norm2_g=norm2_g,
                  router_w=router_w, router_b=router_b, exp_w1=exp_w1, exp_b1=exp_b1, exp_w2=exp_w2, exp_b2=exp_b2)
    bp, tp, _ = x_prompt.shape
    bs, ts, _ = x_sample.shape
    assert tp >= RG_CONV - 1 and ts >= RG_CONV - 1
    zeros = lambda *shape: jnp.zeros((DEPTH, bp) + shape, F32)
    prompt_states = (zeros(S5_GROUPS, S5_STATE), zeros(S5_GROUPS, S5_STATE), zeros(RG_WIDTH),
                     zeros(RG_CONV - 1, RG_WIDTH), zeros(ML_HEADS, ML_DK, ML_DV), zeros(ML_HEADS, ML_DK),
                     zeros(ML_HEADS))
    ts_pad = -(-ts // SUBLANES) * SUBLANES
    xs_pad = jnp.pad(x_sample, ((0, 0), (0, ts_pad - ts), (0, 0)))
    sample_states = (state_s5_re, state_s5_im, state_rg_h, state_rg_conv, state_ml_c, state_ml_n, state_ml_m)
    cfgs = [dict(nb=bp, t=tp, t_valid=tp, s5_tc=128, ml_bb=8, ml_cl=math.gcd(tp, ML_CHUNK)),
            dict(nb=bs, t=ts_pad, t_valid=ts, s5_tc=ts_pad, ml_bb=8, ml_cl=ts_pad)]
    (yp, ys), (sp, ss) = _trunks([x_prompt.reshape(bp * tp, D_MODEL), xs_pad.reshape(bs * ts_pad, D_MODEL)],
                                 [prompt_states, sample_states], cfgs, params, final_norm_g)
    return (yp.reshape(bp, tp, D_MODEL), ys.reshape(bs, ts_pad, D_MODEL)[:, :ts], *sp, *ss)
```

```python
import functools
import math

import jax
import jax.numpy as jnp
from jax import lax
from jax.experimental import pallas as pl
from jax.experimental.pallas import tpu as pltpu

F32 = jnp.float32
BF16 = jnp.bfloat16

SUBLANES = 8
LANES = 128
VMEM_LIMIT_BYTES = 56 * 1024 * 1024

D_MODEL = 1024
DEPTH = 2
S5_WIDTH = 256
S5_GROUP = 16
S5_GROUPS = 16
S5_STATE = 64
S5_FLAT = S5_GROUPS * S5_STATE
RG_WIDTH = 256
RG_BLOCKS = 8
RG_CONV = 4
RG_C = 8.0
ML_WIDTH = 512
ML_HEADS = 4
ML_DK = 128
ML_DV = 128
ML_CHUNK = 64
N_EXPERTS = 32
TOP_K = 4
D_FF = 1024
SWIGLU_LIMIT = 7.0
SWIGLU_ALPHA = 1.702
EPS = 1e-5

IN_S5 = (0, 256)
IN_RG = (256, 768)
IN_ML = (768, 2816)
IN_GATE = (2816, 2944)
N_IN = 2824
N_IN_PAD = 2944

ROW_TILE = 512
ROUTE_TILE = 128
MOE_TILE = 256
NEG_BIG = -1e30


def _cparams(sem):
    return pltpu.CompilerParams(dimension_semantics=sem, vmem_limit_bytes=VMEM_LIMIT_BYTES)


def _full(shape):
    n = len(shape)
    return pl.BlockSpec(shape, lambda *_: (0,) * n)


ROW_TILES = D_MODEL // LANES
assert ROW_TILES == SUBLANES


def _rows_to_tiles(x, dst_ref):
    m = x.shape[0]
    for j in range(ROW_TILES):
        dst_ref[pl.ds(j, m, stride=ROW_TILES), :] = x[:, j * LANES:(j + 1) * LANES]


def _tiles_to_rows(src_ref):
    m = src_ref.shape[0] // ROW_TILES
    return jnp.concatenate([src_ref[pl.ds(j, m, stride=ROW_TILES), :] for j in range(ROW_TILES)], axis=-1)


def _inproj_body(x_ref, g_ref, w_ref, zs5_ref, zrg_ref, zml_ref, zgt_ref):
    x = x_ref[...]
    h = x * lax.rsqrt(jnp.mean(x * x, axis=-1, keepdims=True) + EPS) * g_ref[...]
    hb = h.astype(BF16)
    for ref, (lo, hi) in ((zs5_ref, IN_S5), (zrg_ref, IN_RG), (zml_ref, IN_ML), (zgt_ref, IN_GATE)):
        ref[...] = jnp.dot(hb, w_ref[:, lo:hi], preferred_element_type=F32)


def _inproj(x, g, w):
    n = x.shape[0]
    tm = ROW_TILE
    widths = [hi - lo for lo, hi in (IN_S5, IN_RG, IN_ML, IN_GATE)]
    return pl.pallas_call(
        _inproj_body,
        grid=(n // tm,),
        in_specs=[pl.BlockSpec((tm, D_MODEL), lambda i: (i, 0)), _full((1, D_MODEL)), _full((D_MODEL, N_IN_PAD))],
        out_specs=[pl.BlockSpec((tm, w_), lambda i: (i, 0)) for w_ in widths],
        out_shape=[jax.ShapeDtypeStruct((n, w_), F32) for w_ in widths],
        compiler_params=_cparams(("arbitrary",)),
        name="inproj",
    )(x, g, w)


def _s5_body(u_ref, s0r_ref, s0i_ref, lam_ref, bre_ref, bim_ref, cre_ref, cim_ref, d_ref, gw_ref, gb_ref,
             y_ref, sr_ref, si_ref, bur, bui, ab, bbr, bbi, tm_s, bm_s, *, nb, tc, nc, t_valid):
    c = pl.program_id(0)

    @pl.when(c == 0)
    def _():
        sr_ref[...] = s0r_ref[...]
        si_ref[...] = s0i_ref[...]
        lr = lam_ref[0:1, :]
        li = lam_ref[1:2, :]
        step = jnp.exp(lam_ref[2:3, :])
        mag = jnp.exp(lr * step)
        ab_re = mag * jnp.cos(li * step)
        ab_im = mag * jnp.sin(li * step)
        den = lr * lr + li * li
        num_re = ab_re - 1.0
        z_re = (num_re * lr + ab_im * li) / den
        z_im = (ab_im * lr - num_re * li) / den
        ab[0:1, :] = ab_re
        ab[1:2, :] = ab_im
        bbr[...] = (z_re * bre_ref[...] - z_im * bim_ref[...]).astype(BF16)
        bbi[...] = (z_re * bim_ref[...] + z_im * bre_ref[...]).astype(BF16)

    _to_time_major(u_ref, tm_s, bm_s, nb, tc)
    u = jnp.concatenate([tm_s[j] for j in range(S5_WIDTH // LANES)], axis=-1)
    ub = u.astype(BF16)
    bur[...] = jnp.dot(ub, bbr[...], preferred_element_type=F32)
    bui[...] = jnp.dot(ub, bbi[...], preferred_element_type=F32)

    a_re = jnp.broadcast_to(ab[0:1, :], (SUBLANES, S5_FLAT))
    a_im = jnp.broadcast_to(ab[1:2, :], (SUBLANES, S5_FLAT))
    last_in = t_valid - (nc - 1) * tc
    steps = tc if last_in == tc else jnp.where(c == nc - 1, last_in, tc)

    def group(gi, _):
        g8 = gi * SUBLANES
        rows8 = pl.ds(pl.multiple_of(g8, SUBLANES), SUBLANES)

        def step_fn(t, carry):
            s_re, s_im = carry
            rows = pl.ds(pl.multiple_of(t * nb + g8, SUBLANES), SUBLANES)
            n_re = a_re * s_re - a_im * s_im + bur[rows, :]
            n_im = a_re * s_im + a_im * s_re + bui[rows, :]
            bur[rows, :] = n_re
            bui[rows, :] = n_im
            return n_re, n_im

        s_re, s_im = lax.fori_loop(0, steps, step_fn, (sr_ref[rows8, :], si_ref[rows8, :]))
        sr_ref[rows8, :] = s_re
        si_ref[rows8, :] = s_im
        return 0

    lax.fori_loop(0, nb // SUBLANES, group, 0)

    y = (jnp.dot(bur[...].astype(BF16), cre_ref[...], preferred_element_type=F32)
         - jnp.dot(bui[...].astype(BF16), cim_ref[...], preferred_element_type=F32))
    y = y + d_ref[...] * u
    g = jax.nn.gelu(y)
    gate = jax.nn.sigmoid(jnp.dot(g.astype(BF16), gw_ref[...], preferred_element_type=F32) + gb_ref[...])
    out = g * gate
    for j in range(S5_WIDTH // LANES):
        tm_s[j] = out[:, j * LANES:(j + 1) * LANES]
    _to_batch_major(tm_s, bm_s, y_ref, nb, tc)


def _to_time_major(src_ref, tm_s, bm_s, nb, tc):
    for j in range(tm_s.shape[0]):
        lanes = slice(j * LANES, (j + 1) * LANES)
        if nb <= tc:
            for b in range(nb):
                tm_s[j, pl.ds(b, tc, stride=nb), :] = src_ref[b, :, lanes]
        else:
            bm_s[j] = src_ref[:, :, lanes].reshape(nb * tc, LANES)
            for t in range(tc):
                tm_s[j, t * nb:(t + 1) * nb, :] = bm_s[j, pl.ds(t, nb, stride=tc), :]


def _to_batch_major(tm_s, bm_s, dst_ref, nb, tc):
    for j in range(tm_s.shape[0]):
        lanes = slice(j * LANES, (j + 1) * LANES)
        if nb <= tc:
            for b in range(nb):
                dst_ref[b, :, lanes] = tm_s[j, pl.ds(b, tc, stride=nb), :]
        else:
            for t in range(tc):
                bm_s[j, pl.ds(t, nb, stride=tc), :] = tm_s[j, t * nb:(t + 1) * nb, :]
            dst_ref[:, :, lanes] = bm_s[j].reshape(nb, tc, LANES)


def _s5(u, s0r, s0i, lam, bre, bim, cre, cim, d, gw, gb, *, tc, t_valid):
    nb, t, _ = u.shape
    nc = t // tc
    assert (nc - 1) * tc < t_valid <= t
    body = functools.partial(_s5_body, nb=nb, tc=tc, nc=nc, t_valid=t_valid)
    return pl.pallas_call(
        body,
        grid=(t // tc,),
        in_specs=[pl.BlockSpec((nb, tc, S5_WIDTH), lambda c: (0, c, 0)),
                  _full((nb, S5_FLAT)), _full((nb, S5_FLAT)), _full((SUBLANES, S5_FLAT)),
                  _full((S5_WIDTH, S5_FLAT)), _full((S5_WIDTH, S5_FLAT)),
                  _full((S5_FLAT, S5_WIDTH)), _full((S5_FLAT, S5_WIDTH)),
                  _full((1, S5_WIDTH)), _full((S5_WIDTH, S5_WIDTH)), _full((1, S5_WIDTH))],
        out_specs=[pl.BlockSpec((nb, tc, S5_WIDTH), lambda c: (0, c, 0)),
                   _full((nb, S5_FLAT)), _full((nb, S5_FLAT))],
        out_shape=[jax.ShapeDtypeStruct((nb, t, S5_WIDTH), F32),
                   jax.ShapeDtypeStruct((nb, S5_FLAT), F32), jax.ShapeDtypeStruct((nb, S5_FLAT), F32)],
        scratch_shapes=[pltpu.VMEM((nb * tc, S5_FLAT), F32), pltpu.VMEM((nb * tc, S5_FLAT), F32),
                        pltpu.VMEM((SUBLANES, S5_FLAT), F32),
                        pltpu.VMEM((S5_WIDTH, S5_FLAT), BF16), pltpu.VMEM((S5_WIDTH, S5_FLAT), BF16),
                        pltpu.VMEM((S5_WIDTH // LANES, nb * tc, LANES), F32),
                        pltpu.VMEM((S5_WIDTH // LANES, nb * tc, LANES), F32)],
        compiler_params=_cparams(("arbitrary",)),
        name="s5_scan",
    )(u, s0r, s0i, lam, bre, bim, cre, cim, d, gw, gb)


def _rglru_body(z_ref, h0_ref, cb0_ref, cw_ref, cb_ref, wa_ref, ba_ref, wx_ref, bx_ref, lam_ref,
                y_ref, h_ref, cbn_ref, xs, a_s, b_s, tm_s, bm_s, *, nb, tc, nc, t_valid):
    c = pl.program_id(0)
    hist = RG_CONV - 1

    @pl.when(c == 0)
    def _():
        h_ref[...] = h0_ref[...]
        xs[0:hist * nb, :] = cb0_ref[...]

    _to_time_major(z_ref, tm_s, bm_s, nb, tc)
    xs[hist * nb:(hist + tc) * nb, :] = jnp.concatenate([tm_s[j] for j in range(RG_WIDTH // LANES)], axis=-1)

    xc = jnp.zeros((tc * nb, RG_WIDTH), F32) + cb_ref[...]
    for j in range(RG_CONV):
        xc = xc + xs[j * nb:(j + tc) * nb, :] * cw_ref[j:j + 1, :]

    last_in = t_valid - (nc - 1) * tc
    last = tc if last_in == tc else jnp.where(c == nc - 1, last_in, tc)

    @pl.when(c == nc - 1)
    def _():
        cbn_ref[...] = xs[last_in * nb:(last_in + hist) * nb, :]

    xs[0:hist * nb, :] = xs[tc * nb:(tc + hist) * nb, :]

    xb = xc.astype(BF16)
    r = jax.nn.sigmoid(jnp.dot(xb, wa_ref[...], preferred_element_type=F32) + ba_ref[...])
    i = jax.nn.sigmoid(jnp.dot(xb, wx_ref[...], preferred_element_type=F32) + bx_ref[...])
    log_a = -RG_C * r * jax.nn.softplus(-lam_ref[...])
    a_s[...] = jnp.exp(log_a)
    th = jnp.tanh(log_a)
    b_s[...] = jnp.sqrt(-2.0 * th / (1.0 - th)) * (i * xc)

    def group(gi, _):
        g8 = gi * SUBLANES
        rows8 = pl.ds(pl.multiple_of(g8, SUBLANES), SUBLANES)

        def step_fn(t, h):
            rows = pl.ds(pl.multiple_of(t * nb + g8, SUBLANES), SUBLANES)
            h = a_s[rows, :] * h + b_s[rows, :]
            b_s[rows, :] = h
            return h

        h_ref[rows8, :] = lax.fori_loop(0, last, step_fn, h_ref[rows8, :])
        return 0

    lax.fori_loop(0, nb // SUBLANES, group, 0)
    for j in range(RG_WIDTH // LANES):
        tm_s[j] = b_s[:, j * LANES:(j + 1) * LANES]
    _to_batch_major(tm_s, bm_s, y_ref, nb, tc)
    y_ref[...] = y_ref[...] * jax.nn.gelu(z_ref[:, :, RG_WIDTH:2 * RG_WIDTH])


def _rglru(z, h0, cb0, cw, cb, wa, ba, wx, bx, lam, *, tc, t_valid):
    nb, t, _ = z.shape
    hist = RG_CONV - 1
    nc = t // tc
    assert (nc - 1) * tc + hist <= t_valid <= t
    body = functools.partial(_rglru_body, nb=nb, tc=tc, nc=nc, t_valid=t_valid)
    y, h, cbn = pl.pallas_call(
        body,
        grid=(t // tc,),
        in_specs=[pl.BlockSpec((nb, tc, 2 * RG_WIDTH), lambda c: (0, c, 0)),
                  _full((nb, RG_WIDTH)), _full((hist * nb, RG_WIDTH)),
                  _full((RG_CONV, RG_WIDTH)), _full((1, RG_WIDTH)),
                  _full((RG_WIDTH, RG_WIDTH)), _full((1, RG_WIDTH)),
                  _full((RG_WIDTH, RG_WIDTH)), _full((1, RG_WIDTH)), _full((1, RG_WIDTH))],
        out_specs=[pl.BlockSpec((nb, tc, RG_WIDTH), lambda c: (0, c, 0)),
                   _full((nb, RG_WIDTH)), _full((hist * nb, RG_WIDTH))],
        out_shape=[jax.ShapeDtypeStruct((nb, t, RG_WIDTH), F32),
                   jax.ShapeDtypeStruct((nb, RG_WIDTH), F32), jax.ShapeDtypeStruct((hist * nb, RG_WIDTH), F32)],
        scratch_shapes=[pltpu.VMEM(((tc + hist) * nb, RG_WIDTH), F32),
                        pltpu.VMEM((nb * tc, RG_WIDTH), F32), pltpu.VMEM((nb * tc, RG_WIDTH), F32),
                        pltpu.VMEM((RG_WIDTH // LANES, nb * tc, LANES), F32),
                        pltpu.VMEM((RG_WIDTH // LANES, nb * tc, LANES), F32)],
        compiler_params=_cparams(("arbitrary",)),
        name="rglru_scan",
    )(z, h0, cb0.transpose(1, 0, 2).reshape(hist * nb, RG_WIDTH), cw, cb, wa, ba, wx, bx, lam)
    return y, h, cbn.reshape(hist, nb, RG_WIDTH).transpose(1, 0, 2)


def _dot_nt(a, b, **kw):
    return lax.dot_general(a, b, (((1,), (1,)), ((), ())), preferred_element_type=F32, **kw)


def _dot_tn(a, b, **kw):
    return lax.dot_general(a, b, (((0,), (0,)), ((), ())), preferred_element_type=F32, **kw)


def _mlstm_body(q_ref, k_ref, v_ref, o_ref, g_ref, gb_ref, ng_ref, c0_ref, n0_ref, m0_ref,
                y_ref, cn_ref, nn_ref, mn_ref, *, bb, cl, t_valid):
    c = pl.program_id(1)

    @pl.when(c == 0)
    def _():
        cn_ref[...] = c0_ref[0]
        nn_ref[...] = n0_ref[...]
        mn_ref[...] = m0_ref[...]

    row = lax.broadcasted_iota(jnp.int32, (cl, cl), 0)
    col = lax.broadcasted_iota(jnp.int32, (cl, cl), 1)
    tril = (col <= row).astype(F32)
    keep = (col <= row) & (col + c * cl < t_valid)
    t_ok = (lax.broadcasted_iota(jnp.int32, (cl, LANES), 0) + c * cl) < t_valid
    lane = lax.broadcasted_iota(jnp.int32, (cl, LANES), 1)
    ones = jnp.ones((cl, LANES), F32)

    chains = [(bi, h) for bi in range(bb) for h in range(ML_HEADS)]
    hsl = lambda h: slice(h * ML_DK, (h + 1) * ML_DK)
    hi = lax.Precision.HIGHEST

    bcum, ig = [], []
    for bi in range(bb):
        graw = g_ref[bi] + gb_ref[...]
        lf = jnp.where(t_ok, jax.nn.log_sigmoid(graw), 0.0)
        bcum.append(jnp.dot(tril, lf, preferred_element_type=F32, precision=hi))
        ig.append(jnp.where(t_ok, graw, NEG_BIG))

    qf = {ch: q_ref[ch[0], :, hsl(ch[1])] for ch in chains}
    qh = {ch: qf[ch].astype(BF16) for ch in chains}
    kh = {ch: k_ref[ch[0], :, hsl(ch[1])] * (ML_DK ** -0.5) for ch in chains}
    vh = {ch: v_ref[ch[0], :, hsl(ch[1])].astype(BF16) for ch in chains}
    cm = {ch: cn_ref[ch[0], ch[1]] for ch in chains}
    nv = {ch: nn_ref[ch[0], ch[1]:ch[1] + 1, :] for ch in chains}
    m_old = {ch: mn_ref[ch[0], :, ch[1]:ch[1] + 1] for ch in chains}
    b_col = {ch: bcum[ch[0]][:, ML_HEADS + ch[1]:ML_HEADS + ch[1] + 1] for ch in chains}
    i_col = {ch: ig[ch[0]][:, ch[1]:ch[1] + 1] for ch in chains}

    w_row = {ch: _dot_nt(ones, jnp.where(lane == ch[1], i_col[ch] - b_col[ch], 0.0), precision=hi) for ch in chains}
    qk = {ch: _dot_nt(qh[ch], kh[ch].astype(BF16)) for ch in chains}
    qc = {ch: jnp.dot(qh[ch], cm[ch].astype(BF16), preferred_element_type=F32) for ch in chains}
    dmat = {ch: jnp.where(keep, b_col[ch] + w_row[ch], -jnp.inf) for ch in chains}
    dmax = {ch: jnp.max(dmat[ch], axis=-1, keepdims=True) for ch in chains}
    inter = {ch: b_col[ch] + m_old[ch] for ch in chains}
    mt = {ch: jnp.maximum(inter[ch], dmax[ch]) for ch in chains}
    s = {ch: qk[ch] * jnp.exp(dmat[ch] - mt[ch]) for ch in chains}
    sc = {ch: jnp.exp(inter[ch] - mt[ch]) for ch in chains}
    sv = {ch: jnp.dot(s[ch].astype(BF16), vh[ch], preferred_element_type=F32) for ch in chains}
    qn = {ch: jnp.sum(qf[ch] * nv[ch], axis=-1, keepdims=True) for ch in chains}
    den = {ch: jnp.sum(s[ch], axis=-1, keepdims=True) + sc[ch] * qn[ch] for ch in chains}
    hh = {ch: (sv[ch] + sc[ch] * qc[ch]) / jnp.maximum(jnp.abs(den[ch]), jnp.exp(-mt[ch])) for ch in chains}
    bl = {ch: b_col[ch][cl - 1:cl, :] for ch in chains}
    m_new = {ch: mt[ch][cl - 1:cl, :] for ch in chains}
    wk = {ch: jnp.exp(bl[ch] - b_col[ch] + i_col[ch] - m_new[ch]) * kh[ch] for ch in chains}
    decay = {ch: jnp.exp(bl[ch] + m_old[ch] - m_new[ch]) for ch in chains}
    kv = {ch: _dot_tn(wk[ch].astype(BF16), vh[ch]) for ch in chains}
    rms = {ch: lax.rsqrt(jnp.mean(hh[ch] * hh[ch], axis=-1, keepdims=True) + EPS) for ch in chains}
    for ch in chains:
        bi, h = ch
        cn_ref[bi, h] = decay[ch] * cm[ch] + kv[ch]
        nn_ref[bi, h:h + 1, :] = decay[ch] * nv[ch] + jnp.sum(wk[ch], axis=0, keepdims=True)
        mn_ref[bi, :, h:h + 1] = m_new[ch]
        y_ref[bi, :, hsl(h)] = hh[ch] * rms[ch] * ng_ref[:, hsl(h)] * jax.nn.sigmoid(o_ref[bi, :, hsl(h)])


def _mlstm(layer, zml, zgt, gbias, ng, c0, n0, m0, *, bb, cl, t_valid):
    nb, t, _ = zml.shape
    nc = t // cl
    body = functools.partial(_mlstm_body, bb=bb, cl=cl, t_valid=t_valid)
    zspec = lambda j: pl.BlockSpec((bb, cl, ML_WIDTH), lambda i, c, j=j: (i, c, j))
    return pl.pallas_call(
        body,
        grid=(nb // bb, nc),
        in_specs=[zspec(0), zspec(1), zspec(2), zspec(3),
                  pl.BlockSpec((bb, cl, LANES), lambda i, c: (i, c, 0)),
                  _full((1, LANES)), _full((1, ML_WIDTH)),
                  pl.BlockSpec((1, bb, ML_HEADS, ML_DK, ML_DV), lambda i, c: (layer, i, 0, 0, 0)),
                  pl.BlockSpec((bb, ML_HEADS, ML_DK), lambda i, c: (i, 0, 0)),
                  pl.BlockSpec((bb, 1, ML_HEADS), lambda i, c: (i, 0, 0))],
        out_specs=[pl.BlockSpec((bb, cl, ML_WIDTH), lambda i, c: (i, c, 0)),
                   pl.BlockSpec((bb, ML_HEADS, ML_DK, ML_DV), lambda i, c: (i, 0, 0, 0)),
                   pl.BlockSpec((bb, ML_HEADS, ML_DK), lambda i, c: (i, 0, 0)),
                   pl.BlockSpec((bb, 1, ML_HEADS), lambda i, c: (i, 0, 0))],
        out_shape=[jax.ShapeDtypeStruct((nb, t, ML_WIDTH), F32),
                   jax.ShapeDtypeStruct((nb, ML_HEADS, ML_DK, ML_DV), F32),
                   jax.ShapeDtypeStruct((nb, ML_HEADS, ML_DK), F32),
                   jax.ShapeDtypeStruct((nb, 1, ML_HEADS), F32)],
        compiler_params=_cparams(("arbitrary", "arbitrary")),
        name="mlstm_chunks",
    )(zml, zml, zml, zml, zgt, gbias, ng, c0, n0, m0)


def _outproj_body(x_ref, y1_ref, y2_ref, y3_ref, wo_ref, g2_ref, rw_ref, rb_ref,
                  xn_ref, hn_ref, ridx_ref, rgt_ref):
    acc = jnp.dot(y1_ref[...].astype(BF16), wo_ref[0:256, :], preferred_element_type=F32)
    acc = acc + jnp.dot(y2_ref[...].astype(BF16), wo_ref[256:512, :], preferred_element_type=F32)
    acc = acc + jnp.dot(y3_ref[...].astype(BF16), wo_ref[512:1024, :], preferred_element_type=F32)
    x = x_ref[...] + acc
    xn_ref[...] = x
    h = x * lax.rsqrt(jnp.mean(x * x, axis=-1, keepdims=True) + EPS) * g2_ref[...]
    _rows_to_tiles(h, hn_ref)
    hb = h.astype(BF16)
    tm = x.shape[0]
    eidx = lax.broadcasted_iota(jnp.int32, (N_EXPERTS, ROUTE_TILE), 0)
    sub = lax.broadcasted_iota(jnp.int32, (LANES, ROUTE_TILE), 0)
    for j in range(tm // ROUTE_TILE):
        rows = slice(j * ROUTE_TILE, (j + 1) * ROUTE_TILE)
        vals = _dot_nt(rw_ref[...], hb[rows, :]) + rb_ref[...]
        tops, idxs = [], []
        for _ in range(TOP_K):
            m = jnp.max(vals, axis=0, keepdims=True)
            idx = jnp.min(jnp.where(vals == m, eidx, N_EXPERTS), axis=0, keepdims=True)
            vals = jnp.where(eidx == idx, -jnp.inf, vals)
            tops.append(m)
            idxs.append(idx)
        exps = [jnp.exp(t - tops[0]) for t in tops]
        inv = 1.0 / functools.reduce(lambda a, b: a + b, exps)
        ridx_ref[:, rows] = jnp.concatenate(idxs + idxs, axis=0)
        gt = jnp.zeros((LANES, ROUTE_TILE), F32)
        for kk in range(TOP_K):
            gt = jnp.where(sub == kk, exps[kk] * inv, gt)
        rgt_ref[rows, :] = gt.T


def _outproj(x, y1, y2, y3, wo, g2, rw, rb):
    n = x.shape[0]
    tm = ROW_TILE
    rowspec = lambda w_: pl.BlockSpec((tm, w_), lambda i: (i, 0))
    return pl.pallas_call(
        _outproj_body,
        grid=(n // tm,),
        in_specs=[rowspec(D_MODEL), rowspec(S5_WIDTH), rowspec(RG_WIDTH), rowspec(ML_WIDTH),
                  _full((D_MODEL, D_MODEL)), _full((1, D_MODEL)),
                  _full((N_EXPERTS, D_MODEL)), _full((N_EXPERTS, 1))],
        out_specs=[rowspec(D_MODEL), pl.BlockSpec((tm * ROW_TILES, LANES), lambda i: (i, 0)),
                   pl.BlockSpec((2 * TOP_K, tm), lambda i: (0, i)), rowspec(LANES)],
        out_shape=[jax.ShapeDtypeStruct((n, D_MODEL), F32), jax.ShapeDtypeStruct((n * ROW_TILES, LANES), F32),
                   jax.ShapeDtypeStruct((2 * TOP_K, n), jnp.int32), jax.ShapeDtypeStruct((n, LANES), F32)],
        compiler_params=_cparams(("arbitrary",)),
        name="outproj_route",
    )(x, y1, y2, y3, wo, g2, rw, rb)


GATHER_AHEAD = 2
N_XBUF = GATHER_AHEAD + 1
N_OBUF = 3


def _moe_body(be_ref, bv_ref, tokn_ref, posp_ref, posc_ref, hn_ref, w1_ref, b1_ref, w2_ref, b2_ref, y4_ref,
              w1b, w2b, xbuf, obuf, gsem, ssem):
    i = pl.program_id(0)
    last = pl.num_programs(0) - 1
    real = bv_ref[i] > 0
    fresh = jnp.logical_or(i == GATHER_AHEAD, be_ref[i] != be_ref[jnp.maximum(i - 1, 0)])
    xs, xn = lax.rem(i, N_XBUF), lax.rem(i + GATHER_AHEAD, N_XBUF)
    os_, op, opp = lax.rem(i, N_OBUF), lax.rem(i + N_OBUF - 1, N_OBUF), lax.rem(i + N_OBUF - 2, N_OBUF)
    x_cur, x_nxt, o_cur, o_prev = xbuf.at[xs], xbuf.at[xn], obuf.at[os_], obuf.at[op]

    def tile(ref, row):
        return ref.at[pl.ds(pl.multiple_of(row, ROW_TILES), ROW_TILES)]

    def start_gathers():
        for r in range(MOE_TILE):
            pltpu.make_async_copy(tile(hn_ref, tokn_ref[0, 0, r]), tile(x_nxt, r * ROW_TILES),
                                  gsem.at[xn]).start(priority=0)

    def start_scatters(pos_ref, o_buf, sem):
        for r in range(MOE_TILE):
            pltpu.make_async_copy(tile(o_buf, r * ROW_TILES), tile(y4_ref, pos_ref[0, 0, r]), sem).start(priority=1)

    def wait_scatters(o_buf, sem):
        pltpu.make_async_copy(o_buf, y4_ref.at[pl.ds(0, MOE_TILE * ROW_TILES)], sem).wait()

    gathered = jnp.logical_and(i >= GATHER_AHEAD, jnp.logical_or(
        i < 2 * GATHER_AHEAD, bv_ref[jnp.maximum(i - GATHER_AHEAD, 0)] > 0))

    @pl.when(gathered)
    def _():
        pltpu.make_async_copy(hn_ref.at[pl.ds(0, MOE_TILE * ROW_TILES)], x_cur, gsem.at[xs]).wait()

    @pl.when(i >= N_OBUF)
    def _():
        wait_scatters(o_cur, ssem.at[os_])

    @pl.when(jnp.logical_and(fresh, real))
    def _():
        w1b[...] = w1_ref[0, 0].astype(BF16)
        w2b[...] = w2_ref[0, 0].astype(BF16)

    @pl.when(real)
    def _():
        xb = _tiles_to_rows(x_cur).astype(BF16)
        start_gathers()
        start_scatters(posp_ref, o_prev, ssem.at[op])
        hb = jnp.dot(xb, w1b[...], preferred_element_type=F32) + b1_ref[0, 0]
        g = jnp.minimum(hb[:, :D_FF], SWIGLU_LIMIT)
        u = jnp.clip(hb[:, D_FF:], -SWIGLU_LIMIT, SWIGLU_LIMIT)
        act = g * jax.nn.sigmoid(SWIGLU_ALPHA * g) * (u + 1.0)
        _rows_to_tiles(jnp.dot(act.astype(BF16), w2b[...], preferred_element_type=F32) + b2_ref[0, 0], o_cur)

    @pl.when(jnp.logical_not(real))
    def _():
        @pl.when(i == 0)
        def _():
            obuf[...] = jnp.zeros_like(obuf)

        @pl.when(i < GATHER_AHEAD)
        def _():
            start_gathers()

        @pl.when(i >= 1)
        def _():
            start_scatters(posp_ref, o_prev, ssem.at[op])

    @pl.when(i == last)
    def _():
        start_scatters(posc_ref, o_cur, ssem.at[os_])
        wait_scatters(obuf.at[opp], ssem.at[opp])
        wait_scatters(o_prev, ssem.at[op])
        wait_scatters(o_cur, ssem.at[os_])


def _moe(layer, block_e, block_v, slot_tok, slot_pos, hn, w1, b1, w2, b2):
    n_blocks = block_e.shape[0]
    tm = MOE_TILE
    n_rows = n_blocks * tm
    assert n_blocks > 2 * GATHER_AHEAD
    idx_spec = lambda f: pl.BlockSpec((1, 1, tm), f, memory_space=pltpu.SMEM)
    grid_spec = pltpu.PrefetchScalarGridSpec(
        num_scalar_prefetch=2,
        grid=(n_blocks,),
        in_specs=[idx_spec(lambda i, be, bv: (jnp.minimum(i + GATHER_AHEAD, n_blocks - 1), 0, 0)),
                  idx_spec(lambda i, be, bv: (jnp.maximum(i - 1, 0), 0, 0)),
                  idx_spec(lambda i, be, bv: (i, 0, 0)),
                  pl.BlockSpec(memory_space=pl.ANY),
                  pl.BlockSpec((1, 1, D_MODEL, 2 * D_FF), lambda i, be, bv: (layer, be[i], 0, 0)),
                  pl.BlockSpec((1, 1, 1, 2 * D_FF), lambda i, be, bv: (layer, be[i], 0, 0)),
                  pl.BlockSpec((1, 1, D_FF, D_MODEL), lambda i, be, bv: (layer, be[i], 0, 0)),
                  pl.BlockSpec((1, 1, 1, D_MODEL), lambda i, be, bv: (layer, be[i], 0, 0))],
        out_specs=pl.BlockSpec(memory_space=pl.ANY),
        scratch_shapes=[pltpu.VMEM((D_MODEL, 2 * D_FF), BF16), pltpu.VMEM((D_FF, D_MODEL), BF16)]
                       + [pltpu.VMEM((N_XBUF, tm * ROW_TILES, LANES), F32),
                          pltpu.VMEM((N_OBUF, tm * ROW_TILES, LANES), F32)]
                       + [pltpu.SemaphoreType.DMA((N_XBUF,)), pltpu.SemaphoreType.DMA((N_OBUF,))],
    )
    return pl.pallas_call(
        _moe_body,
        grid_spec=grid_spec,
        out_shape=jax.ShapeDtypeStruct((n_rows * ROW_TILES, LANES), F32),
        compiler_params=_cparams(("arbitrary",)),
        name="moe_experts",
    )(block_e, block_v, slot_tok, slot_pos, slot_pos, hn, w1, b1, w2, b2)


def _combine_body(x_ref, rgt_ref, fg_ref, *rest, final):
    y_refs, o_ref = rest[:TOP_K], rest[TOP_K]
    x = x_ref[...]
    for kk in range(TOP_K):
        x = x + rgt_ref[:, kk:kk + 1] * _tiles_to_rows(y_refs[kk])
    if final:
        x = x * lax.rsqrt(jnp.mean(x * x, axis=-1, keepdims=True) + EPS) * fg_ref[...]
    o_ref[...] = x


def _combine(x, y4, rgt, fg, *, tok0, n_all, final):
    n = x.shape[0]
    tm = ROW_TILE
    assert tok0 % tm == 0 and n_all % tm == 0
    yspec = lambda kk: pl.BlockSpec((tm * ROW_TILES, LANES), lambda i, kk=kk: ((kk * n_all + tok0) // tm + i, 0))
    return pl.pallas_call(
        functools.partial(_combine_body, final=final),
        grid=(n // tm,),
        in_specs=[pl.BlockSpec((tm, D_MODEL), lambda i: (i, 0)),
                  pl.BlockSpec((tm, LANES), lambda i: (i, 0)), _full((1, D_MODEL))]
                 + [yspec(kk) for kk in range(TOP_K)],
        out_specs=pl.BlockSpec((tm, D_MODEL), lambda i: (i, 0)),
        out_shape=jax.ShapeDtypeStruct((n, D_MODEL), F32),
        compiler_params=_cparams(("arbitrary",)),
        name="moe_combine",
    )(x, rgt, fg, *([y4] * TOP_K))


def _routing_tables(ridx, n):
    na = n * TOP_K
    tm = MOE_TILE
    assert na % tm == 0
    lead = GATHER_AHEAD * tm
    n_blocks = na // tm + N_EXPERTS + 2 * GATHER_AHEAD
    n_slots = n_blocks * tm
    experts = jnp.arange(N_EXPERTS, dtype=jnp.int32)
    flat_e = ridx[:TOP_K, :].reshape(-1)
    order = jnp.argsort(flat_e).astype(jnp.int32)
    counts = jnp.sum(flat_e[None, :] == experts[:, None], axis=1).astype(jnp.int32)
    padded = (counts + tm - 1) // tm * tm
    pad_end = jnp.cumsum(padded)
    pad_start = pad_end - padded
    start = jnp.cumsum(counts) - counts
    spare_start = na + pad_start - start
    total = pad_end[-1]
    blk = jnp.arange(n_blocks, dtype=jnp.int32) * tm - lead
    block_v = ((blk >= 0) & (blk < total)).astype(jnp.int32)
    last_e = jnp.max(jnp.where(counts > 0, experts, 0))
    block_e = jnp.minimum(jnp.sum(blk[:, None] >= pad_end[None, :], axis=1), last_e).astype(jnp.int32)
    rank = (blk - pad_start[block_e])[:, None] + jnp.arange(tm, dtype=jnp.int32)[None, :]
    cnt = counts[block_e][:, None]
    slot = blk[:, None] + jnp.arange(tm, dtype=jnp.int32)[None, :]
    used = (slot >= 0) & (slot < total)
    real = (rank < cnt) & used
    src = order[jnp.clip(start[block_e][:, None] + rank, 0, na - 1)]
    slot_tok = (jnp.where(real, src % n, 0) * ROW_TILES).astype(jnp.int32)
    spare = jnp.where(used, spare_start[block_e][:, None] + rank - cnt, jnp.where(slot >= 0, slot, n_slots + slot))
    slot_pos = (jnp.where(real, src, spare) * ROW_TILES).astype(jnp.int32)
    return block_e, block_v, slot_tok.reshape(n_blocks, 1, tm), slot_pos.reshape(n_blocks, 1, tm)


def _blockdiag(m):
    g, a, b = m.shape
    eye = jnp.eye(g, dtype=m.dtype)
    return (eye[:, None, :, None] * m[:, :, None, :]).reshape(g * a, g * b)


def _layer_params(l, p):
    lam = jnp.zeros((SUBLANES, S5_FLAT), F32)
    lam = lam.at[0].set(p['s5_lambda_re'][l].reshape(-1)).at[1].set(p['s5_lambda_im'][l].reshape(-1))
    lam = lam.at[2].set(jnp.repeat(p['s5_log_step'][l], S5_STATE))
    gbias = jnp.zeros((1, LANES), F32).at[0, :2 * ML_HEADS].set(p['ml_gate_bias'][l])
    w_in = jnp.pad(p['w_in'][l], ((0, 0), (0, N_IN_PAD - N_IN))).astype(BF16)
    return dict(
        norm1_g=p['norm1_g'][l].reshape(1, -1), w_in=w_in, s5_lam=lam,
        s5_bre=_blockdiag(p['s5_b_re'][l].transpose(0, 2, 1)), s5_bim=_blockdiag(p['s5_b_im'][l].transpose(0, 2, 1)),
        s5_cre=_blockdiag(p['s5_c_re'][l].transpose(0, 2, 1)).astype(BF16),
        s5_cim=_blockdiag(p['s5_c_im'][l].transpose(0, 2, 1)).astype(BF16),
        s5_d=p['s5_d'][l].reshape(1, -1), s5_gw=p['s5_glu_w'][l].astype(BF16), s5_gb=p['s5_glu_b'][l].reshape(1, -1),
        rg_cw=p['rg_conv_w'][l], rg_cb=p['rg_conv_b'][l].reshape(1, -1),
        rg_wa=_blockdiag(p['rg_wa'][l]).astype(BF16), rg_ba=p['rg_ba'][l].reshape(1, -1),
        rg_wx=_blockdiag(p['rg_wx'][l]).astype(BF16), rg_bx=p['rg_bx'][l].reshape(1, -1),
        rg_lam=p['rg_lambda'][l].reshape(1, -1),
        ml_gb=gbias, ml_ng=p['ml_norm_g'][l].reshape(1, -1),
        w_out=p['w_out'][l].astype(BF16), norm2_g=p['norm2_g'][l].reshape(1, -1),
        router_w=p['router_w'][l].T.astype(BF16), router_b=p['router_b'][l].reshape(-1, 1),
    )


def _mix_and_project(l, lp, xf, states, cfg):
    nb, t, t_valid = cfg['nb'], cfg['t'], cfg['t_valid']
    n = nb * t
    s5r, s5i, rgh, rgc, mlc, mln, mlm = states
    zs5, zrg, zml, zgt = _inproj(xf, lp['norm1_g'], lp['w_in'])
    y1, nr, ni = _s5(zs5.reshape(nb, t, -1), s5r[l].reshape(nb, -1), s5i[l].reshape(nb, -1), lp['s5_lam'],
                     lp['s5_bre'], lp['s5_bim'], lp['s5_cre'], lp['s5_cim'], lp['s5_d'], lp['s5_gw'], lp['s5_gb'],
                     tc=cfg['s5_tc'], t_valid=t_valid)
    y2, nh, ncv = _rglru(zrg.reshape(nb, t, -1), rgh[l], rgc[l], lp['rg_cw'], lp['rg_cb'], lp['rg_wa'],
                         lp['rg_ba'], lp['rg_wx'], lp['rg_bx'], lp['rg_lam'], tc=cfg['s5_tc'], t_valid=t_valid)
    y3, nc_, nn_, nm_ = _mlstm(l, zml.reshape(nb, t, -1), zgt.reshape(nb, t, -1), lp['ml_gb'], lp['ml_ng'],
                               mlc, mln[l], mlm[l].reshape(nb, 1, ML_HEADS),
                               bb=cfg['ml_bb'], cl=cfg['ml_cl'], t_valid=t_valid)
    xn, hn, ridx, rgt = _outproj(xf, y1.reshape(n, -1), y2.reshape(n, -1), y3.reshape(n, -1),
                                 lp['w_out'], lp['norm2_g'], lp['router_w'], lp['router_b'])
    new_states = (nr.reshape(nb, S5_GROUPS, S5_STATE), ni.reshape(nb, S5_GROUPS, S5_STATE), nh, ncv, nc_, nn_,
                  nm_.reshape(nb, ML_HEADS))
    return xn, hn, ridx, rgt, new_states


def _trunks(xs, states, cfgs, params, final_g):
    n_all = sum(x.shape[0] for x in xs)
    tok0 = [sum(x.shape[0] for x in xs[:s]) for s in range(len(xs))]
    outs = [[[] for _ in range(7)] for _ in xs]
    b1 = params['exp_b1'].reshape(DEPTH, N_EXPERTS, 1, -1)
    b2 = params['exp_b2'].reshape(DEPTH, N_EXPERTS, 1, -1)
    for l in range(DEPTH):
        lp = _layer_params(l, params)
        mixed = [_mix_and_project(l, lp, xs[s], states[s], cfgs[s]) for s in range(len(xs))]
        hn_all = jnp.concatenate([m[1] for m in mixed], axis=0)
        ridx_all = jnp.concatenate([m[2] for m in mixed], axis=1)
        block_e, block_v, slot_tok, slot_pos = _routing_tables(ridx_all, n_all)
        y4 = _moe(l, block_e, block_v, slot_tok, slot_pos, hn_all, params['exp_w1'], b1, params['exp_w2'], b2)
        xs = [_combine(mixed[s][0], y4, mixed[s][3], final_g.reshape(1, -1), tok0=tok0[s], n_all=n_all,
                       final=(l == DEPTH - 1)) for s in range(len(xs))]
        for s in range(len(xs)):
            for lst, val in zip(outs[s], mixed[s][4]):
                lst.append(val)
    return xs, [[jnp.stack(v) for v in o] for o in outs]


def kernel(x_prompt, x_sample, state_s5_re, state_s5_im, state_rg_h, state_rg_conv, state_ml_c, state_ml_n, state_ml_m, norm1_g, w_in, s5_lambda_re, s5_lambda_im, s5_log_step, s5_b_re, s5_b_im, s5_c_re, s5_c_im, s5_d, s5_glu_w, s5_glu_b, rg_conv_w, rg_conv_b, rg_wa, rg_ba, rg_wx, rg_bx, rg_lambda, ml_gate_bias, ml_norm_g, w_out, norm2_g, router_w, router_b, exp_w1, exp_b1, exp_w2, exp_b2, final_norm_g):
    params = dict(norm1_g=norm1_g, w_in=w_in, s5_lambda_re=s5_lambda_re, s5_lambda_im=s5_lambda_im,
                  s5_log_step=s5_log_step, s5_b_re=s5_b_re, s5_b_im=s5_b_im, s5_c_re=s5_c_re, s5_c_im=s5_c_im,
                  s5_d=s5_d, s5_glu_w=s5_glu_w, s5_glu_b=s5_glu_b, rg_conv_w=rg_conv_w, rg_conv_b=rg_conv_b,
                  rg_wa=rg_wa, rg_ba=rg_ba, rg_wx=rg_wx, rg_bx=rg_bx, rg_lambda=rg_lambda,
                  ml_gate_bias=ml_gate_bias, ml_norm_g=ml_norm_g, w_out=w_out, norm2_g=norm2_g,
                  router_w=router_w, router_b=router_b, exp_w1=exp_w1, exp_b1=exp_b1, exp_w2=exp_w2, exp_b2=exp_b2)
    bp, tp, _ = x_prompt.shape
    bs, ts, _ = x_sample.shape
    assert tp >= RG_CONV - 1 and ts >= RG_CONV - 1
    zeros = lambda *shape: jnp.zeros((DEPTH, bp) + shape, F32)
    prompt_states = (zeros(S5_GROUPS, S5_STATE), zeros(S5_GROUPS, S5_STATE), zeros(RG_WIDTH),
                     zeros(RG_CONV - 1, RG_WIDTH), zeros(ML_HEADS, ML_DK, ML_DV), zeros(ML_HEADS, ML_DK),
                     zeros(ML_HEADS))
    ts_pad = -(-ts // SUBLANES) * SUBLANES
    xs_pad = jnp.pad(x_sample, ((0, 0), (0, ts_pad - ts), (0, 0)))
    sample_states = (state_s5_re, state_s5_im, state_rg_h, state_rg_conv, state_ml_c, state_ml_n, state_ml_m)
    cfgs = [dict(nb=bp, t=tp, t_valid=tp, s5_tc=128, ml_bb=8, ml_cl=math.gcd(tp, ML_CHUNK)),
            dict(nb=bs, t=ts_pad, t_valid=ts, s5_tc=ts_pad, ml_bb=8, ml_cl=ts_pad)]
    (yp, ys), (sp, ss) = _trunks([x_prompt.reshape(bp * tp, D_MODEL), xs_pad.reshape(bs * ts_pad, D_MODEL)],
                                 [prompt_states, sample_states], cfgs, params, final_norm_g)
    return (yp.reshape(bp, tp, D_MODEL), ys.reshape(bs, ts_pad, D_MODEL)[:, :ts], *sp, *ss)
```

```python
import functools
import math

import jax
import jax.numpy as jnp
from jax import lax
from jax.experimental import pallas as pl
from jax.experimental.pallas import tpu as pltpu

F32 = jnp.float32
BF16 = jnp.bfloat16

SUBLANES = 8
LANES = 128
VMEM_LIMIT_BYTES = 56 * 1024 * 1024

D_MODEL = 1024
DEPTH = 2
S5_WIDTH = 256
S5_GROUP = 16
S5_GROUPS = 16
S5_STATE = 64
S5_FLAT = S5_GROUPS * S5_STATE
RG_WIDTH = 256
RG_BLOCKS = 8
RG_CONV = 4
RG_C = 8.0
ML_WIDTH = 512
ML_HEADS = 4
ML_DK = 128
ML_DV = 128
ML_CHUNK = 64
N_EXPERTS = 32
TOP_K = 4
D_FF = 1024
SWIGLU_LIMIT = 7.0
SWIGLU_ALPHA = 1.702
EPS = 1e-5

IN_S5 = (0, 256)
IN_RG = (256, 768)
IN_ML = (768, 2816)
IN_GATE = (2816, 2944)
N_IN = 2824
N_IN_PAD = 2944

ROW_TILE = 512
ROUTE_TILE = 128
MOE_TILE = 256
NEG_BIG = -1e30


def _cparams(sem):
    return pltpu.CompilerParams(dimension_semantics=sem, vmem_limit_bytes=VMEM_LIMIT_BYTES)


def _full(shape):
    n = len(shape)
    return pl.BlockSpec(shape, lambda *_: (0,) * n)


ROW_TILES = D_MODEL // LANES
assert ROW_TILES == SUBLANES


def _rows_to_tiles(x, dst_ref):
    m = x.shape[0]
    for j in range(ROW_TILES):
        dst_ref[pl.ds(j, m, stride=ROW_TILES), :] = x[:, j * LANES:(j + 1) * LANES]


def _tiles_to_rows(src_ref):
    m = src_ref.shape[0] // ROW_TILES
    return jnp.concatenate([src_ref[pl.ds(j, m, stride=ROW_TILES), :] for j in range(ROW_TILES)], axis=-1)


def _inproj_body(x_ref, g_ref, w_ref, zs5_ref, zrg_ref, zml_ref, zgt_ref):
    x = x_ref[...]
    h = x * lax.rsqrt(jnp.mean(x * x, axis=-1, keepdims=True) + EPS) * g_ref[...]
    hb = h.astype(BF16)
    for ref, (lo, hi) in ((zs5_ref, IN_S5), (zrg_ref, IN_RG), (zml_ref, IN_ML), (zgt_ref, IN_GATE)):
        ref[...] = jnp.dot(hb, w_ref[:, lo:hi], preferred_element_type=F32)


def _inproj(x, g, w):
    n = x.shape[0]
    tm = ROW_TILE
    widths = [hi - lo for lo, hi in (IN_S5, IN_RG, IN_ML, IN_GATE)]
    return pl.pallas_call(
        _inproj_body,
        grid=(n // tm,),
        in_specs=[pl.BlockSpec((tm, D_MODEL), lambda i: (i, 0)), _full((1, D_MODEL)), _full((D_MODEL, N_IN_PAD))],
        out_specs=[pl.BlockSpec((tm, w_), lambda i: (i, 0)) for w_ in widths],
        out_shape=[jax.ShapeDtypeStruct((n, w_), F32) for w_ in widths],
        compiler_params=_cparams(("arbitrary",)),
        name="inproj",
    )(x, g, w)


def _s5_body(u_ref, s0r_ref, s0i_ref, lam_ref, bre_ref, bim_ref, cre_ref, cim_ref, d_ref, gw_ref, gb_ref,
             y_ref, sr_ref, si_ref, bur, bui, ab, bbr, bbi, tm_s, bm_s, *, nb, tc, nc, t_valid):
    c = pl.program_id(0)

    @pl.when(c == 0)
    def _():
        sr_ref[...] = s0r_ref[...]
        si_ref[...] = s0i_ref[...]
        lr = lam_ref[0:1, :]
        li = lam_ref[1:2, :]
        step = jnp.exp(lam_ref[2:3, :])
        mag = jnp.exp(lr * step)
        ab_re = mag * jnp.cos(li * step)
        ab_im = mag * jnp.sin(li * step)
        den = lr * lr + li * li
        num_re = ab_re - 1.0
        z_re = (num_re * lr + ab_im * li) / den
        z_im = (ab_im * lr - num_re * li) / den
        ab[0:1, :] = ab_re
        ab[1:2, :] = ab_im
        bbr[...] = (z_re * bre_ref[...] - z_im * bim_ref[...]).astype(BF16)
        bbi[...] = (z_re * bim_ref[...] + z_im * bre_ref[...]).astype(BF16)

    _to_time_major(u_ref, tm_s, bm_s, nb, tc)
    u = jnp.concatenate([tm_s[j] for j in range(S5_WIDTH // LANES)], axis=-1)
    ub = u.astype(BF16)
    bur[...] = jnp.dot(ub, bbr[...], preferred_element_type=F32)
    bui[...] = jnp.dot(ub, bbi[...], preferred_element_type=F32)

    a_re = jnp.broadcast_to(ab[0:1, :], (SUBLANES, S5_FLAT))
    a_im = jnp.broadcast_to(ab[1:2, :], (SUBLANES, S5_FLAT))
    last_in = t_valid - (nc - 1) * tc
    steps = tc if last_in == tc else jnp.where(c == nc - 1, last_in, tc)

    def group(gi, _):
        g8 = gi * SUBLANES
        rows8 = pl.ds(pl.multiple_of(g8, SUBLANES), SUBLANES)

        def step_fn(t, carry):
            s_re, s_im = carry
            rows = pl.ds(pl.multiple_of(t * nb + g8, SUBLANES), SUBLANES)
            n_re = a_re * s_re - a_im * s_im + bur[rows, :]
            n_im = a_re * s_im + a_im * s_re + bui[rows, :]
            bur[rows, :] = n_re
            bui[rows, :] = n_im
            return n_re, n_im

        s_re, s_im = lax.fori_loop(0, steps, step_fn, (sr_ref[rows8, :], si_ref[rows8, :]))
        sr_ref[rows8, :] = s_re
        si_ref[rows8, :] = s_im
        return 0

    lax.fori_loop(0, nb // SUBLANES, group, 0)

    y = (jnp.dot(bur[...].astype(BF16), cre_ref[...], preferred_element_type=F32)
         - jnp.dot(bui[...].astype(BF16), cim_ref[...], preferred_element_type=F32))
    y = y + d_ref[...] * u
    g = jax.nn.gelu(y)
    gate = jax.nn.sigmoid(jnp.dot(g.astype(BF16), gw_ref[...], preferred_element_type=F32) + gb_ref[...])
    out = g * gate
    for j in range(S5_WIDTH // LANES):
        tm_s[j] = out[:, j * LANES:(j + 1) * LANES]
    _to_batch_major(tm_s, bm_s, y_ref, nb, tc)


def _to_time_major(src_ref, tm_s, bm_s, nb, tc):
    for j in range(tm_s.shape[0]):
        lanes = slice(j * LANES, (j + 1) * LANES)
        if nb <= tc:
            for b in range(nb):
                tm_s[j, pl.ds(b, tc, stride=nb), :] = src_ref[b, :, lanes]
        else:
            bm_s[j] = src_ref[:, :, lanes].reshape(nb * tc, LANES)
            for t in range(tc):
                tm_s[j, t * nb:(t + 1) * nb, :] = bm_s[j, pl.ds(t, nb, stride=tc), :]


def _to_batch_major(tm_s, bm_s, dst_ref, nb, tc):
    for j in range(tm_s.shape[0]):
        lanes = slice(j * LANES, (j + 1) * LANES)
        if nb <= tc:
            for b in range(nb):
                dst_ref[b, :, lanes] = tm_s[j, pl.ds(b, tc, stride=nb), :]
        else:
            for t in range(tc):
                bm_s[j, pl.ds(t, nb, stride=tc), :] = tm_s[j, t * nb:(t + 1) * nb, :]
            dst_ref[:, :, lanes] = bm_s[j].reshape(nb, tc, LANES)


def _s5(u, s0r, s0i, lam, bre, bim, cre, cim, d, gw, gb, *, tc, t_valid):
    nb, t, _ = u.shape
    nc = t // tc
    assert (nc - 1) * tc < t_valid <= t
    body = functools.partial(_s5_body, nb=nb, tc=tc, nc=nc, t_valid=t_valid)
    return pl.pallas_call(
        body,
        grid=(t // tc,),
        in_specs=[pl.BlockSpec((nb, tc, S5_WIDTH), lambda c: (0, c, 0)),
                  _full((nb, S5_FLAT)), _full((nb, S5_FLAT)), _full((SUBLANES, S5_FLAT)),
                  _full((S5_WIDTH, S5_FLAT)), _full((S5_WIDTH, S5_FLAT)),
                  _full((S5_FLAT, S5_WIDTH)), _full((S5_FLAT, S5_WIDTH)),
                  _full((1, S5_WIDTH)), _full((S5_WIDTH, S5_WIDTH)), _full((1, S5_WIDTH))],
        out_specs=[pl.BlockSpec((nb, tc, S5_WIDTH), lambda c: (0, c, 0)),
                   _full((nb, S5_FLAT)), _full((nb, S5_FLAT))],
        out_shape=[jax.ShapeDtypeStruct((nb, t, S5_WIDTH), F32),
                   jax.ShapeDtypeStruct((nb, S5_FLAT), F32), jax.ShapeDtypeStruct((nb, S5_FLAT), F32)],
        scratch_shapes=[pltpu.VMEM((nb * tc, S5_FLAT), F32), pltpu.VMEM((nb * tc, S5_FLAT), F32),
                        pltpu.VMEM((SUBLANES, S5_FLAT), F32),
                        pltpu.VMEM((S5_WIDTH, S5_FLAT), BF16), pltpu.VMEM((S5_WIDTH, S5_FLAT), BF16),
                        pltpu.VMEM((S5_WIDTH // LANES, nb * tc, LANES), F32),
                        pltpu.VMEM((S5_WIDTH // LANES, nb * tc, LANES), F32)],
        compiler_params=_cparams(("arbitrary",)),
        name="s5_scan",
    )(u, s0r, s0i, lam, bre, bim, cre, cim, d, gw, gb)


def _rglru_body(z_ref, h0_ref, cb0_ref, cw_ref, cb_ref, wa_ref, ba_ref, wx_ref, bx_ref, lam_ref,
                y_ref, h_ref, cbn_ref, xs, a_s, b_s, tm_s, bm_s, *, nb, tc, nc, t_valid):
    c = pl.program_id(0)
    hist = RG_CONV - 1

    @pl.when(c == 0)
    def _():
        h_ref[...] = h0_ref[...]
        xs[0:hist * nb, :] = cb0_ref[...]

    _to_time_major(z_ref, tm_s, bm_s, nb, tc)
    xs[hist * nb:(hist + tc) * nb, :] = jnp.concatenate([tm_s[j] for j in range(RG_WIDTH // LANES)], axis=-1)

    xc = jnp.zeros((tc * nb, RG_WIDTH), F32) + cb_ref[...]
    for j in range(RG_CONV):
        xc = xc + xs[j * nb:(j + tc) * nb, :] * cw_ref[j:j + 1, :]

    last_in = t_valid - (nc - 1) * tc
    last = tc if last_in == tc else jnp.where(c == nc - 1, last_in, tc)

    @pl.when(c == nc - 1)
    def _():
        cbn_ref[...] = xs[last_in * nb:(last_in + hist) * nb, :]

    xs[0:hist * nb, :] = xs[tc * nb:(tc + hist) * nb, :]

    xb = xc.astype(BF16)
    r = jax.nn.sigmoid(jnp.dot(xb, wa_ref[...], preferred_element_type=F32) + ba_ref[...])
    i = jax.nn.sigmoid(jnp.dot(xb, wx_ref[...], preferred_element_type=F32) + bx_ref[...])
    log_a = -RG_C * r * jax.nn.softplus(-lam_ref[...])
    a_s[...] = jnp.exp(log_a)
    th = jnp.tanh(log_a)
    b_s[...] = jnp.sqrt(-2.0 * th / (1.0 - th)) * (i * xc)

    def group(gi, _):
        g8 = gi * SUBLANES
        rows8 = pl.ds(pl.multiple_of(g8, SUBLANES), SUBLANES)

        def step_fn(t, h):
            rows = pl.ds(pl.multiple_of(t * nb + g8, SUBLANES), SUBLANES)
            h = a_s[rows, :] * h + b_s[rows, :]
            b_s[rows, :] = h
            return h

        h_ref[rows8, :] = lax.fori_loop(0, last, step_fn, h_ref[rows8, :])
        return 0

    lax.fori_loop(0, nb // SUBLANES, group, 0)
    for j in range(RG_WIDTH // LANES):
        tm_s[j] = b_s[:, j * LANES:(j + 1) * LANES]
    _to_batch_major(tm_s, bm_s, y_ref, nb, tc)
    y_ref[...] = y_ref[...] * jax.nn.gelu(z_ref[:, :, RG_WIDTH:2 * RG_WIDTH])


def _rglru(z, h0, cb0, cw, cb, wa, ba, wx, bx, lam, *, tc, t_valid):
    nb, t, _ = z.shape
    hist = RG_CONV - 1
    nc = t // tc
    assert (nc - 1) * tc + hist <= t_valid <= t
    body = functools.partial(_rglru_body, nb=nb, tc=tc, nc=nc, t_valid=t_valid)
    y, h, cbn = pl.pallas_call(
        body,
        grid=(t // tc,),
        in_specs=[pl.BlockSpec((nb, tc, 2 * RG_WIDTH), lambda c: (0, c, 0)),
                  _full((nb, RG_WIDTH)), _full((hist * nb, RG_WIDTH)),
                  _full((RG_CONV, RG_WIDTH)), _full((1, RG_WIDTH)),
                  _full((RG_WIDTH, RG_WIDTH)), _full((1, RG_WIDTH)),
                  _full((RG_WIDTH, RG_WIDTH)), _full((1, RG_WIDTH)), _full((1, RG_WIDTH))],
        out_specs=[pl.BlockSpec((nb, tc, RG_WIDTH), lambda c: (0, c, 0)),
                   _full((nb, RG_WIDTH)), _full((hist * nb, RG_WIDTH))],
        out_shape=[jax.ShapeDtypeStruct((nb, t, RG_WIDTH), F32),
                   jax.ShapeDtypeStruct((nb, RG_WIDTH), F32), jax.ShapeDtypeStruct((hist * nb, RG_WIDTH), F32)],
        scratch_shapes=[pltpu.VMEM(((tc + hist) * nb, RG_WIDTH), F32),
                        pltpu.VMEM((nb * tc, RG_WIDTH), F32), pltpu.VMEM((nb * tc, RG_WIDTH), F32),
                        pltpu.VMEM((RG_WIDTH // LANES, nb * tc, LANES), F32),
                        pltpu.VMEM((RG_WIDTH // LANES, nb * tc, LANES), F32)],
        compiler_params=_cparams(("arbitrary",)),
        name="rglru_scan",
    )(z, h0, cb0.transpose(1, 0, 2).reshape(hist * nb, RG_WIDTH), cw, cb, wa, ba, wx, bx, lam)
    return y, h, cbn.reshape(hist, nb, RG_WIDTH).transpose(1, 0, 2)


def _dot_nt(a, b, **kw):
    return lax.dot_general(a, b, (((1,), (1,)), ((), ())), preferred_element_type=F32, **kw)


def _dot_tn(a, b, **kw):
    return lax.dot_general(a, b, (((0,), (0,)), ((), ())), preferred_element_type=F32, **kw)


def _mlstm_body(q_ref, k_ref, v_ref, o_ref, g_ref, gb_ref, ng_ref, c0_ref, n0_ref, m0_ref,
                y_ref, cn_ref, nn_ref, mn_ref, *, bb, cl, t_valid):
    c = pl.program_id(1)

    @pl.when(c == 0)
    def _():
        cn_ref[...] = c0_ref[0]
        nn_ref[...] = n0_ref[...]
        mn_ref[...] = m0_ref[...]

    row = lax.broadcasted_iota(jnp.int32, (cl, cl), 0)
    col = lax.broadcasted_iota(jnp.int32, (cl, cl), 1)
    tril = (col <= row).astype(F32)
    keep = (col <= row) & (col + c * cl < t_valid)
    t_ok = (lax.broadcasted_iota(jnp.int32, (cl, LANES), 0) + c * cl) < t_valid
    lane = lax.broadcasted_iota(jnp.int32, (cl, LANES), 1)
    ones = jnp.ones((cl, LANES), F32)

    chains = [(bi, h) for bi in range(bb) for h in range(ML_HEADS)]
    hsl = lambda h: slice(h * ML_DK, (h + 1) * ML_DK)
    hi = lax.Precision.HIGHEST

    bcum, ig = [], []
    for bi in range(bb):
        graw = g_ref[bi] + gb_ref[...]
        lf = jnp.where(t_ok, jax.nn.log_sigmoid(graw), 0.0)
        bcum.append(jnp.dot(tril, lf, preferred_element_type=F32, precision=hi))
        ig.append(jnp.where(t_ok, graw, NEG_BIG))

    qf = {ch: q_ref[ch[0], :, hsl(ch[1])] for ch in chains}
    qh = {ch: qf[ch].astype(BF16) for ch in chains}
    kh = {ch: k_ref[ch[0], :, hsl(ch[1])] * (ML_DK ** -0.5) for ch in chains}
    vh = {ch: v_ref[ch[0], :, hsl(ch[1])].astype(BF16) for ch in chains}
    cm = {ch: cn_ref[ch[0], ch[1]] for ch in chains}
    nv = {ch: nn_ref[ch[0], ch[1]:ch[1] + 1, :] for ch in chains}
    m_old = {ch: mn_ref[ch[0], :, ch[1]:ch[1] + 1] for ch in chains}
    b_col = {ch: bcum[ch[0]][:, ML_HEADS + ch[1]:ML_HEADS + ch[1] + 1] for ch in chains}
    i_col = {ch: ig[ch[0]][:, ch[1]:ch[1] + 1] for ch in chains}

    w_row = {ch: _dot_nt(ones, jnp.where(lane == ch[1], i_col[ch] - b_col[ch], 0.0), precision=hi) for ch in chains}
    qk = {ch: _dot_nt(qh[ch], kh[ch].astype(BF16)) for ch in chains}
    qc = {ch: jnp.dot(qh[ch], cm[ch].astype(BF16), preferred_element_type=F32) for ch in chains}
    dmat = {ch: jnp.where(keep, b_col[ch] + w_row[ch], -jnp.inf) for ch in chains}
    dmax = {ch: jnp.max(dmat[ch], axis=-1, keepdims=True) for ch in chains}
    inter = {ch: b_col[ch] + m_old[ch] for ch in chains}
    mt = {ch: jnp.maximum(inter[ch], dmax[ch]) for ch in chains}
    s = {ch: qk[ch] * jnp.exp(dmat[ch] - mt[ch]) for ch in chains}
    sc = {ch: jnp.exp(inter[ch] - mt[ch]) for ch in chains}
    sv = {ch: jnp.dot(s[ch].astype(BF16), vh[ch], preferred_element_type=F32) for ch in chains}
    qn = {ch: jnp.sum(qf[ch] * nv[ch], axis=-1, keepdims=True) for ch in chains}
    den = {ch: jnp.sum(s[ch], axis=-1, keepdims=True) + sc[ch] * qn[ch] for ch in chains}
    hh = {ch: (sv[ch] + sc[ch] * qc[ch]) / jnp.maximum(jnp.abs(den[ch]), jnp.exp(-mt[ch])) for ch in chains}
    bl = {ch: b_col[ch][cl - 1:cl, :] for ch in chains}
    m_new = {ch: mt[ch][cl - 1:cl, :] for ch in chains}
    wk = {ch: jnp.exp(bl[ch] - b_col[ch] + i_col[ch] - m_new[ch]) * kh[ch] for ch in chains}
    decay = {ch: jnp.exp(bl[ch] + m_old[ch] - m_new[ch]) for ch in chains}
    kv = {ch: _dot_tn(wk[ch].astype(BF16), vh[ch]) for ch in chains}
    rms = {ch: lax.rsqrt(jnp.mean(hh[ch] * hh[ch], axis=-1, keepdims=True) + EPS) for ch in chains}
    for ch in chains:
        bi, h = ch
        cn_ref[bi, h] = decay[ch] * cm[ch] + kv[ch]
        nn_ref[bi, h:h + 1, :] = decay[ch] * nv[ch] + jnp.sum(wk[ch], axis=0, keepdims=True)
        mn_ref[bi, :, h:h + 1] = m_new[ch]
        y_ref[bi, :, hsl(h)] = hh[ch] * rms[ch] * ng_ref[:, hsl(h)] * jax.nn.sigmoid(o_ref[bi, :, hsl(h)])


def _mlstm(layer, zml, zgt, gbias, ng, c0, n0, m0, *, bb, cl, t_valid):
    nb, t, _ = zml.shape
    nc = t // cl
    body = functools.partial(_mlstm_body, bb=bb, cl=cl, t_valid=t_valid)
    zspec = lambda j: pl.BlockSpec((bb, cl, ML_WIDTH), lambda i, c, j=j: (i, c, j))
    return pl.pallas_call(
        body,
        grid=(nb // bb, nc),
        in_specs=[zspec(0), zspec(1), zspec(2), zspec(3),
                  pl.BlockSpec((bb, cl, LANES), lambda i, c: (i, c, 0)),
                  _full((1, LANES)), _full((1, ML_WIDTH)),
                  pl.BlockSpec((1, bb, ML_HEADS, ML_DK, ML_DV), lambda i, c: (layer, i, 0, 0, 0)),
                  pl.BlockSpec((bb, ML_HEADS, ML_DK), lambda i, c: (i, 0, 0)),
                  pl.BlockSpec((bb, 1, ML_HEADS), lambda i, c: (i, 0, 0))],
        out_specs=[pl.BlockSpec((bb, cl, ML_WIDTH), lambda i, c: (i, c, 0)),
                   pl.BlockSpec((bb, ML_HEADS, ML_DK, ML_DV), lambda i, c: (i, 0, 0, 0)),
                   pl.BlockSpec((bb, ML_HEADS, ML_DK), lambda i, c: (i, 0, 0)),
                   pl.BlockSpec((bb, 1, ML_HEADS), lambda i, c: (i, 0, 0))],
        out_shape=[jax.ShapeDtypeStruct((nb, t, ML_WIDTH), F32),
                   jax.ShapeDtypeStruct((nb, ML_HEADS, ML_DK, ML_DV), F32),
                   jax.ShapeDtypeStruct((nb, ML_HEADS, ML_DK), F32),
                   jax.ShapeDtypeStruct((nb, 1, ML_HEADS), F32)],
        compiler_params=_cparams(("arbitrary", "arbitrary")),
        name="mlstm_chunks",
    )(zml, zml, zml, zml, zgt, gbias, ng, c0, n0, m0)


def _outproj_body(x_ref, y1_ref, y2_ref, y3_ref, wo_ref, g2_ref, rw_ref, rb_ref,
                  xn_ref, hn_ref, ridx_ref, rgt_ref):
    acc = jnp.dot(y1_ref[...].astype(BF16), wo_ref[0:256, :], preferred_element_type=F32)
    acc = acc + jnp.dot(y2_ref[...].astype(BF16), wo_ref[256:512, :], preferred_element_type=F32)
    acc = acc + jnp.dot(y3_ref[...].astype(BF16), wo_ref[512:1024, :], preferred_element_type=F32)
    x = x_ref[...] + acc
    xn_ref[...] = x
    h = x * lax.rsqrt(jnp.mean(x * x, axis=-1, keepdims=True) + EPS) * g2_ref[...]
    _rows_to_tiles(h, hn_ref)
    hb = h.astype(BF16)
    tm = x.shape[0]
    eidx = lax.broadcasted_iota(jnp.int32, (N_EXPERTS, ROUTE_TILE), 0)
    sub = lax.broadcasted_iota(jnp.int32, (LANES, ROUTE_TILE), 0)
    for j in range(tm // ROUTE_TILE):
        rows = slice(j * ROUTE_TILE, (j + 1) * ROUTE_TILE)
        vals = _dot_nt(rw_ref[...], hb[rows, :]) + rb_ref[...]
        tops, idxs = [], []
        for _ in range(TOP_K):
            m = jnp.max(vals, axis=0, keepdims=True)
            idx = jnp.min(jnp.where(vals == m, eidx, N_EXPERTS), axis=0, keepdims=True)
            vals = jnp.where(eidx == idx, -jnp.inf, vals)
            tops.append(m)
            idxs.append(idx)
        exps = [jnp.exp(t - tops[0]) for t in tops]
        inv = 1.0 / functools.reduce(lambda a, b: a + b, exps)
        ridx_ref[:, rows] = jnp.concatenate(idxs + idxs, axis=0)
        gt = jnp.zeros((LANES, ROUTE_TILE), F32)
        for kk in range(TOP_K):
            gt = jnp.where(sub == kk, exps[kk] * inv, gt)
        rgt_ref[rows, :] = gt.T


def _outproj(x, y1, y2, y3, wo, g2, rw, rb):
    n = x.shape[0]
    tm = ROW_TILE
    rowspec = lambda w_: pl.BlockSpec((tm, w_), lambda i: (i, 0))
    return pl.pallas_call(
        _outproj_body,
        grid=(n // tm,),
        in_specs=[rowspec(D_MODEL), rowspec(S5_WIDTH), rowspec(RG_WIDTH), rowspec(ML_WIDTH),
                  _full((D_MODEL, D_MODEL)), _full((1, D_MODEL)),
                  _full((N_EXPERTS, D_MODEL)), _full((N_EXPERTS, 1))],
        out_specs=[rowspec(D_MODEL), pl.BlockSpec((tm * ROW_TILES, LANES), lambda i: (i, 0)),
                   pl.BlockSpec((2 * TOP_K, tm), lambda i: (0, i)), rowspec(LANES)],
        out_shape=[jax.ShapeDtypeStruct((n, D_MODEL), F32), jax.ShapeDtypeStruct((n * ROW_TILES, LANES), F32),
                   jax.ShapeDtypeStruct((2 * TOP_K, n), jnp.int32), jax.ShapeDtypeStruct((n, LANES), F32)],
        compiler_params=_cparams(("arbitrary",)),
        name="outproj_route",
    )(x, y1, y2, y3, wo, g2, rw, rb)


GATHER_AHEAD = 2
N_XBUF = GATHER_AHEAD + 1
MOE_PHASES = 4
N_OBUF = 3


def _moe_body(be_ref, bv_ref, tokn_ref, posp_ref, posc_ref, hn_ref, w1_ref, b1_ref, w2_ref, b2_ref, y4_ref,
              w1b, w2b, xbuf, obuf, xbb, actb, gsem, ssem):
    i = pl.program_id(0)
    last = pl.num_programs(0) - 1
    real = bv_ref[i] > 0
    fresh = jnp.logical_or(i == GATHER_AHEAD, be_ref[i] != be_ref[jnp.maximum(i - 1, 0)])
    xs, xn = lax.rem(i, N_XBUF), lax.rem(i + GATHER_AHEAD, N_XBUF)
    os_, op, opp = lax.rem(i, N_OBUF), lax.rem(i + N_OBUF - 1, N_OBUF), lax.rem(i + N_OBUF - 2, N_OBUF)
    x_cur, x_nxt, o_cur, o_prev = xbuf.at[xs], xbuf.at[xn], obuf.at[os_], obuf.at[op]

    def tile(ref, row):
        return ref.at[pl.ds(pl.multiple_of(row, ROW_TILES), ROW_TILES)]

    def start_gathers(rows=range(MOE_TILE)):
        for r in rows:
            pltpu.make_async_copy(tile(hn_ref, tokn_ref[0, 0, r]), tile(x_nxt, r * ROW_TILES),
                                  gsem.at[xn]).start(priority=0)

    def start_scatters(pos_ref, o_buf, sem, rows=range(MOE_TILE)):
        for r in rows:
            pltpu.make_async_copy(tile(o_buf, r * ROW_TILES), tile(y4_ref, pos_ref[0, 0, r]), sem).start(priority=1)

    def wait_scatters(o_buf, sem):
        pltpu.make_async_copy(o_buf, y4_ref.at[pl.ds(0, MOE_TILE * ROW_TILES)], sem).wait()

    gathered = jnp.logical_and(i >= GATHER_AHEAD, jnp.logical_or(
        i < 2 * GATHER_AHEAD, bv_ref[jnp.maximum(i - GATHER_AHEAD, 0)] > 0))

    @pl.when(gathered)
    def _():
        pltpu.make_async_copy(hn_ref.at[pl.ds(0, MOE_TILE * ROW_TILES)], x_cur, gsem.at[xs]).wait()

    @pl.when(i >= N_OBUF)
    def _():
        wait_scatters(o_cur, ssem.at[os_])

    @pl.when(jnp.logical_and(fresh, real))
    def _():
        w1b[...] = w1_ref[0, 0].astype(BF16)
        w2b[...] = w2_ref[0, 0].astype(BF16)

    half_ff, half_d = D_FF // 2, D_MODEL // 2
    share = MOE_TILE // MOE_PHASES

    def issue_share(ph):
        rows = range(ph * share, (ph + 1) * share)
        start_gathers(rows)
        start_scatters(posp_ref, o_prev, ssem.at[op], rows)

    bv = bv_ref[i]
    piece_real = (bv > 0, bv >= 1, bv != 0, bv == 1)

    for ph in range(2):
        @pl.when(piece_real[ph])
        def _(ph=ph):
            if ph == 0:
                xbb[...] = _tiles_to_rows(x_cur).astype(BF16)
            issue_share(ph)
            xb = xbb[...]
            cg = slice(ph * half_ff, (ph + 1) * half_ff)
            cu = slice(D_FF + ph * half_ff, D_FF + (ph + 1) * half_ff)
            g = jnp.dot(xb, w1b[:, cg], preferred_element_type=F32) + b1_ref[0, 0, :, cg]
            u = jnp.dot(xb, w1b[:, cu], preferred_element_type=F32) + b1_ref[0, 0, :, cu]
            g = jnp.minimum(g, SWIGLU_LIMIT)
            u = jnp.clip(u, -SWIGLU_LIMIT, SWIGLU_LIMIT)
            actb[:, cg] = (g * jax.nn.sigmoid(SWIGLU_ALPHA * g) * (u + 1.0)).astype(BF16)

    for ph in range(2):
        @pl.when(piece_real[2 + ph])
        def _(ph=ph):
            issue_share(2 + ph)
            co = slice(ph * half_d, (ph + 1) * half_d)
            out = jnp.dot(actb[...], w2b[:, co], preferred_element_type=F32) + b2_ref[0, 0, :, co]
            for j in range(ROW_TILES // 2):
                o_cur[pl.ds(ph * (ROW_TILES // 2) + j, MOE_TILE, stride=ROW_TILES), :] = out[:, j * LANES:(j + 1) * LANES]

    @pl.when(jnp.logical_not(real))
    def _():
        @pl.when(i == 0)
        def _():
            obuf[...] = jnp.zeros_like(obuf)

        @pl.when(i < GATHER_AHEAD)
        def _():
            start_gathers()

        @pl.when(i >= 1)
        def _():
            start_scatters(posp_ref, o_prev, ssem.at[op])

    @pl.when(i == last)
    def _():
        start_scatters(posc_ref, o_cur, ssem.at[os_])
        wait_scatters(obuf.at[opp], ssem.at[opp])
        wait_scatters(o_prev, ssem.at[op])
        wait_scatters(o_cur, ssem.at[os_])


def _moe(layer, block_e, block_v, slot_tok, slot_pos, hn, w1, b1, w2, b2):
    n_blocks = block_e.shape[0]
    tm = MOE_TILE
    n_rows = n_blocks * tm
    assert n_blocks > 2 * GATHER_AHEAD
    idx_spec = lambda f: pl.BlockSpec((1, 1, tm), f, memory_space=pltpu.SMEM)
    grid_spec = pltpu.PrefetchScalarGridSpec(
        num_scalar_prefetch=2,
        grid=(n_blocks,),
        in_specs=[idx_spec(lambda i, be, bv: (jnp.minimum(i + GATHER_AHEAD, n_blocks - 1), 0, 0)),
                  idx_spec(lambda i, be, bv: (jnp.maximum(i - 1, 0), 0, 0)),
                  idx_spec(lambda i, be, bv: (i, 0, 0)),
                  pl.BlockSpec(memory_space=pl.ANY),
                  pl.BlockSpec((1, 1, D_MODEL, 2 * D_FF), lambda i, be, bv: (layer, be[i], 0, 0)),
                  pl.BlockSpec((1, 1, 1, 2 * D_FF), lambda i, be, bv: (layer, be[i], 0, 0)),
                  pl.BlockSpec((1, 1, D_FF, D_MODEL), lambda i, be, bv: (layer, be[i], 0, 0)),
                  pl.BlockSpec((1, 1, 1, D_MODEL), lambda i, be, bv: (layer, be[i], 0, 0))],
        out_specs=pl.BlockSpec(memory_space=pl.ANY),
        scratch_shapes=[pltpu.VMEM((D_MODEL, 2 * D_FF), BF16), pltpu.VMEM((D_FF, D_MODEL), BF16)]
                       + [pltpu.VMEM((N_XBUF, tm * ROW_TILES, LANES), F32),
                          pltpu.VMEM((N_OBUF, tm * ROW_TILES, LANES), F32),
                          pltpu.VMEM((tm, D_MODEL), BF16), pltpu.VMEM((tm, D_FF), BF16)]
                       + [pltpu.SemaphoreType.DMA((N_XBUF,)), pltpu.SemaphoreType.DMA((N_OBUF,))],
    )
    return pl.pallas_call(
        _moe_body,
        grid_spec=grid_spec,
        out_shape=jax.ShapeDtypeStruct((n_rows * ROW_TILES, LANES), F32),
        compiler_params=_cparams(("arbitrary",)),
        name="moe_experts",
    )(block_e, block_v, slot_tok, slot_pos, slot_pos, hn, w1, b1, w2, b2)


def _combine_body(x_ref, rgt_ref, fg_ref, *rest, final):
    y_refs, o_ref = rest[:TOP_K], rest[TOP_K]
    x = x_ref[...]
    for kk in range(TOP_K):
        x = x + rgt_ref[:, kk:kk + 1] * _tiles_to_rows(y_refs[kk])
    if final:
        x = x * lax.rsqrt(jnp.mean(x * x, axis=-1, keepdims=True) + EPS) * fg_ref[...]
    o_ref[...] = x


def _combine(x, y4, rgt, fg, *, tok0, n_all, final):
    n = x.shape[0]
    tm = ROW_TILE
    assert tok0 % tm == 0 and n_all % tm == 0
    yspec = lambda kk: pl.BlockSpec((tm * ROW_TILES, LANES), lambda i, kk=kk: ((kk * n_all + tok0) // tm + i, 0))
    return pl.pallas_call(
        functools.partial(_combine_body, final=final),
        grid=(n // tm,),
        in_specs=[pl.BlockSpec((tm, D_MODEL), lambda i: (i, 0)),
                  pl.BlockSpec((tm, LANES), lambda i: (i, 0)), _full((1, D_MODEL))]
                 + [yspec(kk) for kk in range(TOP_K)],
        out_specs=pl.BlockSpec((tm, D_MODEL), lambda i: (i, 0)),
        out_shape=jax.ShapeDtypeStruct((n, D_MODEL), F32),
        compiler_params=_cparams(("arbitrary",)),
        name="moe_combine",
    )(x, rgt, fg, *([y4] * TOP_K))


def _routing_tables(ridx, n):
    na = n * TOP_K
    tm = MOE_TILE
    assert na % tm == 0
    lead = GATHER_AHEAD * tm
    n_blocks = na // tm + N_EXPERTS + 2 * GATHER_AHEAD
    n_slots = n_blocks * tm
    experts = jnp.arange(N_EXPERTS, dtype=jnp.int32)
    flat_e = ridx[:TOP_K, :].reshape(-1)
    id_bits = (na - 1).bit_length()
    assert N_EXPERTS << id_bits < 1 << 31
    order = jnp.sort((flat_e << id_bits) | jnp.arange(na, dtype=jnp.int32)) & ((1 << id_bits) - 1)
    counts = jnp.sum(flat_e[None, :] == experts[:, None], axis=1).astype(jnp.int32)
    padded = (counts + tm - 1) // tm * tm
    pad_end = jnp.cumsum(padded)
    pad_start = pad_end - padded
    start = jnp.cumsum(counts) - counts
    spare_start = na + pad_start - start
    total = pad_end[-1]
    blk = jnp.arange(n_blocks, dtype=jnp.int32) * tm - lead
    block_v = ((blk >= 0) & (blk < total)).astype(jnp.int32)
    last_e = jnp.max(jnp.where(counts > 0, experts, 0))
    block_e = jnp.minimum(jnp.sum(blk[:, None] >= pad_end[None, :], axis=1), last_e).astype(jnp.int32)
    rank = (blk - pad_start[block_e])[:, None] + jnp.arange(tm, dtype=jnp.int32)[None, :]
    cnt = counts[block_e][:, None]
    slot = blk[:, None] + jnp.arange(tm, dtype=jnp.int32)[None, :]
    used = (slot >= 0) & (slot < total)
    real = (rank < cnt) & used
    src = order[jnp.clip(start[block_e][:, None] + rank, 0, na - 1)]
    slot_tok = (jnp.where(real, src % n, 0) * ROW_TILES).astype(jnp.int32)
    spare = jnp.where(used, spare_start[block_e][:, None] + rank - cnt, jnp.where(slot >= 0, slot, n_slots + slot))
    slot_pos = (jnp.where(real, src, spare) * ROW_TILES).astype(jnp.int32)
    return block_e, block_v, slot_tok.reshape(n_blocks, 1, tm), slot_pos.reshape(n_blocks, 1, tm)


def _blockdiag(m):
    g, a, b = m.shape
    eye = jnp.eye(g, dtype=m.dtype)
    return (eye[:, None, :, None] * m[:, :, None, :]).reshape(g * a, g * b)


def _layer_params(l, p):
    lam = jnp.zeros((SUBLANES, S5_FLAT), F32)
    lam = lam.at[0].set(p['s5_lambda_re'][l].reshape(-1)).at[1].set(p['s5_lambda_im'][l].reshape(-1))
    lam = lam.at[2].set(jnp.repeat(p['s5_log_step'][l], S5_STATE))
    gbias = jnp.zeros((1, LANES), F32).at[0, :2 * ML_HEADS].set(p['ml_gate_bias'][l])
    w_in = jnp.pad(p['w_in'][l], ((0, 0), (0, N_IN_PAD - N_IN))).astype(BF16)
    return dict(
        norm1_g=p['norm1_g'][l].reshape(1, -1), w_in=w_in, s5_lam=lam,
        s5_bre=_blockdiag(p['s5_b_re'][l].transpose(0, 2, 1)), s5_bim=_blockdiag(p['s5_b_im'][l].transpose(0, 2, 1)),
        s5_cre=_blockdiag(p['s5_c_re'][l].transpose(0, 2, 1)).astype(BF16),
        s5_cim=_blockdiag(p['s5_c_im'][l].transpose(0, 2, 1)).astype(BF16),
        s5_d=p['s5_d'][l].reshape(1, -1), s5_gw=p['s5_glu_w'][l].astype(BF16), s5_gb=p['s5_glu_b'][l].reshape(1, -1),
        rg_cw=p['rg_conv_w'][l], rg_cb=p['rg_conv_b'][l].reshape(1, -1),
        rg_wa=_blockdiag(p['rg_wa'][l]).astype(BF16), rg_ba=p['rg_ba'][l].reshape(1, -1),
        rg_wx=_blockdiag(p['rg_wx'][l]).astype(BF16), rg_bx=p['rg_bx'][l].reshape(1, -1),
        rg_lam=p['rg_lambda'][l].reshape(1, -1),
        ml_gb=gbias, ml_ng=p['ml_norm_g'][l].reshape(1, -1),
        w_out=p['w_out'][l].astype(BF16), norm2_g=p['norm2_g'][l].reshape(1, -1),
        router_w=p['router_w'][l].T.astype(BF16), router_b=p['router_b'][l].reshape(-1, 1),
    )


def _mix_and_project(l, lp, xf, states, cfg):
    nb, t, t_valid = cfg['nb'], cfg['t'], cfg['t_valid']
    n = nb * t
    s5r, s5i, rgh, rgc, mlc, mln, mlm = states
    zs5, zrg, zml, zgt = _inproj(xf, lp['norm1_g'], lp['w_in'])
    y1, nr, ni = _s5(zs5.reshape(nb, t, -1), s5r[l].reshape(nb, -1), s5i[l].reshape(nb, -1), lp['s5_lam'],
                     lp['s5_bre'], lp['s5_bim'], lp['s5_cre'], lp['s5_cim'], lp['s5_d'], lp['s5_gw'], lp['s5_gb'],
                     tc=cfg['s5_tc'], t_valid=t_valid)
    y2, nh, ncv = _rglru(zrg.reshape(nb, t, -1), rgh[l], rgc[l], lp['rg_cw'], lp['rg_cb'], lp['rg_wa'],
                         lp['rg_ba'], lp['rg_wx'], lp['rg_bx'], lp['rg_lam'], tc=cfg['s5_tc'], t_valid=t_valid)
    y3, nc_, nn_, nm_ = _mlstm(l, zml.reshape(nb, t, -1), zgt.reshape(nb, t, -1), lp['ml_gb'], lp['ml_ng'],
                               mlc, mln[l], mlm[l].reshape(nb, 1, ML_HEADS),
                               bb=cfg['ml_bb'], cl=cfg['ml_cl'], t_valid=t_valid)
    xn, hn, ridx, rgt = _outproj(xf, y1.reshape(n, -1), y2.reshape(n, -1), y3.reshape(n, -1),
                                 lp['w_out'], lp['norm2_g'], lp['router_w'], lp['router_b'])
    new_states = (nr.reshape(nb, S5_GROUPS, S5_STATE), ni.reshape(nb, S5_GROUPS, S5_STATE), nh, ncv, nc_, nn_,
                  nm_.reshape(nb, ML_HEADS))
    return xn, hn, ridx, rgt, new_states


def _trunks(xs, states, cfgs, params, final_g):
    n_all = sum(x.shape[0] for x in xs)
    tok0 = [sum(x.shape[0] for x in xs[:s]) for s in range(len(xs))]
    outs = [[[] for _ in range(7)] for _ in xs]
    b1 = params['exp_b1'].reshape(DEPTH, N_EXPERTS, 1, -1)
    b2 = params['exp_b2'].reshape(DEPTH, N_EXPERTS, 1, -1)
    for l in range(DEPTH):
        lp = _layer_params(l, params)
        mixed = [_mix_and_project(l, lp, xs[s], states[s], cfgs[s]) for s in range(len(xs))]
        hn_all = jnp.concatenate([m[1] for m in mixed], axis=0)
        ridx_all = jnp.concatenate([m[2] for m in mixed], axis=1)
        block_e, block_v, slot_tok, slot_pos = _routing_tables(ridx_all, n_all)
        y4 = _moe(l, block_e, block_v, slot_tok, slot_pos, hn_all, params['exp_w1'], b1, params['exp_w2'], b2)
        xs = [_combine(mixed[s][0], y4, mixed[s][3], final_g.reshape(1, -1), tok0=tok0[s], n_all=n_all,
                       final=(l == DEPTH - 1)) for s in range(len(xs))]
        for s in range(len(xs)):
            for lst, val in zip(outs[s], mixed[s][4]):
                lst.append(val)
    return xs, [[jnp.stack(v) for v in o] for o in outs]


def kernel(x_prompt, x_sample, state_s5_re, state_s5_im, state_rg_h, state_rg_conv, state_ml_c, state_ml_n, state_ml_m, norm1_g, w_in, s5_lambda_re, s5_lambda_im, s5_log_step, s5_b_re, s5_b_im, s5_c_re, s5_c_im, s5_d, s5_glu_w, s5_glu_b, rg_conv_w, rg_conv_b, rg_wa, rg_ba, rg_wx, rg_bx, rg_lambda, ml_gate_bias, ml_norm_g, w_out, norm2_g, router_w, router_b, exp_w1, exp_b1, exp_w2, exp_b2, final_norm_g):
    params = dict(norm1_g=norm1_g, w_in=w_in, s5_lambda_re=s5_lambda_re, s5_lambda_im=s5_lambda_im,
                  s5_log_step=s5_log_step, s5_b_re=s5_b_re, s5_b_im=s5_b_im, s5_c_re=s5_c_re, s5_c_im=s5_c_im,
                  s5_d=s5_d, s5_glu_w=s5_glu_w, s5_glu_b=s5_glu_b, rg_conv_w=rg_conv_w, rg_conv_b=rg_conv_b,
                  rg_wa=rg_wa, rg_ba=rg_ba, rg_wx=rg_wx, rg_bx=rg_bx, rg_lambda=rg_lambda,
                  ml_gate_bias=ml_gate_bias, ml_norm_g=ml_norm_g, w_out=w_out, norm2_g=norm2_g,
                  router_w=router_w, router_b=router_b, exp_w1=exp_w1, exp_b1=exp_b1, exp_w2=exp_w2, exp_b2=exp_b2)
    bp, tp, _ = x_prompt.shape
    bs, ts, _ = x_sample.shape
    assert tp >= RG_CONV - 1 and ts >= RG_CONV - 1
    zeros = lambda *shape: jnp.zeros((DEPTH, bp) + shape, F32)
    prompt_states = (zeros(S5_GROUPS, S5_STATE), zeros(S5_GROUPS, S5_STATE), zeros(RG_WIDTH),
                     zeros(RG_CONV - 1, RG_WIDTH), zeros(ML_HEADS, ML_DK, ML_DV), zeros(ML_HEADS, ML_DK),
                     zeros(ML_HEADS))
    ts_pad = -(-ts // SUBLANES) * SUBLANES
    xs_pad = jnp.pad(x_sample, ((0, 0), (0, ts_pad - ts), (0, 0)))
    sample_states = (state_s5_re, state_s5_im, state_rg_h, state_rg_conv, state_ml_c, state_ml_n, state_ml_m)
    cfgs = [dict(nb=bp, t=tp, t_valid=tp, s5_tc=128, ml_bb=8, ml_cl=math.gcd(tp, ML_CHUNK)),
            dict(nb=bs, t=ts_pad, t_valid=ts, s5_tc=ts_pad, ml_bb=8, ml_cl=ts_pad)]
    (yp, ys), (sp, ss) = _trunks([x_prompt.reshape(bp * tp, D_MODEL), xs_pad.reshape(bs * ts_pad, D_MODEL)],
                                 [prompt_states, sample_states], cfgs, params, final_norm_g)
    return (yp.reshape(bp, tp, D_MODEL), ys.reshape(bs, ts_pad, D_MODEL)[:, :ts], *sp, *ss)
```

```python
import functools
import math

import jax
import jax.numpy as jnp
from jax import lax
from jax.experimental import pallas as pl
from jax.experimental.pallas import tpu as pltpu

F32 = jnp.float32
BF16 = jnp.bfloat16

SUBLANES = 8
LANES = 128
VMEM_LIMIT_BYTES = 56 * 1024 * 1024

D_MODEL = 1024
DEPTH = 2
S5_WIDTH = 256
S5_GROUP = 16
S5_GROUPS = 16
S5_STATE = 64
S5_FLAT = S5_GROUPS * S5_STATE
RG_WIDTH = 256
RG_BLOCKS = 8
RG_CONV = 4
RG_C = 8.0
ML_WIDTH = 512
ML_HEADS = 4
ML_DK = 128
ML_DV = 128
ML_CHUNK = 64
N_EXPERTS = 32
TOP_K = 4
D_FF = 1024
SWIGLU_LIMIT = 7.0
SWIGLU_ALPHA = 1.702
EPS = 1e-5

IN_S5 = (0, 256)
IN_RG = (256, 768)
IN_ML = (768, 2816)
IN_GATE = (2816, 2944)
N_IN = 2824
N_IN_PAD = 2944

ROW_TILE = 512
ROUTE_TILE = 128
MOE_TILE = 256
NEG_BIG = -1e30


def _cparams(sem):
    return pltpu.CompilerParams(dimension_semantics=sem, vmem_limit_bytes=VMEM_LIMIT_BYTES)


def _full(shape):
    n = len(shape)
    return pl.BlockSpec(shape, lambda *_: (0,) * n)


ROW_TILES = D_MODEL // LANES
assert ROW_TILES == SUBLANES


def _rows_to_tiles(x, dst_ref):
    m = x.shape[0]
    for j in range(ROW_TILES):
        dst_ref[pl.ds(j, m, stride=ROW_TILES), :] = x[:, j * LANES:(j + 1) * LANES]


def _tiles_to_rows(src_ref):
    m = src_ref.shape[0] // ROW_TILES
    return jnp.concatenate([src_ref[pl.ds(j, m, stride=ROW_TILES), :] for j in range(ROW_TILES)], axis=-1)


def _inproj_body(x_ref, g_ref, w_ref, zs5_ref, zrg_ref, zml_ref, zgt_ref):
    x = x_ref[...]
    h = x * lax.rsqrt(jnp.mean(x * x, axis=-1, keepdims=True) + EPS) * g_ref[...]
    hb = h.astype(BF16)
    for ref, (lo, hi) in ((zs5_ref, IN_S5), (zrg_ref, IN_RG), (zml_ref, IN_ML), (zgt_ref, IN_GATE)):
        ref[...] = jnp.dot(hb, w_ref[:, lo:hi], preferred_element_type=F32)


def _inproj(x, g, w):
    n = x.shape[0]
    tm = ROW_TILE
    widths = [hi - lo for lo, hi in (IN_S5, IN_RG, IN_ML, IN_GATE)]
    return pl.pallas_call(
        _inproj_body,
        grid=(n // tm,),
        in_specs=[pl.BlockSpec((tm, D_MODEL), lambda i: (i, 0)), _full((1, D_MODEL)), _full((D_MODEL, N_IN_PAD))],
        out_specs=[pl.BlockSpec((tm, w_), lambda i: (i, 0)) for w_ in widths],
        out_shape=[jax.ShapeDtypeStruct((n, w_), F32) for w_ in widths],
        compiler_params=_cparams(("arbitrary",)),
        name="inproj",
    )(x, g, w)


def _s5_body(u_ref, s0r_ref, s0i_ref, lam_ref, bre_ref, bim_ref, cre_ref, cim_ref, d_ref, gw_ref, gb_ref,
             y_ref, sr_ref, si_ref, bur, bui, ab, bbr, bbi, tm_s, bm_s, *, nb, tc, nc, t_valid):
    c = pl.program_id(0)

    @pl.when(c == 0)
    def _():
        sr_ref[...] = s0r_ref[...]
        si_ref[...] = s0i_ref[...]
        lr = lam_ref[0:1, :]
        li = lam_ref[1:2, :]
        step = jnp.exp(lam_ref[2:3, :])
        mag = jnp.exp(lr * step)
        ab_re = mag * jnp.cos(li * step)
        ab_im = mag * jnp.sin(li * step)
        den = lr * lr + li * li
        num_re = ab_re - 1.0
        z_re = (num_re * lr + ab_im * li) / den
        z_im = (ab_im * lr - num_re * li) / den
        ab[0:1, :] = ab_re
        ab[1:2, :] = ab_im
        bbr[...] = (z_re * bre_ref[...] - z_im * bim_ref[...]).astype(BF16)
        bbi[...] = (z_re * bim_ref[...] + z_im * bre_ref[...]).astype(BF16)

    _to_time_major(u_ref, tm_s, bm_s, nb, tc)
    u = jnp.concatenate([tm_s[j] for j in range(S5_WIDTH // LANES)], axis=-1)
    ub = u.astype(BF16)
    bur[...] = jnp.dot(ub, bbr[...], preferred_element_type=F32)
    bui[...] = jnp.dot(ub, bbi[...], preferred_element_type=F32)

    a_re = jnp.broadcast_to(ab[0:1, :], (SUBLANES, S5_FLAT))
    a_im = jnp.broadcast_to(ab[1:2, :], (SUBLANES, S5_FLAT))
    last_in = t_valid - (nc - 1) * tc
    steps = tc if last_in == tc else jnp.where(c == nc - 1, last_in, tc)

    def group(gi, _):
        g8 = gi * SUBLANES
        rows8 = pl.ds(pl.multiple_of(g8, SUBLANES), SUBLANES)

        def step_fn(t, carry):
            s_re, s_im = carry
            rows = pl.ds(pl.multiple_of(t * nb + g8, SUBLANES), SUBLANES)
            n_re = a_re * s_re - a_im * s_im + bur[rows, :]
            n_im = a_re * s_im + a_im * s_re + bui[rows, :]
            bur[rows, :] = n_re
            bui[rows, :] = n_im
            return n_re, n_im

        s_re, s_im = lax.fori_loop(0, steps, step_fn, (sr_ref[rows8, :], si_ref[rows8, :]))
        sr_ref[rows8, :] = s_re
        si_ref[rows8, :] = s_im
        return 0

    lax.fori_loop(0, nb // SUBLANES, group, 0)

    y = (jnp.dot(bur[...].astype(BF16), cre_ref[...], preferred_element_type=F32)
         - jnp.dot(bui[...].astype(BF16), cim_ref[...], preferred_element_type=F32))
    y = y + d_ref[...] * u
    g = jax.nn.gelu(y)
    gate = jax.nn.sigmoid(jnp.dot(g.astype(BF16), gw_ref[...], preferred_element_type=F32) + gb_ref[...])
    out = g * gate
    for j in range(S5_WIDTH // LANES):
        tm_s[j] = out[:, j * LANES:(j + 1) * LANES]
    _to_batch_major(tm_s, bm_s, y_ref, nb, tc)


def _to_time_major(src_ref, tm_s, bm_s, nb, tc):
    for j in range(tm_s.shape[0]):
        lanes = slice(j * LANES, (j + 1) * LANES)
        if nb <= tc:
            for b in range(nb):
                tm_s[j, pl.ds(b, tc, stride=nb), :] = src_ref[b, :, lanes]
        else:
            bm_s[j] = src_ref[:, :, lanes].reshape(nb * tc, LANES)
            for t in range(tc):
                tm_s[j, t * nb:(t + 1) * nb, :] = bm_s[j, pl.ds(t, nb, stride=tc), :]


def _to_batch_major(tm_s, bm_s, dst_ref, nb, tc):
    for j in range(tm_s.shape[0]):
        lanes = slice(j * LANES, (j + 1) * LANES)
        if nb <= tc:
            for b in range(nb):
                dst_ref[b, :, lanes] = tm_s[j, pl.ds(b, tc, stride=nb), :]
        else:
            for t in range(tc):
                bm_s[j, pl.ds(t, nb, stride=tc), :] = tm_s[j, t * nb:(t + 1) * nb, :]
            dst_ref[:, :, lanes] = bm_s[j].reshape(nb, tc, LANES)


def _s5(u, s0r, s0i, lam, bre, bim, cre, cim, d, gw, gb, *, tc, t_valid):
    nb, t, _ = u.shape
    nc = t // tc
    assert (nc - 1) * tc < t_valid <= t
    body = functools.partial(_s5_body, nb=nb, tc=tc, nc=nc, t_valid=t_valid)
    return pl.pallas_call(
        body,
        grid=(t // tc,),
        in_specs=[pl.BlockSpec((nb, tc, S5_WIDTH), lambda c: (0, c, 0)),
                  _full((nb, S5_FLAT)), _full((nb, S5_FLAT)), _full((SUBLANES, S5_FLAT)),
                  _full((S5_WIDTH, S5_FLAT)), _full((S5_WIDTH, S5_FLAT)),
                  _full((S5_FLAT, S5_WIDTH)), _full((S5_FLAT, S5_WIDTH)),
                  _full((1, S5_WIDTH)), _full((S5_WIDTH, S5_WIDTH)), _full((1, S5_WIDTH))],
        out_specs=[pl.BlockSpec((nb, tc, S5_WIDTH), lambda c: (0, c, 0)),
                   _full((nb, S5_FLAT)), _full((nb, S5_FLAT))],
        out_shape=[jax.ShapeDtypeStruct((nb, t, S5_WIDTH), F32),
                   jax.ShapeDtypeStruct((nb, S5_FLAT), F32), jax.ShapeDtypeStruct((nb, S5_FLAT), F32)],
        scratch_shapes=[pltpu.VMEM((nb * tc, S5_FLAT), F32), pltpu.VMEM((nb * tc, S5_FLAT), F32),
                        pltpu.VMEM((SUBLANES, S5_FLAT), F32),
                        pltpu.VMEM((S5_WIDTH, S5_FLAT), BF16), pltpu.VMEM((S5_WIDTH, S5_FLAT), BF16),
                        pltpu.VMEM((S5_WIDTH // LANES, nb * tc, LANES), F32),
                        pltpu.VMEM((S5_WIDTH // LANES, nb * tc, LANES), F32)],
        compiler_params=_cparams(("arbitrary",)),
        name="s5_scan",
    )(u, s0r, s0i, lam, bre, bim, cre, cim, d, gw, gb)


def _rglru_body(z_ref, h0_ref, cb0_ref, cw_ref, cb_ref, wa_ref, ba_ref, wx_ref, bx_ref, lam_ref,
                y_ref, h_ref, cbn_ref, xs, a_s, b_s, tm_s, bm_s, *, nb, tc, nc, t_valid):
    c = pl.program_id(0)
    hist = RG_CONV - 1

    @pl.when(c == 0)
    def _():
        h_ref[...] = h0_ref[...]
        xs[0:hist * nb, :] = cb0_ref[...]

    _to_time_major(z_ref, tm_s, bm_s, nb, tc)
    xs[hist * nb:(hist + tc) * nb, :] = jnp.concatenate([tm_s[j] for j in range(RG_WIDTH // LANES)], axis=-1)

    xc = jnp.zeros((tc * nb, RG_WIDTH), F32) + cb_ref[...]
    for j in range(RG_CONV):
        xc = xc + xs[j * nb:(j + tc) * nb, :] * cw_ref[j:j + 1, :]

    last_in = t_valid - (nc - 1) * tc
    last = tc if last_in == tc else jnp.where(c == nc - 1, last_in, tc)

    @pl.when(c == nc - 1)
    def _():
        cbn_ref[...] = xs[last_in * nb:(last_in + hist) * nb, :]

    xs[0:hist * nb, :] = xs[tc * nb:(tc + hist) * nb, :]

    xb = xc.astype(BF16)
    r = jax.nn.sigmoid(jnp.dot(xb, wa_ref[...], preferred_element_type=F32) + ba_ref[...])
    i = jax.nn.sigmoid(jnp.dot(xb, wx_ref[...], preferred_element_type=F32) + bx_ref[...])
    log_a = -RG_C * r * jax.nn.softplus(-lam_ref[...])
    a_s[...] = jnp.exp(log_a)
    th = jnp.tanh(log_a)
    b_s[...] = jnp.sqrt(-2.0 * th / (1.0 - th)) * (i * xc)

    def group(gi, _):
        g8 = gi * SUBLANES
        rows8 = pl.ds(pl.multiple_of(g8, SUBLANES), SUBLANES)

        def step_fn(t, h):
            rows = pl.ds(pl.multiple_of(t * nb + g8, SUBLANES), SUBLANES)
            h = a_s[rows, :] * h + b_s[rows, :]
            b_s[rows, :] = h
            return h

        h_ref[rows8, :] = lax.fori_loop(0, last, step_fn, h_ref[rows8, :])
        return 0

    lax.fori_loop(0, nb // SUBLANES, group, 0)
    for j in range(RG_WIDTH // LANES):
        tm_s[j] = b_s[:, j * LANES:(j + 1) * LANES]
    _to_batch_major(tm_s, bm_s, y_ref, nb, tc)
    y_ref[...] = y_ref[...] * jax.nn.gelu(z_ref[:, :, RG_WIDTH:2 * RG_WIDTH])


def _rglru(z, h0, cb0, cw, cb, wa, ba, wx, bx, lam, *, tc, t_valid):
    nb, t, _ = z.shape
    hist = RG_CONV - 1
    nc = t // tc
    assert (nc - 1) * tc + hist <= t_valid <= t
    body = functools.partial(_rglru_body, nb=nb, tc=tc, nc=nc, t_valid=t_valid)
    y, h, cbn = pl.pallas_call(
        body,
        grid=(t // tc,),
        in_specs=[pl.BlockSpec((nb, tc, 2 * RG_WIDTH), lambda c: (0, c, 0)),
                  _full((nb, RG_WIDTH)), _full((hist * nb, RG_WIDTH)),
                  _full((RG_CONV, RG_WIDTH)), _full((1, RG_WIDTH)),
                  _full((RG_WIDTH, RG_WIDTH)), _full((1, RG_WIDTH)),
                  _full((RG_WIDTH, RG_WIDTH)), _full((1, RG_WIDTH)), _full((1, RG_WIDTH))],
        out_specs=[pl.BlockSpec((nb, tc, RG_WIDTH), lambda c: (0, c, 0)),
                   _full((nb, RG_WIDTH)), _full((hist * nb, RG_WIDTH))],
        out_shape=[jax.ShapeDtypeStruct((nb, t, RG_WIDTH), F32),
                   jax.ShapeDtypeStruct((nb, RG_WIDTH), F32), jax.ShapeDtypeStruct((hist * nb, RG_WIDTH), F32)],
        scratch_shapes=[pltpu.VMEM(((tc + hist) * nb, RG_WIDTH), F32),
                        pltpu.VMEM((nb * tc, RG_WIDTH), F32), pltpu.VMEM((nb * tc, RG_WIDTH), F32),
                        pltpu.VMEM((RG_WIDTH // LANES, nb * tc, LANES), F32),
                        pltpu.VMEM((RG_WIDTH // LANES, nb * tc, LANES), F32)],
        compiler_params=_cparams(("arbitrary",)),
        name="rglru_scan",
    )(z, h0, cb0.transpose(1, 0, 2).reshape(hist * nb, RG_WIDTH), cw, cb, wa, ba, wx, bx, lam)
    return y, h, cbn.reshape(hist, nb, RG_WIDTH).transpose(1, 0, 2)


def _dot_nt(a, b, **kw):
    return lax.dot_general(a, b, (((1,), (1,)), ((), ())), preferred_element_type=F32, **kw)


def _dot_tn(a, b, **kw):
    return lax.dot_general(a, b, (((0,), (0,)), ((), ())), preferred_element_type=F32, **kw)


def _mlstm_body(q_ref, k_ref, v_ref, o_ref, g_ref, gb_ref, ng_ref, c0_ref, n0_ref, m0_ref,
                y_ref, cn_ref, nn_ref, mn_ref, *, bb, cl, t_valid):
    c = pl.program_id(1)

    @pl.when(c == 0)
    def _():
        cn_ref[...] = c0_ref[0]
        nn_ref[...] = n0_ref[...]
        mn_ref[...] = m0_ref[...]

    row = lax.broadcasted_iota(jnp.int32, (cl, cl), 0)
    col = lax.broadcasted_iota(jnp.int32, (cl, cl), 1)
    tril = (col <= row).astype(F32)
    keep = (col <= row) & (col + c * cl < t_valid)
    t_ok = (lax.broadcasted_iota(jnp.int32, (cl, LANES), 0) + c * cl) < t_valid
    lane = lax.broadcasted_iota(jnp.int32, (cl, LANES), 1)
    ones = jnp.ones((cl, LANES), F32)

    chains = [(bi, h) for bi in range(bb) for h in range(ML_HEADS)]
    hsl = lambda h: slice(h * ML_DK, (h + 1) * ML_DK)
    hi = lax.Precision.HIGHEST

    bcum, ig = [], []
    for bi in range(bb):
        graw = g_ref[bi] + gb_ref[...]
        lf = jnp.where(t_ok, jax.nn.log_sigmoid(graw), 0.0)
        bcum.append(jnp.dot(tril, lf, preferred_element_type=F32, precision=hi))
        ig.append(jnp.where(t_ok, graw, NEG_BIG))

    qf = {ch: q_ref[ch[0], :, hsl(ch[1])] for ch in chains}
    qh = {ch: qf[ch].astype(BF16) for ch in chains}
    kh = {ch: k_ref[ch[0], :, hsl(ch[1])] * (ML_DK ** -0.5) for ch in chains}
    vh = {ch: v_ref[ch[0], :, hsl(ch[1])].astype(BF16) for ch in chains}
    cm = {ch: cn_ref[ch[0], ch[1]] for ch in chains}
    nv = {ch: nn_ref[ch[0], ch[1]:ch[1] + 1, :] for ch in chains}
    m_old = {ch: mn_ref[ch[0], :, ch[1]:ch[1] + 1] for ch in chains}
    b_col = {ch: bcum[ch[0]][:, ML_HEADS + ch[1]:ML_HEADS + ch[1] + 1] for ch in chains}
    i_col = {ch: ig[ch[0]][:, ch[1]:ch[1] + 1] for ch in chains}

    w_row = {ch: _dot_nt(ones, jnp.where(lane == ch[1], i_col[ch] - b_col[ch], 0.0), precision=hi) for ch in chains}
    qk = {ch: _dot_nt(qh[ch], kh[ch].astype(BF16)) for ch in chains}
    qc = {ch: jnp.dot(qh[ch], cm[ch].astype(BF16), preferred_element_type=F32) for ch in chains}
    dmat = {ch: jnp.where(keep, b_col[ch] + w_row[ch], -jnp.inf) for ch in chains}
    dmax = {ch: jnp.max(dmat[ch], axis=-1, keepdims=True) for ch in chains}
    inter = {ch: b_col[ch] + m_old[ch] for ch in chains}
    mt = {ch: jnp.maximum(inter[ch], dmax[ch]) for ch in chains}
    s = {ch: qk[ch] * jnp.exp(dmat[ch] - mt[ch]) for ch in chains}
    sc = {ch: jnp.exp(inter[ch] - mt[ch]) for ch in chains}
    sv = {ch: jnp.dot(s[ch].astype(BF16), vh[ch], preferred_element_type=F32) for ch in chains}
    qn = {ch: jnp.sum(qf[ch] * nv[ch], axis=-1, keepdims=True) for ch in chains}
    den = {ch: jnp.sum(s[ch], axis=-1, keepdims=True) + sc[ch] * qn[ch] for ch in chains}
    hh = {ch: (sv[ch] + sc[ch] * qc[ch]) / jnp.maximum(jnp.abs(den[ch]), jnp.exp(-mt[ch])) for ch in chains}
    bl = {ch: b_col[ch][cl - 1:cl, :] for ch in chains}
    m_new = {ch: mt[ch][cl - 1:cl, :] for ch in chains}
    wk = {ch: jnp.exp(bl[ch] - b_col[ch] + i_col[ch] - m_new[ch]) * kh[ch] for ch in chains}
    decay = {ch: jnp.exp(bl[ch] + m_old[ch] - m_new[ch]) for ch in chains}
    kv = {ch: _dot_tn(wk[ch].astype(BF16), vh[ch]) for ch in chains}
    rms = {ch: lax.rsqrt(jnp.mean(hh[ch] * hh[ch], axis=-1, keepdims=True) + EPS) for ch in chains}
    for ch in chains:
        bi, h = ch
        cn_ref[bi, h] = decay[ch] * cm[ch] + kv[ch]
        nn_ref[bi, h:h + 1, :] = decay[ch] * nv[ch] + jnp.sum(wk[ch], axis=0, keepdims=True)
        mn_ref[bi, :, h:h + 1] = m_new[ch]
        y_ref[bi, :, hsl(h)] = hh[ch] * rms[ch] * ng_ref[:, hsl(h)] * jax.nn.sigmoid(o_ref[bi, :, hsl(h)])


def _mlstm(layer, zml, zgt, gbias, ng, c0, n0, m0, *, bb, cl, t_valid):
    nb, t, _ = zml.shape
    nc = t // cl
    body = functools.partial(_mlstm_body, bb=bb, cl=cl, t_valid=t_valid)
    zspec = lambda j: pl.BlockSpec((bb, cl, ML_WIDTH), lambda i, c, j=j: (i, c, j))
    return pl.pallas_call(
        body,
        grid=(nb // bb, nc),
        in_specs=[zspec(0), zspec(1), zspec(2), zspec(3),
                  pl.BlockSpec((bb, cl, LANES), lambda i, c: (i, c, 0)),
                  _full((1, LANES)), _full((1, ML_WIDTH)),
                  pl.BlockSpec((1, bb, ML_HEADS, ML_DK, ML_DV), lambda i, c: (layer, i, 0, 0, 0)),
                  pl.BlockSpec((bb, ML_HEADS, ML_DK), lambda i, c: (i, 0, 0)),
                  pl.BlockSpec((bb, 1, ML_HEADS), lambda i, c: (i, 0, 0))],
        out_specs=[pl.BlockSpec((bb, cl, ML_WIDTH), lambda i, c: (i, c, 0)),
                   pl.BlockSpec((bb, ML_HEADS, ML_DK, ML_DV), lambda i, c: (i, 0, 0, 0)),
                   pl.BlockSpec((bb, ML_HEADS, ML_DK), lambda i, c: (i, 0, 0)),
                   pl.BlockSpec((bb, 1, ML_HEADS), lambda i, c: (i, 0, 0))],
        out_shape=[jax.ShapeDtypeStruct((nb, t, ML_WIDTH), F32),
                   jax.ShapeDtypeStruct((nb, ML_HEADS, ML_DK, ML_DV), F32),
                   jax.ShapeDtypeStruct((nb, ML_HEADS, ML_DK), F32),
                   jax.ShapeDtypeStruct((nb, 1, ML_HEADS), F32)],
        compiler_params=_cparams(("arbitrary", "arbitrary")),
        name="mlstm_chunks",
    )(zml, zml, zml, zml, zgt, gbias, ng, c0, n0, m0)


def _outproj_body(x_ref, y1_ref, y2_ref, y3_ref, wo_ref, g2_ref, rw_ref, rb_ref,
                  xn_ref, hn_ref, ridx_ref, rgt_ref):
    acc = jnp.dot(y1_ref[...].astype(BF16), wo_ref[0:256, :], preferred_element_type=F32)
    acc = acc + jnp.dot(y2_ref[...].astype(BF16), wo_ref[256:512, :], preferred_element_type=F32)
    acc = acc + jnp.dot(y3_ref[...].astype(BF16), wo_ref[512:1024, :], preferred_element_type=F32)
    x = x_ref[...] + acc
    xn_ref[...] = x
    h = x * lax.rsqrt(jnp.mean(x * x, axis=-1, keepdims=True) + EPS) * g2_ref[...]
    _rows_to_tiles(h, hn_ref)
    hb = h.astype(BF16)
    tm = x.shape[0]
    eidx = lax.broadcasted_iota(jnp.int32, (N_EXPERTS, ROUTE_TILE), 0)
    sub = lax.broadcasted_iota(jnp.int32, (LANES, ROUTE_TILE), 0)
    for j in range(tm // ROUTE_TILE):
        rows = slice(j * ROUTE_TILE, (j + 1) * ROUTE_TILE)
        vals = _dot_nt(rw_ref[...], hb[rows, :]) + rb_ref[...]
        tops, idxs = [], []
        for _ in range(TOP_K):
            m = jnp.max(vals, axis=0, keepdims=True)
            idx = jnp.min(jnp.where(vals == m, eidx, N_EXPERTS), axis=0, keepdims=True)
            vals = jnp.where(eidx == idx, -jnp.inf, vals)
            tops.append(m)
            idxs.append(idx)
        exps = [jnp.exp(t - tops[0]) for t in tops]
        inv = 1.0 / functools.reduce(lambda a, b: a + b, exps)
        ridx_ref[:, rows] = jnp.concatenate(idxs + idxs, axis=0)
        gt = jnp.zeros((LANES, ROUTE_TILE), F32)
        for kk in range(TOP_K):
            gt = jnp.where(sub == kk, exps[kk] * inv, gt)
        rgt_ref[rows, :] = gt.T


def _outproj(x, y1, y2, y3, wo, g2, rw, rb):
    n = x.shape[0]
    tm = ROW_TILE
    rowspec = lambda w_: pl.BlockSpec((tm, w_), lambda i: (i, 0))
    return pl.pallas_call(
        _outproj_body,
        grid=(n // tm,),
        in_specs=[rowspec(D_MODEL), rowspec(S5_WIDTH), rowspec(RG_WIDTH), rowspec(ML_WIDTH),
                  _full((D_MODEL, D_MODEL)), _full((1, D_MODEL)),
                  _full((N_EXPERTS, D_MODEL)), _full((N_EXPERTS, 1))],
        out_specs=[rowspec(D_MODEL), pl.BlockSpec((tm * ROW_TILES, LANES), lambda i: (i, 0)),
                   pl.BlockSpec((2 * TOP_K, tm), lambda i: (0, i)), rowspec(LANES)],
        out_shape=[jax.ShapeDtypeStruct((n, D_MODEL), F32), jax.ShapeDtypeStruct((n * ROW_TILES, LANES), F32),
                   jax.ShapeDtypeStruct((2 * TOP_K, n), jnp.int32), jax.ShapeDtypeStruct((n, LANES), F32)],
        compiler_params=_cparams(("arbitrary",)),
        name="outproj_route",
    )(x, y1, y2, y3, wo, g2, rw, rb)


GATHER_AHEAD = 2
N_XBUF = GATHER_AHEAD + 1
N_OBUF = 3


def _moe_body(be_ref, bv_ref, tokn_ref, posp_ref, posc_ref, hn_ref, w1_ref, b1_ref, w2_ref, b2_ref, y4_ref,
              w1b, w2b, xbuf, obuf, gsem, ssem):
    i = pl.program_id(0)
    last = pl.num_programs(0) - 1
    real = bv_ref[i] > 0
    fresh = jnp.logical_or(i == GATHER_AHEAD, be_ref[i] != be_ref[jnp.maximum(i - 1, 0)])
    xs, xn = lax.rem(i, N_XBUF), lax.rem(i + GATHER_AHEAD, N_XBUF)
    os_, op, opp = lax.rem(i, N_OBUF), lax.rem(i + N_OBUF - 1, N_OBUF), lax.rem(i + N_OBUF - 2, N_OBUF)
    x_cur, x_nxt, o_cur, o_prev = xbuf.at[xs], xbuf.at[xn], obuf.at[os_], obuf.at[op]

    def tile(ref, row):
        return ref.at[pl.ds(pl.multiple_of(row, ROW_TILES), ROW_TILES)]

    def start_gathers(rows=range(MOE_TILE)):
        for r in rows:
            pltpu.make_async_copy(tile(hn_ref, tokn_ref[0, 0, r]), tile(x_nxt, r * ROW_TILES),
                                  gsem.at[xn]).start(priority=0)

    def start_scatters(pos_ref, o_buf, sem, rows=range(MOE_TILE)):
        for r in rows:
            pltpu.make_async_copy(tile(o_buf, r * ROW_TILES), tile(y4_ref, pos_ref[0, 0, r]), sem).start(priority=1)

    def wait_scatters(o_buf, sem):
        pltpu.make_async_copy(o_buf, y4_ref.at[pl.ds(0, MOE_TILE * ROW_TILES)], sem).wait()

    gathered = jnp.logical_and(i >= GATHER_AHEAD, jnp.logical_or(
        i < 2 * GATHER_AHEAD, bv_ref[jnp.maximum(i - GATHER_AHEAD, 0)] > 0))

    @pl.when(gathered)
    def _():
        pltpu.make_async_copy(hn_ref.at[pl.ds(0, MOE_TILE * ROW_TILES)], x_cur, gsem.at[xs]).wait()

    @pl.when(i >= N_OBUF)
    def _():
        wait_scatters(o_cur, ssem.at[os_])

    @pl.when(jnp.logical_and(fresh, real))
    def _():
        w1b[...] = w1_ref[0, 0].astype(BF16)
        w2b[...] = w2_ref[0, 0].astype(BF16)

    @pl.when(bv_ref[i] >= 1)
    def _():
        start_gathers()

    @pl.when(real)
    def _():
        xb = _tiles_to_rows(x_cur).astype(BF16)
        start_scatters(posp_ref, o_prev, ssem.at[op])
        hb = jnp.dot(xb, w1b[...], preferred_element_type=F32) + b1_ref[0, 0]
        g = jnp.minimum(hb[:, :D_FF], SWIGLU_LIMIT)
        u = jnp.clip(hb[:, D_FF:], -SWIGLU_LIMIT, SWIGLU_LIMIT)
        act = g * jax.nn.sigmoid(SWIGLU_ALPHA * g) * (u + 1.0)
        _rows_to_tiles(jnp.dot(act.astype(BF16), w2b[...], preferred_element_type=F32) + b2_ref[0, 0], o_cur)

    @pl.when(jnp.logical_not(real))
    def _():
        @pl.when(i == 0)
        def _():
            obuf[...] = jnp.zeros_like(obuf)

        @pl.when(i < GATHER_AHEAD)
        def _():
            start_gathers()

        @pl.when(i >= 1)
        def _():
            start_scatters(posp_ref, o_prev, ssem.at[op])

    @pl.when(i == last)
    def _():
        start_scatters(posc_ref, o_cur, ssem.at[os_])
        wait_scatters(obuf.at[opp], ssem.at[opp])
        wait_scatters(o_prev, ssem.at[op])
        wait_scatters(o_cur, ssem.at[os_])


def _moe(layer, block_e, block_v, slot_tok, slot_pos, hn, w1, b1, w2, b2):
    n_blocks = block_e.shape[0]
    tm = MOE_TILE
    n_rows = n_blocks * tm
    assert n_blocks > 2 * GATHER_AHEAD
    idx_spec = lambda f: pl.BlockSpec((1, 1, tm), f, memory_space=pltpu.SMEM)
    grid_spec = pltpu.PrefetchScalarGridSpec(
        num_scalar_prefetch=2,
        grid=(n_blocks,),
        in_specs=[idx_spec(lambda i, be, bv: (jnp.minimum(i + GATHER_AHEAD, n_blocks - 1), 0, 0)),
                  idx_spec(lambda i, be, bv: (jnp.maximum(i - 1, 0), 0, 0)),
                  idx_spec(lambda i, be, bv: (i, 0, 0)),
                  pl.BlockSpec(memory_space=pl.ANY),
                  pl.BlockSpec((1, 1, D_MODEL, 2 * D_FF), lambda i, be, bv: (layer, be[i], 0, 0)),
                  pl.BlockSpec((1, 1, 1, 2 * D_FF), lambda i, be, bv: (layer, be[i], 0, 0)),
                  pl.BlockSpec((1, 1, D_FF, D_MODEL), lambda i, be, bv: (layer, be[i], 0, 0)),
                  pl.BlockSpec((1, 1, 1, D_MODEL), lambda i, be, bv: (layer, be[i], 0, 0))],
        out_specs=pl.BlockSpec(memory_space=pl.ANY),
        scratch_shapes=[pltpu.VMEM((D_MODEL, 2 * D_FF), BF16), pltpu.VMEM((D_FF, D_MODEL), BF16)]
                       + [pltpu.VMEM((N_XBUF, tm * ROW_TILES, LANES), F32),
                          pltpu.VMEM((N_OBUF, tm * ROW_TILES, LANES), F32)]
                       + [pltpu.SemaphoreType.DMA((N_XBUF,)), pltpu.SemaphoreType.DMA((N_OBUF,))],
    )
    return pl.pallas_call(
        _moe_body,
        grid_spec=grid_spec,
        out_shape=jax.ShapeDtypeStruct((n_rows * ROW_TILES, LANES), F32),
        compiler_params=_cparams(("arbitrary",)),
        name="moe_experts",
    )(block_e, block_v, slot_tok, slot_pos, slot_pos, hn, w1, b1, w2, b2)


def _combine_body(x_ref, rgt_ref, fg_ref, *rest, final):
    y_refs, o_ref = rest[:TOP_K], rest[TOP_K]
    x = x_ref[...]
    for kk in range(TOP_K):
        x = x + rgt_ref[:, kk:kk + 1] * _tiles_to_rows(y_refs[kk])
    if final:
        x = x * lax.rsqrt(jnp.mean(x * x, axis=-1, keepdims=True) + EPS) * fg_ref[...]
    o_ref[...] = x


def _combine(x, y4, rgt, fg, *, tok0, n_all, final):
    n = x.shape[0]
    tm = ROW_TILE
    assert tok0 % tm == 0 and n_all % tm == 0
    yspec = lambda kk: pl.BlockSpec((tm * ROW_TILES, LANES), lambda i, kk=kk: ((kk * n_all + tok0) // tm + i, 0))
    return pl.pallas_call(
        functools.partial(_combine_body, final=final),
        grid=(n // tm,),
        in_specs=[pl.BlockSpec((tm, D_MODEL), lambda i: (i, 0)),
                  pl.BlockSpec((tm, LANES), lambda i: (i, 0)), _full((1, D_MODEL))]
                 + [yspec(kk) for kk in range(TOP_K)],
        out_specs=pl.BlockSpec((tm, D_MODEL), lambda i: (i, 0)),
        out_shape=jax.ShapeDtypeStruct((n, D_MODEL), F32),
        compiler_params=_cparams(("arbitrary",)),
        name="moe_combine",
    )(x, rgt, fg, *([y4] * TOP_K))


def _routing_tables(ridx, n):
    na = n * TOP_K
    tm = MOE_TILE
    assert na % tm == 0
    lead = GATHER_AHEAD * tm
    n_blocks = na // tm + N_EXPERTS + 2 * GATHER_AHEAD
    n_slots = n_blocks * tm
    experts = jnp.arange(N_EXPERTS, dtype=jnp.int32)
    flat_e = ridx[:TOP_K, :].reshape(-1)
    order = jnp.argsort(flat_e).astype(jnp.int32)
    counts = jnp.sum(flat_e[None, :] == experts[:, None], axis=1).astype(jnp.int32)
    padded = (counts + tm - 1) // tm * tm
    pad_end = jnp.cumsum(padded)
    pad_start = pad_end - padded
    start = jnp.cumsum(counts) - counts
    spare_start = na + pad_start - start
    total = pad_end[-1]
    blk = jnp.arange(n_blocks, dtype=jnp.int32) * tm - lead
    block_v = ((blk >= 0) & (blk < total)).astype(jnp.int32)
    last_e = jnp.max(jnp.where(counts > 0, experts, 0))
    block_e = jnp.minimum(jnp.sum(blk[:, None] >= pad_end[None, :], axis=1), last_e).astype(jnp.int32)
    rank = (blk - pad_start[block_e])[:, None] + jnp.arange(tm, dtype=jnp.int32)[None, :]
    cnt = counts[block_e][:, None]
    slot = blk[:, None] + jnp.arange(tm, dtype=jnp.int32)[None, :]
    used = (slot >= 0) & (slot < total)
    real = (rank < cnt) & used
    src = order[jnp.clip(start[block_e][:, None] + rank, 0, na - 1)]
    slot_tok = (jnp.where(real, src % n, 0) * ROW_TILES).astype(jnp.int32)
    spare = jnp.where(used, spare_start[block_e][:, None] + rank - cnt, jnp.where(slot >= 0, slot, n_slots + slot))
    slot_pos = (jnp.where(real, src, spare) * ROW_TILES).astype(jnp.int32)
    return block_e, block_v, slot_tok.reshape(n_blocks, 1, tm), slot_pos.reshape(n_blocks, 1, tm)


def _blockdiag(m):
    g, a, b = m.shape
    eye = jnp.eye(g, dtype=m.dtype)
    return (eye[:, None, :, None] * m[:, :, None, :]).reshape(g * a, g * b)


def _layer_params(l, p):
    lam = jnp.zeros((SUBLANES, S5_FLAT), F32)
    lam = lam.at[0].set(p['s5_lambda_re'][l].reshape(-1)).at[1].set(p['s5_lambda_im'][l].reshape(-1))
    lam = lam.at[2].set(jnp.repeat(p['s5_log_step'][l], S5_STATE))
    gbias = jnp.zeros((1, LANES), F32).at[0, :2 * ML_HEADS].set(p['ml_gate_bias'][l])
    w_in = jnp.pad(p['w_in'][l], ((0, 0), (0, N_IN_PAD - N_IN))).astype(BF16)
    return dict(
        norm1_g=p['norm1_g'][l].reshape(1, -1), w_in=w_in, s5_lam=lam,
        s5_bre=_blockdiag(p['s5_b_re'][l].transpose(0, 2, 1)), s5_bim=_blockdiag(p['s5_b_im'][l].transpose(0, 2, 1)),
        s5_cre=_blockdiag(p['s5_c_re'][l].transpose(0, 2, 1)).astype(BF16),
        s5_cim=_blockdiag(p['s5_c_im'][l].transpose(0, 2, 1)).astype(BF16),
        s5_d=p['s5_d'][l].reshape(1, -1), s5_gw=p['s5_glu_w'][l].astype(BF16), s5_gb=p['s5_glu_b'][l].reshape(1, -1),
        rg_cw=p['rg_conv_w'][l], rg_cb=p['rg_conv_b'][l].reshape(1, -1),
        rg_wa=_blockdiag(p['rg_wa'][l]).astype(BF16), rg_ba=p['rg_ba'][l].reshape(1, -1),
        rg_wx=_blockdiag(p['rg_wx'][l]).astype(BF16), rg_bx=p['rg_bx'][l].reshape(1, -1),
        rg_lam=p['rg_lambda'][l].reshape(1, -1),
        ml_gb=gbias, ml_ng=p['ml_norm_g'][l].reshape(1, -1),
        w_out=p['w_out'][l].astype(BF16), norm2_g=p['norm2_g'][l].reshape(1, -1),
        router_w=p['router_w'][l].T.astype(BF16), router_b=p['router_b'][l].reshape(-1, 1),
    )


def _mix_and_project(l, lp, xf, states, cfg):
    nb, t, t_valid = cfg['nb'], cfg['t'], cfg['t_valid']
    n = nb * t
    s5r, s5i, rgh, rgc, mlc, mln, mlm = states
    zs5, zrg, zml, zgt = _inproj(xf, lp['norm1_g'], lp['w_in'])
    y1, nr, ni = _s5(zs5.reshape(nb, t, -1), s5r[l].reshape(nb, -1), s5i[l].reshape(nb, -1), lp['s5_lam'],
                     lp['s5_bre'], lp['s5_bim'], lp['s5_cre'], lp['s5_cim'], lp['s5_d'], lp['s5_gw'], lp['s5_gb'],
                     tc=cfg['s5_tc'], t_valid=t_valid)
    y2, nh, ncv = _rglru(zrg.reshape(nb, t, -1), rgh[l], rgc[l], lp['rg_cw'], lp['rg_cb'], lp['rg_wa'],
                         lp['rg_ba'], lp['rg_wx'], lp['rg_bx'], lp['rg_lam'], tc=cfg['s5_tc'], t_valid=t_valid)
    y3, nc_, nn_, nm_ = _mlstm(l, zml.reshape(nb, t, -1), zgt.reshape(nb, t, -1), lp['ml_gb'], lp['ml_ng'],
                               mlc, mln[l], mlm[l].reshape(nb, 1, ML_HEADS),
                               bb=cfg['ml_bb'], cl=cfg['ml_cl'], t_valid=t_valid)
    xn, hn, ridx, rgt = _outproj(xf, y1.reshape(n, -1), y2.reshape(n, -1), y3.reshape(n, -1),
                                 lp['w_out'], lp['norm2_g'], lp['router_w'], lp['router_b'])
    new_states = (nr.reshape(nb, S5_GROUPS, S5_STATE), ni.reshape(nb, S5_GROUPS, S5_STATE), nh, ncv, nc_, nn_,
                  nm_.reshape(nb, ML_HEADS))
    return xn, hn, ridx, rgt, new_states


def _trunks(xs, states, cfgs, params, final_g):
    n_all = sum(x.shape[0] for x in xs)
    tok0 = [sum(x.shape[0] for x in xs[:s]) for s in range(len(xs))]
    outs = [[[] for _ in range(7)] for _ in xs]
    b1 = params['exp_b1'].reshape(DEPTH, N_EXPERTS, 1, -1)
    b2 = params['exp_b2'].reshape(DEPTH, N_EXPERTS, 1, -1)
    for l in range(DEPTH):
        lp = _layer_params(l, params)
        mixed = [_mix_and_project(l, lp, xs[s], states[s], cfgs[s]) for s in range(len(xs))]
        hn_all = jnp.concatenate([m[1] for m in mixed], axis=0)
        ridx_all = jnp.concatenate([m[2] for m in mixed], axis=1)
        block_e, block_v, slot_tok, slot_pos = _routing_tables(ridx_all, n_all)
        y4 = _moe(l, block_e, block_v, slot_tok, slot_pos, hn_all, params['exp_w1'], b1, params['exp_w2'], b2)
        xs = [_combine(mixed[s][0], y4, mixed[s][3], final_g.reshape(1, -1), tok0=tok0[s], n_all=n_all,
                       final=(l == DEPTH - 1)) for s in range(len(xs))]
        for s in range(len(xs)):
            for lst, val in zip(outs[s], mixed[s][4]):
                lst.append(val)
    return xs, [[jnp.stack(v) for v in o] for o in outs]


def kernel(x_prompt, x_sample, state_s5_re, state_s5_im, state_rg_h, state_rg_conv, state_ml_c, state_ml_n, state_ml_m, norm1_g, w_in, s5_lambda_re, s5_lambda_im, s5_log_step, s5_b_re, s5_b_im, s5_c_re, s5_c_im, s5_d, s5_glu_w, s5_glu_b, rg_conv_w, rg_conv_b, rg_wa, rg_ba, rg_wx, rg_bx, rg_lambda, ml_gate_bias, ml_norm_g, w_out, norm2_g, router_w, router_b, exp_w1, exp_b1, exp_w2, exp_b2, final_norm_g):
    params = dict(norm1_g=norm1_g, w_in=w_in, s5_lambda_re=s5_lambda_re, s5_lambda_im=s5_lambda_im,
                  s5_log_step=s5_log_step, s5_b_re=s5_b_re, s5_b_im=s5_b_im, s5_c_re=s5_c_re, s5_c_im=s5_c_im,
                  s5_d=s5_d, s5_glu_w=s5_glu_w, s5_glu_b=s5_glu_b, rg_conv_w=rg_conv_w, rg_conv_b=rg_conv_b,
                  rg_wa=rg_wa, rg_ba=rg_ba, rg_wx=rg_wx, rg_bx=rg_bx, rg_lambda=rg_lambda,
                  ml_gate_bias=ml_gate_bias, ml_norm_g=ml_norm_g, w_out=w_out, norm2_g=norm2_g,
                  router_w=router_w, router_b=router_b, exp_w1=exp_w1, exp_b1=exp_b1, exp_w2=exp_w2, exp_b2=exp_b2)
    bp, tp, _ = x_prompt.shape
    bs, ts, _ = x_sample.shape
    assert tp >= RG_CONV - 1 and ts >= RG_CONV - 1
    zeros = lambda *shape: jnp.zeros((DEPTH, bp) + shape, F32)
    prompt_states = (zeros(S5_GROUPS, S5_STATE), zeros(S5_GROUPS, S5_STATE), zeros(RG_WIDTH),
                     zeros(RG_CONV - 1, RG_WIDTH), zeros(ML_HEADS, ML_DK, ML_DV), zeros(ML_HEADS, ML_DK),
                     zeros(ML_HEADS))
    ts_pad = -(-ts // SUBLANES) * SUBLANES
    xs_pad = jnp.pad(x_sample, ((0, 0), (0, ts_pad - ts), (0, 0)))
    sample_states = (state_s5_re, state_s5_im, state_rg_h, state_rg_conv, state_ml_c, state_ml_n, state_ml_m)
    cfgs = [dict(nb=bp, t=tp, t_valid=tp, s5_tc=128, ml_bb=8, ml_cl=math.gcd(tp, ML_CHUNK)),
            dict(nb=bs, t=ts_pad, t_valid=ts, s5_tc=ts_pad, ml_bb=8, ml_cl=ts_pad)]
    (yp, ys), (sp, ss) = _trunks([x_prompt.reshape(bp * tp, D_MODEL), xs_pad.reshape(bs * ts_pad, D_MODEL)],
                                 [prompt_states, sample_states], cfgs, params, final_norm_g)
    return (yp.reshape(bp, tp, D_MODEL), ys.reshape(bs, ts_pad, D_MODEL)[:, :ts], *sp, *ss)
```

```python
import functools
import math

import jax
import jax.numpy as jnp
from jax import lax
from jax.experimental import pallas as pl
from jax.experimental.pallas import tpu as pltpu

F32 = jnp.float32
BF16 = jnp.bfloat16

SUBLANES = 8
LANES = 128
VMEM_LIMIT_BYTES = 56 * 1024 * 1024

D_MODEL = 1024
DEPTH = 2
S5_WIDTH = 256
S5_GROUP = 16
S5_GROUPS = 16
S5_STATE = 64
S5_FLAT = S5_GROUPS * S5_STATE
RG_WIDTH = 256
RG_BLOCKS = 8
RG_CONV = 4
RG_C = 8.0
ML_WIDTH = 512
ML_HEADS = 4
ML_DK = 128
ML_DV = 128
ML_CHUNK_LEN = 128
N_EXPERTS = 32
TOP_K = 4
D_FF = 1024
SWIGLU_LIMIT = 7.0
SWIGLU_ALPHA = 1.702
EPS = 1e-5

IN_S5 = (0, 256)
IN_RG = (256, 768)
IN_ML = (768, 2816)
IN_GATE = (2816, 2944)
N_IN = 2824
N_IN_PAD = 2944

ROW_TILE = 512
ROUTE_TILE = 128
MOE_TILE = 512
NEG_BIG = -1e30


def _cparams(sem):
    return pltpu.CompilerParams(dimension_semantics=sem, vmem_limit_bytes=VMEM_LIMIT_BYTES)


def _full(shape):
    n = len(shape)
    return pl.BlockSpec(shape, lambda *_: (0,) * n)


ROW_TILES = D_MODEL // LANES
assert ROW_TILES == SUBLANES


def _rows_to_tiles(x, dst_ref):
    m = x.shape[0]
    for j in range(ROW_TILES):
        dst_ref[pl.ds(j, m, stride=ROW_TILES), :] = x[:, j * LANES:(j + 1) * LANES]


def _tiles_to_rows(src_ref):
    m = src_ref.shape[0] // ROW_TILES
    return jnp.concatenate([src_ref[pl.ds(j, m, stride=ROW_TILES), :] for j in range(ROW_TILES)], axis=-1)


def _inproj_body(x_ref, g_ref, w_ref, zs5_ref, zrg_ref, zml_ref, zgt_ref):
    x = x_ref[...]
    h = x * lax.rsqrt(jnp.mean(x * x, axis=-1, keepdims=True) + EPS) * g_ref[...]
    hb = h.astype(BF16)
    for ref, (lo, hi) in ((zs5_ref, IN_S5), (zrg_ref, IN_RG), (zml_ref, IN_ML), (zgt_ref, IN_GATE)):
        ref[...] = jnp.dot(hb, w_ref[:, lo:hi], preferred_element_type=F32)


def _inproj(x, g, w):
    n = x.shape[0]
    tm = ROW_TILE
    widths = [hi - lo for lo, hi in (IN_S5, IN_RG, IN_ML, IN_GATE)]
    return pl.pallas_call(
        _inproj_body,
        grid=(n // tm,),
        in_specs=[pl.BlockSpec((tm, D_MODEL), lambda i: (i, 0)), _full((1, D_MODEL)), _full((D_MODEL, N_IN_PAD))],
        out_specs=[pl.BlockSpec((tm, w_), lambda i: (i, 0)) for w_ in widths],
        out_shape=[jax.ShapeDtypeStruct((n, w_), F32) for w_ in widths],
        compiler_params=_cparams(("arbitrary",)),
        name="inproj",
    )(x, g, w)


def _s5_body(u_ref, s0r_ref, s0i_ref, lam_ref, bre_ref, bim_ref, cre_ref, cim_ref, d_ref, gw_ref, gb_ref,
             y_ref, sr_ref, si_ref, bur, bui, ab, bbr, bbi, tm_s, bm_s, *, nb, tc, nc, t_valid):
    c = pl.program_id(0)

    @pl.when(c == 0)
    def _():
        sr_ref[...] = s0r_ref[...]
        si_ref[...] = s0i_ref[...]
        lr = lam_ref[0:1, :]
        li = lam_ref[1:2, :]
        step = jnp.exp(lam_ref[2:3, :])
        mag = jnp.exp(lr * step)
        ab_re = mag * jnp.cos(li * step)
        ab_im = mag * jnp.sin(li * step)
        den = lr * lr + li * li
        num_re = ab_re - 1.0
        z_re = (num_re * lr + ab_im * li) / den
        z_im = (ab_im * lr - num_re * li) / den
        ab[0:1, :] = ab_re
        ab[1:2, :] = ab_im
        bbr[...] = (z_re * bre_ref[...] - z_im * bim_ref[...]).astype(BF16)
        bbi[...] = (z_re * bim_ref[...] + z_im * bre_ref[...]).astype(BF16)

    _to_time_major(u_ref, tm_s, bm_s, nb, tc)
    u = jnp.concatenate([tm_s[j] for j in range(S5_WIDTH // LANES)], axis=-1)
    ub = u.astype(BF16)
    bur[...] = jnp.dot(ub, bbr[...], preferred_element_type=F32)
    bui[...] = jnp.dot(ub, bbi[...], preferred_element_type=F32)

    a_re = jnp.broadcast_to(ab[0:1, :], (SUBLANES, S5_FLAT))
    a_im = jnp.broadcast_to(ab[1:2, :], (SUBLANES, S5_FLAT))
    last_in = t_valid - (nc - 1) * tc
    steps = tc if last_in == tc else jnp.where(c == nc - 1, last_in, tc)

    def group(gi, _):
        g8 = gi * SUBLANES
        rows8 = pl.ds(pl.multiple_of(g8, SUBLANES), SUBLANES)

        def step_fn(t, carry):
            s_re, s_im = carry
            rows = pl.ds(pl.multiple_of(t * nb + g8, SUBLANES), SUBLANES)
            n_re = a_re * s_re - a_im * s_im + bur[rows, :]
            n_im = a_re * s_im + a_im * s_re + bui[rows, :]
            bur[rows, :] = n_re
            bui[rows, :] = n_im
            return n_re, n_im

        s_re, s_im = lax.fori_loop(0, steps, step_fn, (sr_ref[rows8, :], si_ref[rows8, :]))
        sr_ref[rows8, :] = s_re
        si_ref[rows8, :] = s_im
        return 0

    lax.fori_loop(0, nb // SUBLANES, group, 0)

    y = (jnp.dot(bur[...].astype(BF16), cre_ref[...], preferred_element_type=F32)
         - jnp.dot(bui[...].astype(BF16), cim_ref[...], preferred_element_type=F32))
    y = y + d_ref[...] * u
    g = jax.nn.gelu(y)
    gate = jax.nn.sigmoid(jnp.dot(g.astype(BF16), gw_ref[...], preferred_element_type=F32) + gb_ref[...])
    out = g * gate
    for j in range(S5_WIDTH // LANES):
        tm_s[j] = out[:, j * LANES:(j + 1) * LANES]
    _to_batch_major(tm_s, bm_s, y_ref, nb, tc)


def _to_time_major(src_ref, tm_s, bm_s, nb, tc):
    for j in range(tm_s.shape[0]):
        lanes = slice(j * LANES, (j + 1) * LANES)
        if nb <= tc:
            for b in range(nb):
                tm_s[j, pl.ds(b, tc, stride=nb), :] = src_ref[b, :, lanes]
        else:
            bm_s[j] = src_ref[:, :, lanes].reshape(nb * tc, LANES)
            for t in range(tc):
                tm_s[j, t * nb:(t + 1) * nb, :] = bm_s[j, pl.ds(t, nb, stride=tc), :]


def _to_batch_major(tm_s, bm_s, dst_ref, nb, tc):
    for j in range(tm_s.shape[0]):
        lanes = slice(j * LANES, (j + 1) * LANES)
        if nb <= tc:
            for b in range(nb):
                dst_ref[b, :, lanes] = tm_s[j, pl.ds(b, tc, stride=nb), :]
        else:
            for t in range(tc):
                bm_s[j, pl.ds(t, nb, stride=tc), :] = tm_s[j, t * nb:(t + 1) * nb, :]
            dst_ref[:, :, lanes] = bm_s[j].reshape(nb, tc, LANES)


def _s5(u, s0r, s0i, lam, bre, bim, cre, cim, d, gw, gb, *, tc, t_valid):
    nb, t, _ = u.shape
    nc = t // tc
    assert (nc - 1) * tc < t_valid <= t
    body = functools.partial(_s5_body, nb=nb, tc=tc, nc=nc, t_valid=t_valid)
    return pl.pallas_call(
        body,
        grid=(t // tc,),
        in_specs=[pl.BlockSpec((nb, tc, S5_WIDTH), lambda c: (0, c, 0)),
                  _full((nb, S5_FLAT)), _full((nb, S5_FLAT)), _full((SUBLANES, S5_FLAT)),
                  _full((S5_WIDTH, S5_FLAT)), _full((S5_WIDTH, S5_FLAT)),
                  _full((S5_FLAT, S5_WIDTH)), _full((S5_FLAT, S5_WIDTH)),
                  _full((1, S5_WIDTH)), _full((S5_WIDTH, S5_WIDTH)), _full((1, S5_WIDTH))],
        out_specs=[pl.BlockSpec((nb, tc, S5_WIDTH), lambda c: (0, c, 0)),
                   _full((nb, S5_FLAT)), _full((nb, S5_FLAT))],
        out_shape=[jax.ShapeDtypeStruct((nb, t, S5_WIDTH), F32),
                   jax.ShapeDtypeStruct((nb, S5_FLAT), F32), jax.ShapeDtypeStruct((nb, S5_FLAT), F32)],
        scratch_shapes=[pltpu.VMEM((nb * tc, S5_FLAT), F32), pltpu.VMEM((nb * tc, S5_FLAT), F32),
                        pltpu.VMEM((SUBLANES, S5_FLAT), F32),
                        pltpu.VMEM((S5_WIDTH, S5_FLAT), BF16), pltpu.VMEM((S5_WIDTH, S5_FLAT), BF16),
                        pltpu.VMEM((S5_WIDTH // LANES, nb * tc, LANES), F32),
                        pltpu.VMEM((S5_WIDTH // LANES, nb * tc, LANES), F32)],
        compiler_params=_cparams(("arbitrary",)),
        name="s5_scan",
    )(u, s0r, s0i, lam, bre, bim, cre, cim, d, gw, gb)


def _rglru_body(z_ref, h0_ref, cb0_ref, cw_ref, cb_ref, wa_ref, ba_ref, wx_ref, bx_ref, lam_ref,
                y_ref, h_ref, cbn_ref, xs, a_s, b_s, tm_s, bm_s, *, nb, tc, nc, t_valid):
    c = pl.program_id(0)
    hist = RG_CONV - 1

    @pl.when(c == 0)
    def _():
        h_ref[...] = h0_ref[...]
        xs[0:hist * nb, :] = cb0_ref[...]

    _to_time_major(z_ref, tm_s, bm_s, nb, tc)
    xs[hist * nb:(hist + tc) * nb, :] = jnp.concatenate([tm_s[j] for j in range(RG_WIDTH // LANES)], axis=-1)

    xc = jnp.zeros((tc * nb, RG_WIDTH), F32) + cb_ref[...]
    for j in range(RG_CONV):
        xc = xc + xs[j * nb:(j + tc) * nb, :] * cw_ref[j:j + 1, :]

    last_in = t_valid - (nc - 1) * tc
    last = tc if last_in == tc else jnp.where(c == nc - 1, last_in, tc)

    @pl.when(c == nc - 1)
    def _():
        cbn_ref[...] = xs[last_in * nb:(last_in + hist) * nb, :]

    xs[0:hist * nb, :] = xs[tc * nb:(tc + hist) * nb, :]

    xb = xc.astype(BF16)
    r = jax.nn.sigmoid(jnp.dot(xb, wa_ref[...], preferred_element_type=F32) + ba_ref[...])
    i = jax.nn.sigmoid(jnp.dot(xb, wx_ref[...], preferred_element_type=F32) + bx_ref[...])
    log_a = -RG_C * r * jax.nn.softplus(-lam_ref[...])
    a_s[...] = jnp.exp(log_a)
    th = jnp.tanh(log_a)
    b_s[...] = jnp.sqrt(-2.0 * th / (1.0 - th)) * (i * xc)

    def group(gi, _):
        g8 = gi * SUBLANES
        rows8 = pl.ds(pl.multiple_of(g8, SUBLANES), SUBLANES)

        def step_fn(t, h):
            rows = pl.ds(pl.multiple_of(t * nb + g8, SUBLANES), SUBLANES)
            h = a_s[rows, :] * h + b_s[rows, :]
            b_s[rows, :] = h
            return h

        h_ref[rows8, :] = lax.fori_loop(0, last, step_fn, h_ref[rows8, :])
        return 0

    lax.fori_loop(0, nb // SUBLANES, group, 0)
    for j in range(RG_WIDTH // LANES):
        tm_s[j] = b_s[:, j * LANES:(j + 1) * LANES]
    _to_batch_major(tm_s, bm_s, y_ref, nb, tc)
    y_ref[...] = y_ref[...] * jax.nn.gelu(z_ref[:, :, RG_WIDTH:2 * RG_WIDTH])


def _rglru(z, h0, cb0, cw, cb, wa, ba, wx, bx, lam, *, tc, t_valid):
    nb, t, _ = z.shape
    hist = RG_CONV - 1
    nc = t // tc
    assert (nc - 1) * tc + hist <= t_valid <= t
    body = functools.partial(_rglru_body, nb=nb, tc=tc, nc=nc, t_valid=t_valid)
    y, h, cbn = pl.pallas_call(
        body,
        grid=(t // tc,),
        in_specs=[pl.BlockSpec((nb, tc, 2 * RG_WIDTH), lambda c: (0, c, 0)),
                  _full((nb, RG_WIDTH)), _full((hist * nb, RG_WIDTH)),
                  _full((RG_CONV, RG_WIDTH)), _full((1, RG_WIDTH)),
                  _full((RG_WIDTH, RG_WIDTH)), _full((1, RG_WIDTH)),
                  _full((RG_WIDTH, RG_WIDTH)), _full((1, RG_WIDTH)), _full((1, RG_WIDTH))],
        out_specs=[pl.BlockSpec((nb, tc, RG_WIDTH), lambda c: (0, c, 0)),
                   _full((nb, RG_WIDTH)), _full((hist * nb, RG_WIDTH))],
        out_shape=[jax.ShapeDtypeStruct((nb, t, RG_WIDTH), F32),
                   jax.ShapeDtypeStruct((nb, RG_WIDTH), F32), jax.ShapeDtypeStruct((hist * nb, RG_WIDTH), F32)],
        scratch_shapes=[pltpu.VMEM(((tc + hist) * nb, RG_WIDTH), F32),
                        pltpu.VMEM((nb * tc, RG_WIDTH), F32), pltpu.VMEM((nb * tc, RG_WIDTH), F32),
                        pltpu.VMEM((RG_WIDTH // LANES, nb * tc, LANES), F32),
                        pltpu.VMEM((RG_WIDTH // LANES, nb * tc, LANES), F32)],
        compiler_params=_cparams(("arbitrary",)),
        name="rglru_scan",
    )(z, h0, cb0.transpose(1, 0, 2).reshape(hist * nb, RG_WIDTH), cw, cb, wa, ba, wx, bx, lam)
    return y, h, cbn.reshape(hist, nb, RG_WIDTH).transpose(1, 0, 2)


def _dot_nt(a, b, **kw):
    return lax.dot_general(a, b, (((1,), (1,)), ((), ())), preferred_element_type=F32, **kw)


def _dot_tn(a, b, **kw):
    return lax.dot_general(a, b, (((0,), (0,)), ((), ())), preferred_element_type=F32, **kw)


def _mlstm_body(q_ref, k_ref, v_ref, o_ref, g_ref, gb_ref, ng_ref, c0_ref, n0_ref, m0_ref,
                y_ref, cn_ref, nn_ref, mn_ref, *, bb, cl, t_valid):
    c = pl.program_id(1)

    @pl.when(c == 0)
    def _():
        cn_ref[...] = c0_ref[0]
        nn_ref[...] = n0_ref[...]
        mn_ref[...] = m0_ref[...]

    row = lax.broadcasted_iota(jnp.int32, (cl, cl), 0)
    col = lax.broadcasted_iota(jnp.int32, (cl, cl), 1)
    tril = (col <= row).astype(F32)
    keep = (col <= row) & (col + c * cl < t_valid)
    t_ok = (lax.broadcasted_iota(jnp.int32, (cl, LANES), 0) + c * cl) < t_valid
    lane = lax.broadcasted_iota(jnp.int32, (cl, LANES), 1)
    ones = jnp.ones((cl, LANES), F32)

    chains = [(bi, h) for bi in range(bb) for h in range(ML_HEADS)]
    hsl = lambda h: slice(h * ML_DK, (h + 1) * ML_DK)
    hi = lax.Precision.HIGHEST

    bcum, ig = [], []
    for bi in range(bb):
        graw = g_ref[bi] + gb_ref[...]
        lf = jnp.where(t_ok, jax.nn.log_sigmoid(graw), 0.0)
        bcum.append(jnp.dot(tril, lf, preferred_element_type=F32, precision=hi))
        ig.append(jnp.where(t_ok, graw, NEG_BIG))

    qf = {ch: q_ref[ch[0], :, hsl(ch[1])] for ch in chains}
    qh = {ch: qf[ch].astype(BF16) for ch in chains}
    kh = {ch: k_ref[ch[0], :, hsl(ch[1])] * (ML_DK ** -0.5) for ch in chains}
    vh = {ch: v_ref[ch[0], :, hsl(ch[1])].astype(BF16) for ch in chains}
    cm = {ch: cn_ref[ch[0], ch[1]] for ch in chains}
    nv = {ch: nn_ref[ch[0], ch[1]:ch[1] + 1, :] for ch in chains}
    m_old = {ch: mn_ref[ch[0], :, ch[1]:ch[1] + 1] for ch in chains}
    b_col = {ch: bcum[ch[0]][:, ML_HEADS + ch[1]:ML_HEADS + ch[1] + 1] for ch in chains}
    i_col = {ch: ig[ch[0]][:, ch[1]:ch[1] + 1] for ch in chains}

    w_row = {ch: _dot_nt(ones, jnp.where(lane == ch[1], i_col[ch] - b_col[ch], 0.0), precision=hi) for ch in chains}
    qk = {ch: _dot_nt(qh[ch], kh[ch].astype(BF16)) for ch in chains}
    qc = {ch: jnp.dot(qh[ch], cm[ch].astype(BF16), preferred_element_type=F32) for ch in chains}
    dmat = {ch: jnp.where(keep, b_col[ch] + w_row[ch], -jnp.inf) for ch in chains}
    dmax = {ch: jnp.max(dmat[ch], axis=-1, keepdims=True) for ch in chains}
    inter = {ch: b_col[ch] + m_old[ch] for ch in chains}
    mt = {ch: jnp.maximum(inter[ch], dmax[ch]) for ch in chains}
    s = {ch: qk[ch] * jnp.exp(dmat[ch] - mt[ch]) for ch in chains}
    sc = {ch: jnp.exp(inter[ch] - mt[ch]) for ch in chains}
    sv = {ch: jnp.dot(s[ch].astype(BF16), vh[ch], preferred_element_type=F32) for ch in chains}
    qn = {ch: jnp.sum(qf[ch] * nv[ch], axis=-1, keepdims=True) for ch in chains}
    den = {ch: jnp.sum(s[ch], axis=-1, keepdims=True) + sc[ch] * qn[ch] for ch in chains}
    hh = {ch: (sv[ch] + sc[ch] * qc[ch]) / jnp.maximum(jnp.abs(den[ch]), jnp.exp(-mt[ch])) for ch in chains}
    bl = {ch: b_col[ch][cl - 1:cl, :] for ch in chains}
    m_new = {ch: mt[ch][cl - 1:cl, :] for ch in chains}
    wk = {ch: jnp.exp(bl[ch] - b_col[ch] + i_col[ch] - m_new[ch]) * kh[ch] for ch in chains}
    decay = {ch: jnp.exp(bl[ch] + m_old[ch] - m_new[ch]) for ch in chains}
    kv = {ch: _dot_tn(wk[ch].astype(BF16), vh[ch]) for ch in chains}
    rms = {ch: lax.rsqrt(jnp.mean(hh[ch] * hh[ch], axis=-1, keepdims=True) + EPS) for ch in chains}
    for ch in chains:
        bi, h = ch
        cn_ref[bi, h] = decay[ch] * cm[ch] + kv[ch]
        nn_ref[bi, h:h + 1, :] = decay[ch] * nv[ch] + jnp.sum(wk[ch], axis=0, keepdims=True)
        mn_ref[bi, :, h:h + 1] = m_new[ch]
        y_ref[bi, :, hsl(h)] = hh[ch] * rms[ch] * ng_ref[:, hsl(h)] * jax.nn.sigmoid(o_ref[bi, :, hsl(h)])


def _mlstm(layer, zml, zgt, gbias, ng, c0, n0, m0, *, bb, cl, t_valid):
    nb, t, _ = zml.shape
    nc = t // cl
    body = functools.partial(_mlstm_body, bb=bb, cl=cl, t_valid=t_valid)
    zspec = lambda j: pl.BlockSpec((bb, cl, ML_WIDTH), lambda i, c, j=j: (i, c, j))
    return pl.pallas_call(
        body,
        grid=(nb // bb, nc),
        in_specs=[zspec(0), zspec(1), zspec(2), zspec(3),
                  pl.BlockSpec((bb, cl, LANES), lambda i, c: (i, c, 0)),
                  _full((1, LANES)), _full((1, ML_WIDTH)),
                  pl.BlockSpec((1, bb, ML_HEADS, ML_DK, ML_DV), lambda i, c: (layer, i, 0, 0, 0)),
                  pl.BlockSpec((bb, ML_HEADS, ML_DK), lambda i, c: (i, 0, 0)),
                  pl.BlockSpec((bb, 1, ML_HEADS), lambda i, c: (i, 0, 0))],
        out_specs=[pl.BlockSpec((bb, cl, ML_WIDTH), lambda i, c: (i, c, 0)),
                   pl.BlockSpec((bb, ML_HEADS, ML_DK, ML_DV), lambda i, c: (i, 0, 0, 0)),
                   pl.BlockSpec((bb, ML_HEADS, ML_DK), lambda i, c: (i, 0, 0)),
                   pl.BlockSpec((bb, 1, ML_HEADS), lambda i, c: (i, 0, 0))],
        out_shape=[jax.ShapeDtypeStruct((nb, t, ML_WIDTH), F32),
                   jax.ShapeDtypeStruct((nb, ML_HEADS, ML_DK, ML_DV), F32),
                   jax.ShapeDtypeStruct((nb, ML_HEADS, ML_DK), F32),
                   jax.ShapeDtypeStruct((nb, 1, ML_HEADS), F32)],
        compiler_params=_cparams(("arbitrary", "arbitrary")),
        name="mlstm_chunks",
    )(zml, zml, zml, zml, zgt, gbias, ng, c0, n0, m0)


def _outproj_body(x_ref, y1_ref, y2_ref, y3_ref, wo_ref, g2_ref, rw_ref, rb_ref,
                  xn_ref, hn_ref, ridx_ref, rgt_ref):
    acc = jnp.dot(y1_ref[...].astype(BF16), wo_ref[0:256, :], preferred_element_type=F32)
    acc = acc + jnp.dot(y2_ref[...].astype(BF16), wo_ref[256:512, :], preferred_element_type=F32)
    acc = acc + jnp.dot(y3_ref[...].astype(BF16), wo_ref[512:1024, :], preferred_element_type=F32)
    x = x_ref[...] + acc
    xn_ref[...] = x
    h = x * lax.rsqrt(jnp.mean(x * x, axis=-1, keepdims=True) + EPS) * g2_ref[...]
    _rows_to_tiles(h, hn_ref)
    hb = h.astype(BF16)
    tm = x.shape[0]
    eidx = lax.broadcasted_iota(jnp.int32, (N_EXPERTS, ROUTE_TILE), 0)
    sub = lax.broadcasted_iota(jnp.int32, (LANES, ROUTE_TILE), 0)
    for j in range(tm // ROUTE_TILE):
        rows = slice(j * ROUTE_TILE, (j + 1) * ROUTE_TILE)
        vals = _dot_nt(rw_ref[...], hb[rows, :]) + rb_ref[...]
        tops, idxs = [], []
        for _ in range(TOP_K):
            m = jnp.max(vals, axis=0, keepdims=True)
            idx = jnp.min(jnp.where(vals == m, eidx, N_EXPERTS), axis=0, keepdims=True)
            vals = jnp.where(eidx == idx, -jnp.inf, vals)
            tops.append(m)
            idxs.append(idx)
        exps = [jnp.exp(t - tops[0]) for t in tops]
        inv = 1.0 / functools.reduce(lambda a, b: a + b, exps)
        ridx_ref[:, rows] = jnp.concatenate(idxs + idxs, axis=0)
        gt = jnp.zeros((LANES, ROUTE_TILE), F32)
        for kk in range(TOP_K):
            gt = jnp.where(sub == kk, exps[kk] * inv, gt)
        rgt_ref[rows, :] = gt.T


def _outproj(x, y1, y2, y3, wo, g2, rw, rb):
    n = x.shape[0]
    tm = ROW_TILE
    rowspec = lambda w_: pl.BlockSpec((tm, w_), lambda i: (i, 0))
    return pl.pallas_call(
        _outproj_body,
        grid=(n // tm,),
        in_specs=[rowspec(D_MODEL), rowspec(S5_WIDTH), rowspec(RG_WIDTH), rowspec(ML_WIDTH),
                  _full((D_MODEL, D_MODEL)), _full((1, D_MODEL)),
                  _full((N_EXPERTS, D_MODEL)), _full((N_EXPERTS, 1))],
        out_specs=[rowspec(D_MODEL), pl.BlockSpec((tm * ROW_TILES, LANES), lambda i: (i, 0)),
                   pl.BlockSpec((2 * TOP_K, tm), lambda i: (0, i)), rowspec(LANES)],
        out_shape=[jax.ShapeDtypeStruct((n, D_MODEL), F32), jax.ShapeDtypeStruct((n * ROW_TILES, LANES), F32),
                   jax.ShapeDtypeStruct((2 * TOP_K, n), jnp.int32), jax.ShapeDtypeStruct((n, LANES), F32)],
        compiler_params=_cparams(("arbitrary",)),
        name="outproj_route",
    )(x, y1, y2, y3, wo, g2, rw, rb)


GATHER_AHEAD = 2
N_XBUF = GATHER_AHEAD + 1
N_OBUF = 3


def _moe_body(be_ref, bv_ref, tokn_ref, posp_ref, posc_ref, hn_ref, w1_ref, b1_ref, w2_ref, b2_ref, y4_ref,
              w1b, w2b, xbuf, obuf, gsem, ssem):
    i = pl.program_id(0)
    last = pl.num_programs(0) - 1
    real = bv_ref[i] > 0
    fresh = jnp.logical_or(i == GATHER_AHEAD, be_ref[i] != be_ref[jnp.maximum(i - 1, 0)])
    xs, xn = lax.rem(i, N_XBUF), lax.rem(i + GATHER_AHEAD, N_XBUF)
    os_, op, opp = lax.rem(i, N_OBUF), lax.rem(i + N_OBUF - 1, N_OBUF), lax.rem(i + N_OBUF - 2, N_OBUF)
    x_cur, x_nxt, o_cur, o_prev = xbuf.at[xs], xbuf.at[xn], obuf.at[os_], obuf.at[op]

    def tile(ref, row):
        return ref.at[pl.ds(pl.multiple_of(row, ROW_TILES), ROW_TILES)]

    def start_gathers(rows=range(MOE_TILE)):
        for r in rows:
            pltpu.make_async_copy(tile(hn_ref, tokn_ref[0, 0, r]), tile(x_nxt, r * ROW_TILES),
                                  gsem.at[xn]).start(priority=0)

    def start_scatters(pos_ref, o_buf, sem, rows=range(MOE_TILE)):
        for r in rows:
            pltpu.make_async_copy(tile(o_buf, r * ROW_TILES), tile(y4_ref, pos_ref[0, 0, r]), sem).start(priority=1)

    def wait_scatters(o_buf, sem):
        pltpu.make_async_copy(o_buf, y4_ref.at[pl.ds(0, MOE_TILE * ROW_TILES)], sem).wait()

    gathered = jnp.logical_and(i >= GATHER_AHEAD, jnp.logical_or(
        i < 2 * GATHER_AHEAD, bv_ref[jnp.maximum(i - GATHER_AHEAD, 0)] > 0))

    @pl.when(gathered)
    def _():
        pltpu.make_async_copy(hn_ref.at[pl.ds(0, MOE_TILE * ROW_TILES)], x_cur, gsem.at[xs]).wait()

    @pl.when(i >= N_OBUF)
    def _():
        wait_scatters(o_cur, ssem.at[os_])

    @pl.when(jnp.logical_and(fresh, real))
    def _():
        w1b[...] = w1_ref[0, 0].astype(BF16)
        w2b[...] = w2_ref[0, 0].astype(BF16)

    @pl.when(bv_ref[i] >= 1)
    def _():
        start_gathers()

    @pl.when(real)
    def _():
        xb = _tiles_to_rows(x_cur).astype(BF16)
        start_scatters(posp_ref, o_prev, ssem.at[op])
        hb = jnp.dot(xb, w1b[...], preferred_element_type=F32) + b1_ref[0, 0]
        g = jnp.minimum(hb[:, :D_FF], SWIGLU_LIMIT)
        u = jnp.clip(hb[:, D_FF:], -SWIGLU_LIMIT, SWIGLU_LIMIT)
        act = g * jax.nn.sigmoid(SWIGLU_ALPHA * g) * (u + 1.0)
        _rows_to_tiles(jnp.dot(act.astype(BF16), w2b[...], preferred_element_type=F32) + b2_ref[0, 0], o_cur)

    @pl.when(jnp.logical_not(real))
    def _():
        @pl.when(i == 0)
        def _():
            obuf[...] = jnp.zeros_like(obuf)

        @pl.when(i < GATHER_AHEAD)
        def _():
            start_gathers()

        @pl.when(i >= 1)
        def _():
            start_scatters(posp_ref, o_prev, ssem.at[op])

    @pl.when(i == last)
    def _():
        start_scatters(posc_ref, o_cur, ssem.at[os_])
        wait_scatters(obuf.at[opp], ssem.at[opp])
        wait_scatters(o_prev, ssem.at[op])
        wait_scatters(o_cur, ssem.at[os_])


def _moe(layer, block_e, block_v, slot_tok, slot_pos, hn, w1, b1, w2, b2):
    n_blocks = block_e.shape[0]
    tm = MOE_TILE
    n_rows = n_blocks * tm
    assert n_blocks > 2 * GATHER_AHEAD
    idx_spec = lambda f: pl.BlockSpec((1, 1, tm), f, memory_space=pltpu.SMEM)
    grid_spec = pltpu.PrefetchScalarGridSpec(
        num_scalar_prefetch=2,
        grid=(n_blocks,),
        in_specs=[idx_spec(lambda i, be, bv: (jnp.minimum(i + GATHER_AHEAD, n_blocks - 1), 0, 0)),
                  idx_spec(lambda i, be, bv: (jnp.maximum(i - 1, 0), 0, 0)),
                  idx_spec(lambda i, be, bv: (i, 0, 0)),
                  pl.BlockSpec(memory_space=pl.ANY),
                  pl.BlockSpec((1, 1, D_MODEL, 2 * D_FF), lambda i, be, bv: (layer, be[i], 0, 0)),
                  pl.BlockSpec((1, 1, 1, 2 * D_FF), lambda i, be, bv: (layer, be[i], 0, 0)),
                  pl.BlockSpec((1, 1, D_FF, D_MODEL), lambda i, be, bv: (layer, be[i], 0, 0)),
                  pl.BlockSpec((1, 1, 1, D_MODEL), lambda i, be, bv: (layer, be[i], 0, 0))],
        out_specs=pl.BlockSpec(memory_space=pl.ANY),
        scratch_shapes=[pltpu.VMEM((D_MODEL, 2 * D_FF), BF16), pltpu.VMEM((D_FF, D_MODEL), BF16)]
                       + [pltpu.VMEM((N_XBUF, tm * ROW_TILES, LANES), F32),
                          pltpu.VMEM((N_OBUF, tm * ROW_TILES, LANES), F32)]
                       + [pltpu.SemaphoreType.DMA((N_XBUF,)), pltpu.SemaphoreType.DMA((N_OBUF,))],
    )
    return pl.pallas_call(
        _moe_body,
        grid_spec=grid_spec,
        out_shape=jax.ShapeDtypeStruct((n_rows * ROW_TILES, LANES), F32),
        compiler_params=_cparams(("arbitrary",)),
        name="moe_experts",
    )(block_e, block_v, slot_tok, slot_pos, slot_pos, hn, w1, b1, w2, b2)


def _combine_body(x_ref, rgt_ref, fg_ref, *rest, final):
    y_refs, o_ref = rest[:TOP_K], rest[TOP_K]
    x = x_ref[...]
    for kk in range(TOP_K):
        x = x + rgt_ref[:, kk:kk + 1] * _tiles_to_rows(y_refs[kk])
    if final:
        x = x * lax.rsqrt(jnp.mean(x * x, axis=-1, keepdims=True) + EPS) * fg_ref[...]
    o_ref[...] = x


def _combine(x, y4, rgt, fg, *, tok0, n_all, final):
    n = x.shape[0]
    tm = ROW_TILE
    assert tok0 % tm == 0 and n_all % tm == 0
    yspec = lambda kk: pl.BlockSpec((tm * ROW_TILES, LANES), lambda i, kk=kk: ((kk * n_all + tok0) // tm + i, 0))
    return pl.pallas_call(
        functools.partial(_combine_body, final=final),
        grid=(n // tm,),
        in_specs=[pl.BlockSpec((tm, D_MODEL), lambda i: (i, 0)),
                  pl.BlockSpec((tm, LANES), lambda i: (i, 0)), _full((1, D_MODEL))]
                 + [yspec(kk) for kk in range(TOP_K)],
        out_specs=pl.BlockSpec((tm, D_MODEL), lambda i: (i, 0)),
        out_shape=jax.ShapeDtypeStruct((n, D_MODEL), F32),
        compiler_params=_cparams(("arbitrary",)),
        name="moe_combine",
    )(x, rgt, fg, *([y4] * TOP_K))


def _routing_tables(ridx, n):
    na = n * TOP_K
    tm = MOE_TILE
    assert na % tm == 0
    lead = GATHER_AHEAD * tm
    n_blocks = na // tm + N_EXPERTS + 2 * GATHER_AHEAD
    n_slots = n_blocks * tm
    experts = jnp.arange(N_EXPERTS, dtype=jnp.int32)
    flat_e = ridx[:TOP_K, :].reshape(-1)
    order = jnp.argsort(flat_e).astype(jnp.int32)
    counts = jnp.sum(flat_e[None, :] == experts[:, None], axis=1).astype(jnp.int32)
    padded = (counts + tm - 1) // tm * tm
    pad_end = jnp.cumsum(padded)
    pad_start = pad_end - padded
    start = jnp.cumsum(counts) - counts
    spare_start = na + pad_start - start
    total = pad_end[-1]
    blk = jnp.arange(n_blocks, dtype=jnp.int32) * tm - lead
    block_v = ((blk >= 0) & (blk < total)).astype(jnp.int32)
    last_e = jnp.max(jnp.where(counts > 0, experts, 0))
    block_e = jnp.minimum(jnp.sum(blk[:, None] >= pad_end[None, :], axis=1), last_e).astype(jnp.int32)
    rank = (blk - pad_start[block_e])[:, None] + jnp.arange(tm, dtype=jnp.int32)[None, :]
    cnt = counts[block_e][:, None]
    slot = blk[:, None] + jnp.arange(tm, dtype=jnp.int32)[None, :]
    used = (slot >= 0) & (slot < total)
    real = (rank < cnt) & used
    src = order[jnp.clip(start[block_e][:, None] + rank, 0, na - 1)]
    slot_tok = (jnp.where(real, src % n, 0) * ROW_TILES).astype(jnp.int32)
    spare = jnp.where(used, spare_start[block_e][:, None] + rank - cnt, jnp.where(slot >= 0, slot, n_slots + slot))
    slot_pos = (jnp.where(real, src, spare) * ROW_TILES).astype(jnp.int32)
    return block_e, block_v, slot_tok.reshape(n_blocks, 1, tm), slot_pos.reshape(n_blocks, 1, tm)


def _blockdiag(m):
    g, a, b = m.shape
    eye = jnp.eye(g, dtype=m.dtype)
    return (eye[:, None, :, None] * m[:, :, None, :]).reshape(g * a, g * b)


def _layer_params(l, p):
    lam = jnp.zeros((SUBLANES, S5_FLAT), F32)
    lam = lam.at[0].set(p['s5_lambda_re'][l].reshape(-1)).at[1].set(p['s5_lambda_im'][l].reshape(-1))
    lam = lam.at[2].set(jnp.repeat(p['s5_log_step'][l], S5_STATE))
    gbias = jnp.zeros((1, LANES), F32).at[0, :2 * ML_HEADS].set(p['ml_gate_bias'][l])
    w_in = jnp.pad(p['w_in'][l], ((0, 0), (0, N_IN_PAD - N_IN))).astype(BF16)
    return dict(
        norm1_g=p['norm1_g'][l].reshape(1, -1), w_in=w_in, s5_lam=lam,
        s5_bre=_blockdiag(p['s5_b_re'][l].transpose(0, 2, 1)), s5_bim=_blockdiag(p['s5_b_im'][l].transpose(0, 2, 1)),
        s5_cre=_blockdiag(p['s5_c_re'][l].transpose(0, 2, 1)).astype(BF16),
        s5_cim=_blockdiag(p['s5_c_im'][l].transpose(0, 2, 1)).astype(BF16),
        s5_d=p['s5_d'][l].reshape(1, -1), s5_gw=p['s5_glu_w'][l].astype(BF16), s5_gb=p['s5_glu_b'][l].reshape(1, -1),
        rg_cw=p['rg_conv_w'][l], rg_cb=p['rg_conv_b'][l].reshape(1, -1),
        rg_wa=_blockdiag(p['rg_wa'][l]).astype(BF16), rg_ba=p['rg_ba'][l].reshape(1, -1),
        rg_wx=_blockdiag(p['rg_wx'][l]).astype(BF16), rg_bx=p['rg_bx'][l].reshape(1, -1),
        rg_lam=p['rg_lambda'][l].reshape(1, -1),
        ml_gb=gbias, ml_ng=p['ml_norm_g'][l].reshape(1, -1),
        w_out=p['w_out'][l].astype(BF16), norm2_g=p['norm2_g'][l].reshape(1, -1),
        router_w=p['router_w'][l].T.astype(BF16), router_b=p['router_b'][l].reshape(-1, 1),
    )


def _mix_and_project(l, lp, xf, states, cfg):
    nb, t, t_valid = cfg['nb'], cfg['t'], cfg['t_valid']
    n = nb * t
    s5r, s5i, rgh, rgc, mlc, mln, mlm = states
    zs5, zrg, zml, zgt = _inproj(xf, lp['norm1_g'], lp['w_in'])
    y1, nr, ni = _s5(zs5.reshape(nb, t, -1), s5r[l].reshape(nb, -1), s5i[l].reshape(nb, -1), lp['s5_lam'],
                     lp['s5_bre'], lp['s5_bim'], lp['s5_cre'], lp['s5_cim'], lp['s5_d'], lp['s5_gw'], lp['s5_gb'],
                     tc=cfg['s5_tc'], t_valid=t_valid)
    y2, nh, ncv = _rglru(zrg.reshape(nb, t, -1), rgh[l], rgc[l], lp['rg_cw'], lp['rg_cb'], lp['rg_wa'],
                         lp['rg_ba'], lp['rg_wx'], lp['rg_bx'], lp['rg_lam'], tc=cfg['s5_tc'], t_valid=t_valid)
    y3, nc_, nn_, nm_ = _mlstm(l, zml.reshape(nb, t, -1), zgt.reshape(nb, t, -1), lp['ml_gb'], lp['ml_ng'],
                               mlc, mln[l], mlm[l].reshape(nb, 1, ML_HEADS),
                               bb=cfg['ml_bb'], cl=cfg['ml_cl'], t_valid=t_valid)
    xn, hn, ridx, rgt = _outproj(xf, y1.reshape(n, -1), y2.reshape(n, -1), y3.reshape(n, -1),
                                 lp['w_out'], lp['norm2_g'], lp['router_w'], lp['router_b'])
    new_states = (nr.reshape(nb, S5_GROUPS, S5_STATE), ni.reshape(nb, S5_GROUPS, S5_STATE), nh, ncv, nc_, nn_,
                  nm_.reshape(nb, ML_HEADS))
    return xn, hn, ridx, rgt, new_states


def _trunks(xs, states, cfgs, params, final_g):
    n_all = sum(x.shape[0] for x in xs)
    tok0 = [sum(x.shape[0] for x in xs[:s]) for s in range(len(xs))]
    outs = [[[] for _ in range(7)] for _ in xs]
    b1 = params['exp_b1'].reshape(DEPTH, N_EXPERTS, 1, -1)
    b2 = params['exp_b2'].reshape(DEPTH, N_EXPERTS, 1, -1)
    for l in range(DEPTH):
        lp = _layer_params(l, params)
        mixed = [_mix_and_project(l, lp, xs[s], states[s], cfgs[s]) for s in range(len(xs))]
        hn_all = jnp.concatenate([m[1] for m in mixed], axis=0)
        ridx_all = jnp.concatenate([m[2] for m in mixed], axis=1)
        block_e, block_v, slot_tok, slot_pos = _routing_tables(ridx_all, n_all)
        y4 = _moe(l, block_e, block_v, slot_tok, slot_pos, hn_all, params['exp_w1'], b1, params['exp_w2'], b2)
        xs = [_combine(mixed[s][0], y4, mixed[s][3], final_g.reshape(1, -1), tok0=tok0[s], n_all=n_all,
                       final=(l == DEPTH - 1)) for s in range(len(xs))]
        for s in range(len(xs)):
            for lst, val in zip(outs[s], mixed[s][4]):
                lst.append(val)
    return xs, [[jnp.stack(v) for v in o] for o in outs]


def kernel(x_prompt, x_sample, state_s5_re, state_s5_im, state_rg_h, state_rg_conv, state_ml_c, state_ml_n, state_ml_m, norm1_g, w_in, s5_lambda_re, s5_lambda_im, s5_log_step, s5_b_re, s5_b_im, s5_c_re, s5_c_im, s5_d, s5_glu_w, s5_glu_b, rg_conv_w, rg_conv_b, rg_wa, rg_ba, rg_wx, rg_bx, rg_lambda, ml_gate_bias, ml_norm_g, w_out, norm2_g, router_w, router_b, exp_w1, exp_b1, exp_w2, exp_b2, final_norm_g):
    params = dict(norm1_g=norm1_g, w_in=w_in, s5_lambda_re=s5_lambda_re, s5_lambda_im=s5_lambda_im,
                  s5_log_step=s5_log_step, s5_b_re=s5_b_re, s5_b_im=s5_b_im, s5_c_re=s5_c_re, s5_c_im=s5_c_im,
                  s5_d=s5_d, s5_glu_w=s5_glu_w, s5_glu_b=s5_glu_b, rg_conv_w=rg_conv_w, rg_conv_b=rg_conv_b,
                  rg_wa=rg_wa, rg_ba=rg_ba, rg_wx=rg_wx, rg_bx=rg_bx, rg_lambda=rg_lambda,
                  ml_gate_bias=ml_gate_bias, ml_norm_g=ml_norm_g, w_out=w_out, norm2_g=norm2_g,
                  router_w=router_w, router_b=router_b, exp_w1=exp_w1, exp_b1=exp_b1, exp_w2=exp_w2, exp_b2=exp_b2)
    bp, tp, _ = x_prompt.shape
    bs, ts, _ = x_sample.shape
    assert tp >= RG_CONV - 1 and ts >= RG_CONV - 1
    zeros = lambda *shape: jnp.zeros((DEPTH, bp) + shape, F32)
    prompt_states = (zeros(S5_GROUPS, S5_STATE), zeros(S5_GROUPS, S5_STATE), zeros(RG_WIDTH),
                     zeros(RG_CONV - 1, RG_WIDTH), zeros(ML_HEADS, ML_DK, ML_DV), zeros(ML_HEADS, ML_DK),
                     zeros(ML_HEADS))
    ts_pad = -(-ts // SUBLANES) * SUBLANES
    xs_pad = jnp.pad(x_sample, ((0, 0), (0, ts_pad - ts), (0, 0)))
    sample_states = (state_s5_re, state_s5_im, state_rg_h, state_rg_conv, state_ml_c, state_ml_n, state_ml_m)
    cfgs = [dict(nb=bp, t=tp, t_valid=tp, s5_tc=128, ml_bb=8, ml_cl=math.gcd(tp, ML_CHUNK_LEN)),
            dict(nb=bs, t=ts_pad, t_valid=ts, s5_tc=ts_pad, ml_bb=8, ml_cl=ts_pad)]
    (yp, ys), (sp, ss) = _trunks([x_prompt.reshape(bp * tp, D_MODEL), xs_pad.reshape(bs * ts_pad, D_MODEL)],
                                 [prompt_states, sample_states], cfgs, params, final_norm_g)
    return (yp.reshape(bp, tp, D_MODEL), ys.reshape(bs, ts_pad, D_MODEL)[:, :ts], *sp, *ss)
```

```python
import functools
import math

import jax
import jax.numpy as jnp
from jax import lax
from jax.experimental import pallas as pl
from jax.experimental.pallas import tpu as pltpu

F32 = jnp.float32
BF16 = jnp.bfloat16

SUBLANES = 8
LANES = 128
VMEM_LIMIT_BYTES = 56 * 1024 * 1024

D_MODEL = 1024
DEPTH = 2
S5_WIDTH = 256
S5_GROUP = 16
S5_GROUPS = 16
S5_STATE = 64
S5_FLAT = S5_GROUPS * S5_STATE
RG_WIDTH = 256
RG_BLOCKS = 8
RG_CONV = 4
RG_C = 8.0
ML_WIDTH = 512
ML_HEADS = 4
ML_DK = 128
ML_DV = 128
ML_CHUNK_LEN = 128
N_EXPERTS = 32
TOP_K = 4
D_FF = 1024
SWIGLU_LIMIT = 7.0
SWIGLU_ALPHA = 1.702
EPS = 1e-5

IN_S5 = (0, 256)
IN_RG = (256, 768)
IN_ML = (768, 2816)
IN_GATE = (2816, 2944)
N_IN = 2824
N_IN_PAD = 2944

ROW_TILE = 512
ROUTE_TILE = 128
MOE_TILE = 256
NEG_BIG = -1e30


def _cparams(sem):
    return pltpu.CompilerParams(dimension_semantics=sem, vmem_limit_bytes=VMEM_LIMIT_BYTES)


def _full(shape):
    n = len(shape)
    return pl.BlockSpec(shape, lambda *_: (0,) * n)


ROW_TILES = D_MODEL // LANES
assert ROW_TILES == SUBLANES


def _rows_to_tiles(x, dst_ref):
    m = x.shape[0]
    for j in range(ROW_TILES):
        dst_ref[pl.ds(j, m, stride=ROW_TILES), :] = x[:, j * LANES:(j + 1) * LANES]


def _tiles_to_rows(src_ref):
    m = src_ref.shape[0] // ROW_TILES
    return jnp.concatenate([src_ref[pl.ds(j, m, stride=ROW_TILES), :] for j in range(ROW_TILES)], axis=-1)


def _inproj_body(x_ref, g_ref, w_ref, zs5_ref, zrg_ref, zml_ref, zgt_ref):
    x = x_ref[...]
    h = x * lax.rsqrt(jnp.mean(x * x, axis=-1, keepdims=True) + EPS) * g_ref[...]
    hb = h.astype(BF16)
    for ref, (lo, hi) in ((zs5_ref, IN_S5), (zrg_ref, IN_RG), (zml_ref, IN_ML), (zgt_ref, IN_GATE)):
        ref[...] = jnp.dot(hb, w_ref[:, lo:hi], preferred_element_type=F32)


def _inproj(x, g, w):
    n = x.shape[0]
    tm = ROW_TILE
    widths = [hi - lo for lo, hi in (IN_S5, IN_RG, IN_ML, IN_GATE)]
    return pl.pallas_call(
        _inproj_body,
        grid=(n // tm,),
        in_specs=[pl.BlockSpec((tm, D_MODEL), lambda i: (i, 0)), _full((1, D_MODEL)), _full((D_MODEL, N_IN_PAD))],
        out_specs=[pl.BlockSpec((tm, w_), lambda i: (i, 0)) for w_ in widths],
        out_shape=[jax.ShapeDtypeStruct((n, w_), F32) for w_ in widths],
        compiler_params=_cparams(("arbitrary",)),
        name="inproj",
    )(x, g, w)


def _s5_body(u_ref, s0r_ref, s0i_ref, lam_ref, bre_ref, bim_ref, cre_ref, cim_ref, d_ref, gw_ref, gb_ref,
             y_ref, sr_ref, si_ref, bur, bui, ab, bbr, bbi, tm_s, bm_s, *, nb, tc, nc, t_valid):
    c = pl.program_id(0)

    @pl.when(c == 0)
    def _():
        sr_ref[...] = s0r_ref[...]
        si_ref[...] = s0i_ref[...]
        lr = lam_ref[0:1, :]
        li = lam_ref[1:2, :]
        step = jnp.exp(lam_ref[2:3, :])
        mag = jnp.exp(lr * step)
        ab_re = mag * jnp.cos(li * step)
        ab_im = mag * jnp.sin(li * step)
        den = lr * lr + li * li
        num_re = ab_re - 1.0
        z_re = (num_re * lr + ab_im * li) / den
        z_im = (ab_im * lr - num_re * li) / den
        ab[0:1, :] = ab_re
        ab[1:2, :] = ab_im
        bbr[...] = (z_re * bre_ref[...] - z_im * bim_ref[...]).astype(BF16)
        bbi[...] = (z_re * bim_ref[...] + z_im * bre_ref[...]).astype(BF16)

    _to_time_major(u_ref, tm_s, bm_s, nb, tc)
    u = jnp.concatenate([tm_s[j] for j in range(S5_WIDTH // LANES)], axis=-1)
    ub = u.astype(BF16)
    bur[...] = jnp.dot(ub, bbr[...], preferred_element_type=F32)
    bui[...] = jnp.dot(ub, bbi[...], preferred_element_type=F32)

    a_re = jnp.broadcast_to(ab[0:1, :], (SUBLANES, S5_FLAT))
    a_im = jnp.broadcast_to(ab[1:2, :], (SUBLANES, S5_FLAT))
    last_in = t_valid - (nc - 1) * tc
    steps = tc if last_in == tc else jnp.where(c == nc - 1, last_in, tc)

    def group(gi, _):
        g8 = gi * SUBLANES
        rows8 = pl.ds(pl.multiple_of(g8, SUBLANES), SUBLANES)

        def step_fn(t, carry):
            s_re, s_im = carry
            rows = pl.ds(pl.multiple_of(t * nb + g8, SUBLANES), SUBLANES)
            n_re = a_re * s_re - a_im * s_im + bur[rows, :]
            n_im = a_re * s_im + a_im * s_re + bui[rows, :]
            bur[rows, :] = n_re
            bui[rows, :] = n_im
            return n_re, n_im

        s_re, s_im = lax.fori_loop(0, steps, step_fn, (sr_ref[rows8, :], si_ref[rows8, :]))
        sr_ref[rows8, :] = s_re
        si_ref[rows8, :] = s_im
        return 0

    lax.fori_loop(0, nb // SUBLANES, group, 0)

    y = (jnp.dot(bur[...].astype(BF16), cre_ref[...], preferred_element_type=F32)
         - jnp.dot(bui[...].astype(BF16), cim_ref[...], preferred_element_type=F32))
    y = y + d_ref[...] * u
    g = jax.nn.gelu(y)
    gate = jax.nn.sigmoid(jnp.dot(g.astype(BF16), gw_ref[...], preferred_element_type=F32) + gb_ref[...])
    out = g * gate
    for j in range(S5_WIDTH // LANES):
        tm_s[j] = out[:, j * LANES:(j + 1) * LANES]
    _to_batch_major(tm_s, bm_s, y_ref, nb, tc)


def _to_time_major(src_ref, tm_s, bm_s, nb, tc):
    for j in range(tm_s.shape[0]):
        lanes = slice(j * LANES, (j + 1) * LANES)
        if nb <= tc:
            for b in range(nb):
                tm_s[j, pl.ds(b, tc, stride=nb), :] = src_ref[b, :, lanes]
        else:
            bm_s[j] = src_ref[:, :, lanes].reshape(nb * tc, LANES)
            for t in range(tc):
                tm_s[j, t * nb:(t + 1) * nb, :] = bm_s[j, pl.ds(t, nb, stride=tc), :]


def _to_batch_major(tm_s, bm_s, dst_ref, nb, tc):
    for j in range(tm_s.shape[0]):
        lanes = slice(j * LANES, (j + 1) * LANES)
        if nb <= tc:
            for b in range(nb):
                dst_ref[b, :, lanes] = tm_s[j, pl.ds(b, tc, stride=nb), :]
        else:
            for t in range(tc):
                bm_s[j, pl.ds(t, nb, stride=tc), :] = tm_s[j, t * nb:(t + 1) * nb, :]
            dst_ref[:, :, lanes] = bm_s[j].reshape(nb, tc, LANES)


def _s5(u, s0r, s0i, lam, bre, bim, cre, cim, d, gw, gb, *, tc, t_valid):
    nb, t, _ = u.shape
    nc = t // tc
    assert (nc - 1) * tc < t_valid <= t
    body = functools.partial(_s5_body, nb=nb, tc=tc, nc=nc, t_valid=t_valid)
    return pl.pallas_call(
        body,
        grid=(t // tc,),
        in_specs=[pl.BlockSpec((nb, tc, S5_WIDTH), lambda c: (0, c, 0)),
                  _full((nb, S5_FLAT)), _full((nb, S5_FLAT)), _full((SUBLANES, S5_FLAT)),
                  _full((S5_WIDTH, S5_FLAT)), _full((S5_WIDTH, S5_FLAT)),
                  _full((S5_FLAT, S5_WIDTH)), _full((S5_FLAT, S5_WIDTH)),
                  _full((1, S5_WIDTH)), _full((S5_WIDTH, S5_WIDTH)), _full((1, S5_WIDTH))],
        out_specs=[pl.BlockSpec((nb, tc, S5_WIDTH), lambda c: (0, c, 0)),
                   _full((nb, S5_FLAT)), _full((nb, S5_FLAT))],
        out_shape=[jax.ShapeDtypeStruct((nb, t, S5_WIDTH), F32),
                   jax.ShapeDtypeStruct((nb, S5_FLAT), F32), jax.ShapeDtypeStruct((nb, S5_FLAT), F32)],
        scratch_shapes=[pltpu.VMEM((nb * tc, S5_FLAT), F32), pltpu.VMEM((nb * tc, S5_FLAT), F32),
                        pltpu.VMEM((SUBLANES, S5_FLAT), F32),
                        pltpu.VMEM((S5_WIDTH, S5_FLAT), BF16), pltpu.VMEM((S5_WIDTH, S5_FLAT), BF16),
                        pltpu.VMEM((S5_WIDTH // LANES, nb * tc, LANES), F32),
                        pltpu.VMEM((S5_WIDTH // LANES, nb * tc, LANES), F32)],
        compiler_params=_cparams(("arbitrary",)),
        name="s5_scan",
    )(u, s0r, s0i, lam, bre, bim, cre, cim, d, gw, gb)


def _rglru_body(z_ref, h0_ref, cb0_ref, cw_ref, cb_ref, wa_ref, ba_ref, wx_ref, bx_ref, lam_ref,
                y_ref, h_ref, cbn_ref, xs, a_s, b_s, tm_s, bm_s, *, nb, tc, nc, t_valid):
    c = pl.program_id(0)
    hist = RG_CONV - 1

    @pl.when(c == 0)
    def _():
        h_ref[...] = h0_ref[...]
        xs[0:hist * nb, :] = cb0_ref[...]

    _to_time_major(z_ref, tm_s, bm_s, nb, tc)
    xs[hist * nb:(hist + tc) * nb, :] = jnp.concatenate([tm_s[j] for j in range(RG_WIDTH // LANES)], axis=-1)

    xc = jnp.zeros((tc * nb, RG_WIDTH), F32) + cb_ref[...]
    for j in range(RG_CONV):
        xc = xc + xs[j * nb:(j + tc) * nb, :] * cw_ref[j:j + 1, :]

    last_in = t_valid - (nc - 1) * tc
    last = tc if last_in == tc else jnp.where(c == nc - 1, last_in, tc)

    @pl.when(c == nc - 1)
    def _():
        cbn_ref[...] = xs[last_in * nb:(last_in + hist) * nb, :]

    xs[0:hist * nb, :] = xs[tc * nb:(tc + hist) * nb, :]

    xb = xc.astype(BF16)
    r = jax.nn.sigmoid(jnp.dot(xb, wa_ref[...], preferred_element_type=F32) + ba_ref[...])
    i = jax.nn.sigmoid(jnp.dot(xb, wx_ref[...], preferred_element_type=F32) + bx_ref[...])
    log_a = -RG_C * r * jax.nn.softplus(-lam_ref[...])
    a_s[...] = jnp.exp(log_a)
    th = jnp.tanh(log_a)
    b_s[...] = jnp.sqrt(-2.0 * th / (1.0 - th)) * (i * xc)

    def group(gi, _):
        g8 = gi * SUBLANES
        rows8 = pl.ds(pl.multiple_of(g8, SUBLANES), SUBLANES)

        def step_fn(t, h):
            rows = pl.ds(pl.multiple_of(t * nb + g8, SUBLANES), SUBLANES)
            h = a_s[rows, :] * h + b_s[rows, :]
            b_s[rows, :] = h
            return h

        h_ref[rows8, :] = lax.fori_loop(0, last, step_fn, h_ref[rows8, :])
        return 0

    lax.fori_loop(0, nb // SUBLANES, group, 0)
    for j in range(RG_WIDTH // LANES):
        tm_s[j] = b_s[:, j * LANES:(j + 1) * LANES]
    _to_batch_major(tm_s, bm_s, y_ref, nb, tc)
    y_ref[...] = y_ref[...] * jax.nn.gelu(z_ref[:, :, RG_WIDTH:2 * RG_WIDTH])


def _rglru(z, h0, cb0, cw, cb, wa, ba, wx, bx, lam, *, tc, t_valid):
    nb, t, _ = z.shape
    hist = RG_CONV - 1
    nc = t // tc
    assert (nc - 1) * tc + hist <= t_valid <= t
    body = functools.partial(_rglru_body, nb=nb, tc=tc, nc=nc, t_valid=t_valid)
    y, h, cbn = pl.pallas_call(
        body,
        grid=(t // tc,),
        in_specs=[pl.BlockSpec((nb, tc, 2 * RG_WIDTH), lambda c: (0, c, 0)),
                  _full((nb, RG_WIDTH)), _full((hist * nb, RG_WIDTH)),
                  _full((RG_CONV, RG_WIDTH)), _full((1, RG_WIDTH)),
                  _full((RG_WIDTH, RG_WIDTH)), _full((1, RG_WIDTH)),
                  _full((RG_WIDTH, RG_WIDTH)), _full((1, RG_WIDTH)), _full((1, RG_WIDTH))],
        out_specs=[pl.BlockSpec((nb, tc, RG_WIDTH), lambda c: (0, c, 0)),
                   _full((nb, RG_WIDTH)), _full((hist * nb, RG_WIDTH))],
        out_shape=[jax.ShapeDtypeStruct((nb, t, RG_WIDTH), F32),
                   jax.ShapeDtypeStruct((nb, RG_WIDTH), F32), jax.ShapeDtypeStruct((hist * nb, RG_WIDTH), F32)],
        scratch_shapes=[pltpu.VMEM(((tc + hist) * nb, RG_WIDTH), F32),
                        pltpu.VMEM((nb * tc, RG_WIDTH), F32), pltpu.VMEM((nb * tc, RG_WIDTH), F32),
                        pltpu.VMEM((RG_WIDTH // LANES, nb * tc, LANES), F32),
                        pltpu.VMEM((RG_WIDTH // LANES, nb * tc, LANES), F32)],
        compiler_params=_cparams(("arbitrary",)),
        name="rglru_scan",
    )(z, h0, cb0.transpose(1, 0, 2).reshape(hist * nb, RG_WIDTH), cw, cb, wa, ba, wx, bx, lam)
    return y, h, cbn.reshape(hist, nb, RG_WIDTH).transpose(1, 0, 2)


def _dot_nt(a, b, **kw):
    return lax.dot_general(a, b, (((1,), (1,)), ((), ())), preferred_element_type=F32, **kw)


def _dot_tn(a, b, **kw):
    return lax.dot_general(a, b, (((0,), (0,)), ((), ())), preferred_element_type=F32, **kw)


def _mlstm_body(q_ref, k_ref, v_ref, o_ref, g_ref, gb_ref, ng_ref, c0_ref, n0_ref, m0_ref,
                y_ref, cn_ref, nn_ref, mn_ref, *, bb, cl, t_valid):
    c = pl.program_id(1)

    @pl.when(c == 0)
    def _():
        cn_ref[...] = c0_ref[0]
        nn_ref[...] = n0_ref[...]
        mn_ref[...] = m0_ref[...]

    row = lax.broadcasted_iota(jnp.int32, (cl, cl), 0)
    col = lax.broadcasted_iota(jnp.int32, (cl, cl), 1)
    tril = (col <= row).astype(F32)
    keep = (col <= row) & (col + c * cl < t_valid)
    t_ok = (lax.broadcasted_iota(jnp.int32, (cl, LANES), 0) + c * cl) < t_valid
    lane = lax.broadcasted_iota(jnp.int32, (cl, LANES), 1)
    ones = jnp.ones((cl, LANES), F32)

    chains = [(bi, h) for bi in range(bb) for h in range(ML_HEADS)]
    hsl = lambda h: slice(h * ML_DK, (h + 1) * ML_DK)
    hi = lax.Precision.HIGHEST

    bcum, ig = [], []
    for bi in range(bb):
        graw = g_ref[bi] + gb_ref[...]
        lf = jnp.where(t_ok, jax.nn.log_sigmoid(graw), 0.0)
        bcum.append(jnp.dot(tril, lf, preferred_element_type=F32, precision=hi))
        ig.append(jnp.where(t_ok, graw, NEG_BIG))

    qf = {ch: q_ref[ch[0], :, hsl(ch[1])] for ch in chains}
    qh = {ch: qf[ch].astype(BF16) for ch in chains}
    kh = {ch: k_ref[ch[0], :, hsl(ch[1])] * (ML_DK ** -0.5) for ch in chains}
    vh = {ch: v_ref[ch[0], :, hsl(ch[1])].astype(BF16) for ch in chains}
    cm = {ch: cn_ref[ch[0], ch[1]] for ch in chains}
    nv = {ch: nn_ref[ch[0], ch[1]:ch[1] + 1, :] for ch in chains}
    m_old = {ch: mn_ref[ch[0], :, ch[1]:ch[1] + 1] for ch in chains}
    b_col = {ch: bcum[ch[0]][:, ML_HEADS + ch[1]:ML_HEADS + ch[1] + 1] for ch in chains}
    i_col = {ch: ig[ch[0]][:, ch[1]:ch[1] + 1] for ch in chains}

    w_row = {ch: _dot_nt(ones, jnp.where(lane == ch[1], i_col[ch] - b_col[ch], 0.0), precision=hi) for ch in chains}
    qk = {ch: _dot_nt(qh[ch], kh[ch].astype(BF16)) for ch in chains}
    qc = {ch: jnp.dot(qh[ch], cm[ch].astype(BF16), preferred_element_type=F32) for ch in chains}
    dmat = {ch: jnp.where(keep, b_col[ch] + w_row[ch], -jnp.inf) for ch in chains}
    dmax = {ch: jnp.max(dmat[ch], axis=-1, keepdims=True) for ch in chains}
    inter = {ch: b_col[ch] + m_old[ch] for ch in chains}
    mt = {ch: jnp.maximum(inter[ch], dmax[ch]) for ch in chains}
    s = {ch: qk[ch] * jnp.exp(dmat[ch] - mt[ch]) for ch in chains}
    sc = {ch: jnp.exp(inter[ch] - mt[ch]) for ch in chains}
    sv = {ch: jnp.dot(s[ch].astype(BF16), vh[ch], preferred_element_type=F32) for ch in chains}
    qn = {ch: jnp.sum(qf[ch] * nv[ch], axis=-1, keepdims=True) for ch in chains}
    den = {ch: jnp.sum(s[ch], axis=-1, keepdims=True) + sc[ch] * qn[ch] for ch in chains}
    hh = {ch: (sv[ch] + sc[ch] * qc[ch]) / jnp.maximum(jnp.abs(den[ch]), jnp.exp(-mt[ch])) for ch in chains}
    bl = {ch: b_col[ch][cl - 1:cl, :] for ch in chains}
    m_new = {ch: mt[ch][cl - 1:cl, :] for ch in chains}
    wk = {ch: jnp.exp(bl[ch] - b_col[ch] + i_col[ch] - m_new[ch]) * kh[ch] for ch in chains}
    decay = {ch: jnp.exp(bl[ch] + m_old[ch] - m_new[ch]) for ch in chains}
    kv = {ch: _dot_tn(wk[ch].astype(BF16), vh[ch]) for ch in chains}
    rms = {ch: lax.rsqrt(jnp.mean(hh[ch] * hh[ch], axis=-1, keepdims=True) + EPS) for ch in chains}
    for ch in chains:
        bi, h = ch
        cn_ref[bi, h] = decay[ch] * cm[ch] + kv[ch]
        nn_ref[bi, h:h + 1, :] = decay[ch] * nv[ch] + jnp.sum(wk[ch], axis=0, keepdims=True)
        mn_ref[bi, :, h:h + 1] = m_new[ch]
        y_ref[bi, :, hsl(h)] = hh[ch] * rms[ch] * ng_ref[:, hsl(h)] * jax.nn.sigmoid(o_ref[bi, :, hsl(h)])


def _mlstm(layer, zml, zgt, gbias, ng, c0, n0, m0, *, bb, cl, t_valid):
    nb, t, _ = zml.shape
    nc = t // cl
    body = functools.partial(_mlstm_body, bb=bb, cl=cl, t_valid=t_valid)
    zspec = lambda j: pl.BlockSpec((bb, cl, ML_WIDTH), lambda i, c, j=j: (i, c, j))
    return pl.pallas_call(
        body,
        grid=(nb // bb, nc),
        in_specs=[zspec(0), zspec(1), zspec(2), zspec(3),
                  pl.BlockSpec((bb, cl, LANES), lambda i, c: (i, c, 0)),
                  _full((1, LANES)), _full((1, ML_WIDTH)),
                  pl.BlockSpec((1, bb, ML_HEADS, ML_DK, ML_DV), lambda i, c: (layer, i, 0, 0, 0)),
                  pl.BlockSpec((bb, ML_HEADS, ML_DK), lambda i, c: (i, 0, 0)),
                  pl.BlockSpec((bb, 1, ML_HEADS), lambda i, c: (i, 0, 0))],
        out_specs=[pl.BlockSpec((bb, cl, ML_WIDTH), lambda i, c: (i, c, 0)),
                   pl.BlockSpec((bb, ML_HEADS, ML_DK, ML_DV), lambda i, c: (i, 0, 0, 0)),
                   pl.BlockSpec((bb, ML_HEADS, ML_DK), lambda i, c: (i, 0, 0)),
                   pl.BlockSpec((bb, 1, ML_HEADS), lambda i, c: (i, 0, 0))],
        out_shape=[jax.ShapeDtypeStruct((nb, t, ML_WIDTH), F32),
                   jax.ShapeDtypeStruct((nb, ML_HEADS, ML_DK, ML_DV), F32),
                   jax.ShapeDtypeStruct((nb, ML_HEADS, ML_DK), F32),
                   jax.ShapeDtypeStruct((nb, 1, ML_HEADS), F32)],
        compiler_params=_cparams(("arbitrary", "arbitrary")),
        name="mlstm_chunks",
    )(zml, zml, zml, zml, zgt, gbias, ng, c0, n0, m0)


def _outproj_body(x_ref, y1_ref, y2_ref, y3_ref, wo_ref, g2_ref, rw_ref, rb_ref,
                  xn_ref, hn_ref, ridx_ref, rgt_ref, cnt_ref):
    acc = jnp.dot(y1_ref[...].astype(BF16), wo_ref[0:256, :], preferred_element_type=F32)
    acc = acc + jnp.dot(y2_ref[...].astype(BF16), wo_ref[256:512, :], preferred_element_type=F32)
    acc = acc + jnp.dot(y3_ref[...].astype(BF16), wo_ref[512:1024, :], preferred_element_type=F32)
    x = x_ref[...] + acc
    xn_ref[...] = x
    h = x * lax.rsqrt(jnp.mean(x * x, axis=-1, keepdims=True) + EPS) * g2_ref[...]
    _rows_to_tiles(h, hn_ref)
    hb = h.astype(BF16)
    tm = x.shape[0]
    eidx = lax.broadcasted_iota(jnp.int32, (N_EXPERTS, ROUTE_TILE), 0)
    sub = lax.broadcasted_iota(jnp.int32, (LANES, ROUTE_TILE), 0)
    cnt = jnp.zeros((N_EXPERTS, 1), F32)
    for j in range(tm // ROUTE_TILE):
        rows = slice(j * ROUTE_TILE, (j + 1) * ROUTE_TILE)
        vals = _dot_nt(rw_ref[...], hb[rows, :]) + rb_ref[...]
        tops, idxs = [], []
        for _ in range(TOP_K):
            m = jnp.max(vals, axis=0, keepdims=True)
            idx = jnp.min(jnp.where(vals == m, eidx, N_EXPERTS), axis=0, keepdims=True)
            vals = jnp.where(eidx == idx, -jnp.inf, vals)
            tops.append(m)
            idxs.append(idx)
        exps = [jnp.exp(t - tops[0]) for t in tops]
        inv = 1.0 / functools.reduce(lambda a, b: a + b, exps)
        ridx_ref[:, rows] = jnp.concatenate(idxs + idxs, axis=0)
        for idx in idxs:
            cnt = cnt + jnp.sum((eidx == idx).astype(F32), axis=1, keepdims=True)
        gt = jnp.zeros((LANES, ROUTE_TILE), F32)
        for kk in range(TOP_K):
            gt = jnp.where(sub == kk, exps[kk] * inv, gt)
        rgt_ref[rows, :] = gt.T
    cnt_ref[...] = jnp.broadcast_to(cnt, (N_EXPERTS, LANES))


def _outproj(x, y1, y2, y3, wo, g2, rw, rb):
    n = x.shape[0]
    tm = ROW_TILE
    rowspec = lambda w_: pl.BlockSpec((tm, w_), lambda i: (i, 0))
    return pl.pallas_call(
        _outproj_body,
        grid=(n // tm,),
        in_specs=[rowspec(D_MODEL), rowspec(S5_WIDTH), rowspec(RG_WIDTH), rowspec(ML_WIDTH),
                  _full((D_MODEL, D_MODEL)), _full((1, D_MODEL)),
                  _full((N_EXPERTS, D_MODEL)), _full((N_EXPERTS, 1))],
        out_specs=[rowspec(D_MODEL), pl.BlockSpec((tm * ROW_TILES, LANES), lambda i: (i, 0)),
                   pl.BlockSpec((2 * TOP_K, tm), lambda i: (0, i)), rowspec(LANES),
                   pl.BlockSpec((N_EXPERTS, LANES), lambda i: (0, i))],
        out_shape=[jax.ShapeDtypeStruct((n, D_MODEL), F32), jax.ShapeDtypeStruct((n * ROW_TILES, LANES), F32),
                   jax.ShapeDtypeStruct((2 * TOP_K, n), jnp.int32), jax.ShapeDtypeStruct((n, LANES), F32),
                   jax.ShapeDtypeStruct((N_EXPERTS, n // tm * LANES), F32)],
        compiler_params=_cparams(("arbitrary",)),
        name="outproj_route",
    )(x, y1, y2, y3, wo, g2, rw, rb)


GATHER_AHEAD = 2
N_XBUF = GATHER_AHEAD + 1
N_OBUF = 3


def _moe_body(be_ref, bv_ref, tokn_ref, posp_ref, posc_ref, hn_ref, w1_ref, b1_ref, w2_ref, b2_ref, y4_ref,
              w1b, w2b, xbuf, obuf, gsem, ssem):
    i = pl.program_id(0)
    last = pl.num_programs(0) - 1
    real = bv_ref[i] > 0
    fresh = jnp.logical_or(i == GATHER_AHEAD, be_ref[i] != be_ref[jnp.maximum(i - 1, 0)])
    xs, xn = lax.rem(i, N_XBUF), lax.rem(i + GATHER_AHEAD, N_XBUF)
    os_, op, opp = lax.rem(i, N_OBUF), lax.rem(i + N_OBUF - 1, N_OBUF), lax.rem(i + N_OBUF - 2, N_OBUF)
    x_cur, x_nxt, o_cur, o_prev = xbuf.at[xs], xbuf.at[xn], obuf.at[os_], obuf.at[op]

    def tile(ref, row):
        return ref.at[pl.ds(pl.multiple_of(row, ROW_TILES), ROW_TILES)]

    def start_gathers(rows=range(MOE_TILE)):
        for r in rows:
            pltpu.make_async_copy(tile(hn_ref, tokn_ref[0, 0, r]), tile(x_nxt, r * ROW_TILES),
                                  gsem.at[xn]).start(priority=0)

    def start_scatters(pos_ref, o_buf, sem, rows=range(MOE_TILE)):
        for r in rows:
            pltpu.make_async_copy(tile(o_buf, r * ROW_TILES), tile(y4_ref, pos_ref[0, 0, r]), sem).start(priority=1)

    def wait_scatters(o_buf, sem):
        pltpu.make_async_copy(o_buf, y4_ref.at[pl.ds(0, MOE_TILE * ROW_TILES)], sem).wait()

    gathered = jnp.logical_and(i >= GATHER_AHEAD, jnp.logical_or(
        i < 2 * GATHER_AHEAD, bv_ref[jnp.maximum(i - GATHER_AHEAD, 0)] > 0))

    @pl.when(gathered)
    def _():
        pltpu.make_async_copy(hn_ref.at[pl.ds(0, MOE_TILE * ROW_TILES)], x_cur, gsem.at[xs]).wait()

    @pl.when(i >= N_OBUF)
    def _():
        wait_scatters(o_cur, ssem.at[os_])

    @pl.when(jnp.logical_and(fresh, real))
    def _():
        w1b[...] = w1_ref[0, 0].astype(BF16)
        w2b[...] = w2_ref[0, 0].astype(BF16)

    @pl.when(bv_ref[i] >= 1)
    def _():
        start_gathers()

    @pl.when(real)
    def _():
        xb = _tiles_to_rows(x_cur).astype(BF16)
        start_scatters(posp_ref, o_prev, ssem.at[op])
        hb = jnp.dot(xb, w1b[...], preferred_element_type=F32) + b1_ref[0, 0]
        g = jnp.minimum(hb[:, :D_FF], SWIGLU_LIMIT)
        u = jnp.clip(hb[:, D_FF:], -SWIGLU_LIMIT, SWIGLU_LIMIT)
        act = g * jax.nn.sigmoid(SWIGLU_ALPHA * g) * (u + 1.0)
        _rows_to_tiles(jnp.dot(act.astype(BF16), w2b[...], preferred_element_type=F32) + b2_ref[0, 0], o_cur)

    @pl.when(jnp.logical_not(real))
    def _():
        @pl.when(i == 0)
        def _():
            obuf[...] = jnp.zeros_like(obuf)

        @pl.when(i < GATHER_AHEAD)
        def _():
            start_gathers()

        @pl.when(i >= 1)
        def _():
            start_scatters(posp_ref, o_prev, ssem.at[op])

    @pl.when(i == last)
    def _():
        start_scatters(posc_ref, o_cur, ssem.at[os_])
        wait_scatters(obuf.at[opp], ssem.at[opp])
        wait_scatters(o_prev, ssem.at[op])
        wait_scatters(o_cur, ssem.at[os_])


def _moe(layer, block_e, block_v, slot_tok, slot_pos, hn, w1, b1, w2, b2):
    n_blocks = block_e.shape[0]
    tm = MOE_TILE
    n_rows = n_blocks * tm
    assert n_blocks > 2 * GATHER_AHEAD
    idx_spec = lambda f: pl.BlockSpec((1, 1, tm), f, memory_space=pltpu.SMEM)
    grid_spec = pltpu.PrefetchScalarGridSpec(
        num_scalar_prefetch=2,
        grid=(n_blocks,),
        in_specs=[idx_spec(lambda i, be, bv: (jnp.minimum(i + GATHER_AHEAD, n_blocks - 1), 0, 0)),
                  idx_spec(lambda i, be, bv: (jnp.maximum(i - 1, 0), 0, 0)),
                  idx_spec(lambda i, be, bv: (i, 0, 0)),
                  pl.BlockSpec(memory_space=pl.ANY),
                  pl.BlockSpec((1, 1, D_MODEL, 2 * D_FF), lambda i, be, bv: (layer, be[i], 0, 0)),
                  pl.BlockSpec((1, 1, 1, 2 * D_FF), lambda i, be, bv: (layer, be[i], 0, 0)),
                  pl.BlockSpec((1, 1, D_FF, D_MODEL), lambda i, be, bv: (layer, be[i], 0, 0)),
                  pl.BlockSpec((1, 1, 1, D_MODEL), lambda i, be, bv: (layer, be[i], 0, 0))],
        out_specs=pl.BlockSpec(memory_space=pl.ANY),
        scratch_shapes=[pltpu.VMEM((D_MODEL, 2 * D_FF), BF16), pltpu.VMEM((D_FF, D_MODEL), BF16)]
                       + [pltpu.VMEM((N_XBUF, tm * ROW_TILES, LANES), F32),
                          pltpu.VMEM((N_OBUF, tm * ROW_TILES, LANES), F32)]
                       + [pltpu.SemaphoreType.DMA((N_XBUF,)), pltpu.SemaphoreType.DMA((N_OBUF,))],
    )
    return pl.pallas_call(
        _moe_body,
        grid_spec=grid_spec,
        out_shape=jax.ShapeDtypeStruct((n_rows * ROW_TILES, LANES), F32),
        compiler_params=_cparams(("arbitrary",)),
        name="moe_experts",
    )(block_e, block_v, slot_tok, slot_pos, slot_pos, hn, w1, b1, w2, b2)


def _combine_body(x_ref, rgt_ref, fg_ref, *rest, final):
    y_refs, o_ref = rest[:TOP_K], rest[TOP_K]
    x = x_ref[...]
    for kk in range(TOP_K):
        x = x + rgt_ref[:, kk:kk + 1] * _tiles_to_rows(y_refs[kk])
    if final:
        x = x * lax.rsqrt(jnp.mean(x * x, axis=-1, keepdims=True) + EPS) * fg_ref[...]
    o_ref[...] = x


def _combine(x, y4, rgt, fg, *, tok0, n_all, final):
    n = x.shape[0]
    tm = ROW_TILE
    assert tok0 % tm == 0 and n_all % tm == 0
    yspec = lambda kk: pl.BlockSpec((tm * ROW_TILES, LANES), lambda i, kk=kk: ((kk * n_all + tok0) // tm + i, 0))
    return pl.pallas_call(
        functools.partial(_combine_body, final=final),
        grid=(n // tm,),
        in_specs=[pl.BlockSpec((tm, D_MODEL), lambda i: (i, 0)),
                  pl.BlockSpec((tm, LANES), lambda i: (i, 0)), _full((1, D_MODEL))]
                 + [yspec(kk) for kk in range(TOP_K)],
        out_specs=pl.BlockSpec((tm, D_MODEL), lambda i: (i, 0)),
        out_shape=jax.ShapeDtypeStruct((n, D_MODEL), F32),
        compiler_params=_cparams(("arbitrary",)),
        name="moe_combine",
    )(x, rgt, fg, *([y4] * TOP_K))


def _routing_tables(ridx, n, counts):
    na = n * TOP_K
    tm = MOE_TILE
    assert na % tm == 0
    lead = GATHER_AHEAD * tm
    n_blocks = na // tm + N_EXPERTS + 2 * GATHER_AHEAD
    n_slots = n_blocks * tm
    experts = jnp.arange(N_EXPERTS, dtype=jnp.int32)
    flat_e = ridx[:TOP_K, :].reshape(-1)
    order = jnp.argsort(flat_e).astype(jnp.int32)
    padded = (counts + tm - 1) // tm * tm
    pad_end = jnp.cumsum(padded)
    pad_start = pad_end - padded
    start = jnp.cumsum(counts) - counts
    spare_start = na + pad_start - start
    total = pad_end[-1]
    blk = jnp.arange(n_blocks, dtype=jnp.int32) * tm - lead
    block_v = ((blk >= 0) & (blk < total)).astype(jnp.int32)
    last_e = jnp.max(jnp.where(counts > 0, experts, 0))
    block_e = jnp.minimum(jnp.sum(blk[:, None] >= pad_end[None, :], axis=1), last_e).astype(jnp.int32)
    rank = (blk - pad_start[block_e])[:, None] + jnp.arange(tm, dtype=jnp.int32)[None, :]
    cnt = counts[block_e][:, None]
    slot = blk[:, None] + jnp.arange(tm, dtype=jnp.int32)[None, :]
    used = (slot >= 0) & (slot < total)
    real = (rank < cnt) & used
    src = order[jnp.clip(start[block_e][:, None] + rank, 0, na - 1)]
    slot_tok = (jnp.where(real, src % n, 0) * ROW_TILES).astype(jnp.int32)
    spare = jnp.where(used, spare_start[block_e][:, None] + rank - cnt, jnp.where(slot >= 0, slot, n_slots + slot))
    slot_pos = (jnp.where(real, src, spare) * ROW_TILES).astype(jnp.int32)
    return block_e, block_v, slot_tok.reshape(n_blocks, 1, tm), slot_pos.reshape(n_blocks, 1, tm)


def _blockdiag(m):
    g, a, b = m.shape
    eye = jnp.eye(g, dtype=m.dtype)
    return (eye[:, None, :, None] * m[:, :, None, :]).reshape(g * a, g * b)


def _layer_params(l, p):
    lam = jnp.zeros((SUBLANES, S5_FLAT), F32)
    lam = lam.at[0].set(p['s5_lambda_re'][l].reshape(-1)).at[1].set(p['s5_lambda_im'][l].reshape(-1))
    lam = lam.at[2].set(jnp.repeat(p['s5_log_step'][l], S5_STATE))
    gbias = jnp.zeros((1, LANES), F32).at[0, :2 * ML_HEADS].set(p['ml_gate_bias'][l])
    w_in = jnp.pad(p['w_in'][l], ((0, 0), (0, N_IN_PAD - N_IN))).astype(BF16)
    return dict(
        norm1_g=p['norm1_g'][l].reshape(1, -1), w_in=w_in, s5_lam=lam,
        s5_bre=_blockdiag(p['s5_b_re'][l].transpose(0, 2, 1)), s5_bim=_blockdiag(p['s5_b_im'][l].transpose(0, 2, 1)),
        s5_cre=_blockdiag(p['s5_c_re'][l].transpose(0, 2, 1)).astype(BF16),
        s5_cim=_blockdiag(p['s5_c_im'][l].transpose(0, 2, 1)).astype(BF16),
        s5_d=p['s5_d'][l].reshape(1, -1), s5_gw=p['s5_glu_w'][l].astype(BF16), s5_gb=p['s5_glu_b'][l].reshape(1, -1),
        rg_cw=p['rg_conv_w'][l], rg_cb=p['rg_conv_b'][l].reshape(1, -1),
        rg_wa=_blockdiag(p['rg_wa'][l]).astype(BF16), rg_ba=p['rg_ba'][l].reshape(1, -1),
        rg_wx=_blockdiag(p['rg_wx'][l]).astype(BF16), rg_bx=p['rg_bx'][l].reshape(1, -1),
        rg_lam=p['rg_lambda'][l].reshape(1, -1),
        ml_gb=gbias, ml_ng=p['ml_norm_g'][l].reshape(1, -1),
        w_out=p['w_out'][l].astype(BF16), norm2_g=p['norm2_g'][l].reshape(1, -1),
        router_w=p['router_w'][l].T.astype(BF16), router_b=p['router_b'][l].reshape(-1, 1),
    )


def _mix_and_project(l, lp, xf, states, cfg):
    nb, t, t_valid = cfg['nb'], cfg['t'], cfg['t_valid']
    n = nb * t
    s5r, s5i, rgh, rgc, mlc, mln, mlm = states
    zs5, zrg, zml, zgt = _inproj(xf, lp['norm1_g'], lp['w_in'])
    y1, nr, ni = _s5(zs5.reshape(nb, t, -1), s5r[l].reshape(nb, -1), s5i[l].reshape(nb, -1), lp['s5_lam'],
                     lp['s5_bre'], lp['s5_bim'], lp['s5_cre'], lp['s5_cim'], lp['s5_d'], lp['s5_gw'], lp['s5_gb'],
                     tc=cfg['s5_tc'], t_valid=t_valid)
    y2, nh, ncv = _rglru(zrg.reshape(nb, t, -1), rgh[l], rgc[l], lp['rg_cw'], lp['rg_cb'], lp['rg_wa'],
                         lp['rg_ba'], lp['rg_wx'], lp['rg_bx'], lp['rg_lam'], tc=cfg['s5_tc'], t_valid=t_valid)
    y3, nc_, nn_, nm_ = _mlstm(l, zml.reshape(nb, t, -1), zgt.reshape(nb, t, -1), lp['ml_gb'], lp['ml_ng'],
                               mlc, mln[l], mlm[l].reshape(nb, 1, ML_HEADS),
                               bb=cfg['ml_bb'], cl=cfg['ml_cl'], t_valid=t_valid)
    xn, hn, ridx, rgt, cnt = _outproj(xf, y1.reshape(n, -1), y2.reshape(n, -1), y3.reshape(n, -1),
                                      lp['w_out'], lp['norm2_g'], lp['router_w'], lp['router_b'])
    new_states = (nr.reshape(nb, S5_GROUPS, S5_STATE), ni.reshape(nb, S5_GROUPS, S5_STATE), nh, ncv, nc_, nn_,
                  nm_.reshape(nb, ML_HEADS))
    return xn, hn, ridx, rgt, new_states, cnt


def _trunks(xs, states, cfgs, params, final_g):
    n_all = sum(x.shape[0] for x in xs)
    tok0 = [sum(x.shape[0] for x in xs[:s]) for s in range(len(xs))]
    outs = [[[] for _ in range(7)] for _ in xs]
    b1 = params['exp_b1'].reshape(DEPTH, N_EXPERTS, 1, -1)
    b2 = params['exp_b2'].reshape(DEPTH, N_EXPERTS, 1, -1)
    for l in range(DEPTH):
        lp = _layer_params(l, params)
        mixed = [_mix_and_project(l, lp, xs[s], states[s], cfgs[s]) for s in range(len(xs))]
        hn_all = jnp.concatenate([m[1] for m in mixed], axis=0)
        ridx_all = jnp.concatenate([m[2] for m in mixed], axis=1)
        counts = sum(jnp.sum(m[5][:, ::LANES], axis=1) for m in mixed).astype(jnp.int32)
        block_e, block_v, slot_tok, slot_pos = _routing_tables(ridx_all, n_all, counts)
        y4 = _moe(l, block_e, block_v, slot_tok, slot_pos, hn_all, params['exp_w1'], b1, params['exp_w2'], b2)
        xs = [_combine(mixed[s][0], y4, mixed[s][3], final_g.reshape(1, -1), tok0=tok0[s], n_all=n_all,
                       final=(l == DEPTH - 1)) for s in range(len(xs))]
        for s in range(len(xs)):
            for lst, val in zip(outs[s], mixed[s][4]):
                lst.append(val)
    return xs, [[jnp.stack(v) for v in o] for o in outs]


def kernel(x_prompt, x_sample, state_s5_re, state_s5_im, state_rg_h, state_rg_conv, state_ml_c, state_ml_n, state_ml_m, norm1_g, w_in, s5_lambda_re, s5_lambda_im, s5_log_step, s5_b_re, s5_b_im, s5_c_re, s5_c_im, s5_d, s5_glu_w, s5_glu_b, rg_conv_w, rg_conv_b, rg_wa, rg_ba, rg_wx, rg_bx, rg_lambda, ml_gate_bias, ml_norm_g, w_out, norm2_g, router_w, router_b, exp_w1, exp_b1, exp_w2, exp_b2, final_norm_g):
    params = dict(norm1_g=norm1_g, w_in=w_in, s5_lambda_re=s5_lambda_re, s5_lambda_im=s5_lambda_im,
                  s5_log_step=s5_log_step, s5_b_re=s5_b_re, s5_b_im=s5_b_im, s5_c_re=s5_c_re, s5_c_im=s5_c_im,
                  s5_d=s5_d, s5_glu_w=s5_glu_w, s5_glu_b=s5_glu_b, rg_conv_w=rg_conv_w, rg_conv_b=rg_conv_b,
                  rg_wa=rg_wa, rg_ba=rg_ba, rg_wx=rg_wx, rg_bx=rg_bx, rg_lambda=rg_lambda,
                  ml_gate_bias=ml_gate_bias, ml_norm_g=ml_norm_g, w_out=w_out, norm2_g=norm2_g,
                  router_w=router_w, router_b=router_b, exp_w1=exp_w1, exp_b1=exp_b1, exp_w2=exp_w2, exp_b2=exp_b2)
    bp, tp, _ = x_prompt.shape
    bs, ts, _ = x_sample.shape
    assert tp >= RG_CONV - 1 and ts >= RG_CONV - 1
    zeros = lambda *shape: jnp.zeros((DEPTH, bp) + shape, F32)
    prompt_states = (zeros(S5_GROUPS, S5_STATE), zeros(S5_GROUPS, S5_STATE), zeros(RG_WIDTH),
                     zeros(RG_CONV - 1, RG_WIDTH), zeros(ML_HEADS, ML_DK, ML_DV), zeros(ML_HEADS, ML_DK),
                     zeros(ML_HEADS))
    ts_pad = -(-ts // SUBLANES) * SUBLANES
    xs_pad = jnp.pad(x_sample, ((0, 0), (0, ts_pad - ts), (0, 0)))
    sample_states = (state_s5_re, state_s5_im, state_rg_h, state_rg_conv, state_ml_c, state_ml_n, state_ml_m)
    cfgs = [dict(nb=bp, t=tp, t_valid=tp, s5_tc=128, ml_bb=8, ml_cl=math.gcd(tp, ML_CHUNK_LEN)),
            dict(nb=bs, t=ts_pad, t_valid=ts, s5_tc=ts_pad, ml_bb=8, ml_cl=ts_pad)]
    (yp, ys), (sp, ss) = _trunks([x_prompt.reshape(bp * tp, D_MODEL), xs_pad.reshape(bs * ts_pad, D_MODEL)],
                                 [prompt_states, sample_states], cfgs, params, final_norm_g)
    return (yp.reshape(bp, tp, D_MODEL), ys.reshape(bs, ts_pad, D_MODEL)[:, :ts], *sp, *ss)
```

```python
import functools
import math

import jax
import jax.numpy as jnp
from jax import lax
from jax.experimental import pallas as pl
from jax.experimental.pallas import tpu as pltpu

F32 = jnp.float32
BF16 = jnp.bfloat16

SUBLANES = 8
LANES = 128
VMEM_LIMIT_BYTES = 56 * 1024 * 1024

D_MODEL = 1024
DEPTH = 2
S5_WIDTH = 256
S5_GROUP = 16
S5_GROUPS = 16
S5_STATE = 64
S5_FLAT = S5_GROUPS * S5_STATE
RG_WIDTH = 256
RG_BLOCKS = 8
RG_CONV = 4
RG_C = 8.0
ML_WIDTH = 512
ML_HEADS = 4
ML_DK = 128
ML_DV = 128
ML_CHUNK_LEN = 128
N_EXPERTS = 32
TOP_K = 4
D_FF = 1024
SWIGLU_LIMIT = 7.0
SWIGLU_ALPHA = 1.702
EPS = 1e-5

IN_S5 = (0, 256)
IN_RG = (256, 768)
IN_ML = (768, 2816)
IN_GATE = (2816, 2944)
N_IN = 2824
N_IN_PAD = 2944

ROW_TILE = 512
ROUTE_TILE = 128
MOE_TILE = 256
NEG_BIG = -1e30


def _cparams(sem):
    return pltpu.CompilerParams(dimension_semantics=sem, vmem_limit_bytes=VMEM_LIMIT_BYTES)


def _full(shape):
    n = len(shape)
    return pl.BlockSpec(shape, lambda *_: (0,) * n)


ROW_TILES = D_MODEL // LANES
assert ROW_TILES == SUBLANES


def _rows_to_tiles(x, dst_ref):
    m = x.shape[0]
    for j in range(ROW_TILES):
        dst_ref[pl.ds(j, m, stride=ROW_TILES), :] = x[:, j * LANES:(j + 1) * LANES]


def _tiles_to_rows(src_ref):
    m = src_ref.shape[0] // ROW_TILES
    return jnp.concatenate([src_ref[pl.ds(j, m, stride=ROW_TILES), :] for j in range(ROW_TILES)], axis=-1)


def _inproj_body(x_ref, g_ref, w_ref, zs5_ref, zrg_ref, zml_ref, zgt_ref):
    x = x_ref[...]
    h = x * lax.rsqrt(jnp.mean(x * x, axis=-1, keepdims=True) + EPS) * g_ref[...]
    hb = h.astype(BF16)
    for ref, (lo, hi) in ((zs5_ref, IN_S5), (zrg_ref, IN_RG), (zml_ref, IN_ML), (zgt_ref, IN_GATE)):
        ref[...] = jnp.dot(hb, w_ref[:, lo:hi], preferred_element_type=F32)


def _inproj(x, g, w):
    n = x.shape[0]
    tm = ROW_TILE
    widths = [hi - lo for lo, hi in (IN_S5, IN_RG, IN_ML, IN_GATE)]
    return pl.pallas_call(
        _inproj_body,
        grid=(n // tm,),
        in_specs=[pl.BlockSpec((tm, D_MODEL), lambda i: (i, 0)), _full((1, D_MODEL)), _full((D_MODEL, N_IN_PAD))],
        out_specs=[pl.BlockSpec((tm, w_), lambda i: (i, 0)) for w_ in widths],
        out_shape=[jax.ShapeDtypeStruct((n, w_), F32) for w_ in widths],
        compiler_params=_cparams(("arbitrary",)),
        name="inproj",
    )(x, g, w)


def _s5_body(u_ref, s0r_ref, s0i_ref, lam_ref, bre_ref, bim_ref, cre_ref, cim_ref, d_ref, gw_ref, gb_ref,
             y_ref, sr_ref, si_ref, bur, bui, ab, bbr, bbi, tm_s, bm_s, *, nb, tc, nc, t_valid):
    c = pl.program_id(0)

    @pl.when(c == 0)
    def _():
        sr_ref[...] = s0r_ref[...]
        si_ref[...] = s0i_ref[...]
        lr = lam_ref[0:1, :]
        li = lam_ref[1:2, :]
        step = jnp.exp(lam_ref[2:3, :])
        mag = jnp.exp(lr * step)
        ab_re = mag * jnp.cos(li * step)
        ab_im = mag * jnp.sin(li * step)
        den = lr * lr + li * li
        num_re = ab_re - 1.0
        z_re = (num_re * lr + ab_im * li) / den
        z_im = (ab_im * lr - num_re * li) / den
        ab[0:1, :] = ab_re
        ab[1:2, :] = ab_im
        bbr[...] = (z_re * bre_ref[...] - z_im * bim_ref[...]).astype(BF16)
        bbi[...] = (z_re * bim_ref[...] + z_im * bre_ref[...]).astype(BF16)

    _to_time_major(u_ref, tm_s, bm_s, nb, tc)
    u = jnp.concatenate([tm_s[j] for j in range(S5_WIDTH // LANES)], axis=-1)
    ub = u.astype(BF16)
    bur[...] = jnp.dot(ub, bbr[...], preferred_element_type=F32)
    bui[...] = jnp.dot(ub, bbi[...], preferred_element_type=F32)

    a_re = jnp.broadcast_to(ab[0:1, :], (SUBLANES, S5_FLAT))
    a_im = jnp.broadcast_to(ab[1:2, :], (SUBLANES, S5_FLAT))
    last_in = t_valid - (nc - 1) * tc
    steps = tc if last_in == tc else jnp.where(c == nc - 1, last_in, tc)

    def group(gi, _):
        g8 = gi * SUBLANES
        rows8 = pl.ds(pl.multiple_of(g8, SUBLANES), SUBLANES)

        def step_fn(t, carry):
            s_re, s_im = carry
            rows = pl.ds(pl.multiple_of(t * nb + g8, SUBLANES), SUBLANES)
            n_re = a_re * s_re - a_im * s_im + bur[rows, :]
            n_im = a_re * s_im + a_im * s_re + bui[rows, :]
            bur[rows, :] = n_re
            bui[rows, :] = n_im
            return n_re, n_im

        s_re, s_im = lax.fori_loop(0, steps, step_fn, (sr_ref[rows8, :], si_ref[rows8, :]))
        sr_ref[rows8, :] = s_re
        si_ref[rows8, :] = s_im
        return 0

    lax.fori_loop(0, nb // SUBLANES, group, 0)

    y = (jnp.dot(bur[...].astype(BF16), cre_ref[...], preferred_element_type=F32)
         - jnp.dot(bui[...].astype(BF16), cim_ref[...], preferred_element_type=F32))
    y = y + d_ref[...] * u
    g = jax.nn.gelu(y)
    gate = jax.nn.sigmoid(jnp.dot(g.astype(BF16), gw_ref[...], preferred_element_type=F32) + gb_ref[...])
    out = g * gate
    for j in range(S5_WIDTH // LANES):
        tm_s[j] = out[:, j * LANES:(j + 1) * LANES]
    _to_batch_major(tm_s, bm_s, y_ref, nb, tc)


def _to_time_major(src_ref, tm_s, bm_s, nb, tc):
    for j in range(tm_s.shape[0]):
        lanes = slice(j * LANES, (j + 1) * LANES)
        if nb <= tc:
            for b in range(nb):
                tm_s[j, pl.ds(b, tc, stride=nb), :] = src_ref[b, :, lanes]
        else:
            bm_s[j] = src_ref[:, :, lanes].reshape(nb * tc, LANES)
            for t in range(tc):
                tm_s[j, t * nb:(t + 1) * nb, :] = bm_s[j, pl.ds(t, nb, stride=tc), :]


def _to_batch_major(tm_s, bm_s, dst_ref, nb, tc):
    for j in range(tm_s.shape[0]):
        lanes = slice(j * LANES, (j + 1) * LANES)
        if nb <= tc:
            for b in range(nb):
                dst_ref[b, :, lanes] = tm_s[j, pl.ds(b, tc, stride=nb), :]
        else:
            for t in range(tc):
                bm_s[j, pl.ds(t, nb, stride=tc), :] = tm_s[j, t * nb:(t + 1) * nb, :]
            dst_ref[:, :, lanes] = bm_s[j].reshape(nb, tc, LANES)


def _s5(u, s0r, s0i, lam, bre, bim, cre, cim, d, gw, gb, *, tc, t_valid):
    nb, t, _ = u.shape
    nc = t // tc
    assert (nc - 1) * tc < t_valid <= t
    body = functools.partial(_s5_body, nb=nb, tc=tc, nc=nc, t_valid=t_valid)
    return pl.pallas_call(
        body,
        grid=(t // tc,),
        in_specs=[pl.BlockSpec((nb, tc, S5_WIDTH), lambda c: (0, c, 0)),
                  _full((nb, S5_FLAT)), _full((nb, S5_FLAT)), _full((SUBLANES, S5_FLAT)),
                  _full((S5_WIDTH, S5_FLAT)), _full((S5_WIDTH, S5_FLAT)),
                  _full((S5_FLAT, S5_WIDTH)), _full((S5_FLAT, S5_WIDTH)),
                  _full((1, S5_WIDTH)), _full((S5_WIDTH, S5_WIDTH)), _full((1, S5_WIDTH))],
        out_specs=[pl.BlockSpec((nb, tc, S5_WIDTH), lambda c: (0, c, 0)),
                   _full((nb, S5_FLAT)), _full((nb, S5_FLAT))],
        out_shape=[jax.ShapeDtypeStruct((nb, t, S5_WIDTH), F32),
                   jax.ShapeDtypeStruct((nb, S5_FLAT), F32), jax.ShapeDtypeStruct((nb, S5_FLAT), F32)],
        scratch_shapes=[pltpu.VMEM((nb * tc, S5_FLAT), F32), pltpu.VMEM((nb * tc, S5_FLAT), F32),
                        pltpu.VMEM((SUBLANES, S5_FLAT), F32),
                        pltpu.VMEM((S5_WIDTH, S5_FLAT), BF16), pltpu.VMEM((S5_WIDTH, S5_FLAT), BF16),
                        pltpu.VMEM((S5_WIDTH // LANES, nb * tc, LANES), F32),
                        pltpu.VMEM((S5_WIDTH // LANES, nb * tc, LANES), F32)],
        compiler_params=_cparams(("arbitrary",)),
        name="s5_scan",
    )(u, s0r, s0i, lam, bre, bim, cre, cim, d, gw, gb)


def _rglru_body(z_ref, h0_ref, cb0_ref, cw_ref, cb_ref, wa_ref, ba_ref, wx_ref, bx_ref, lam_ref,
                y_ref, h_ref, cbn_ref, xs, a_s, b_s, tm_s, bm_s, *, nb, tc, nc, t_valid):
    c = pl.program_id(0)
    hist = RG_CONV - 1

    @pl.when(c == 0)
    def _():
        h_ref[...] = h0_ref[...]
        xs[0:hist * nb, :] = cb0_ref[...]

    _to_time_major(z_ref, tm_s, bm_s, nb, tc)
    xs[hist * nb:(hist + tc) * nb, :] = jnp.concatenate([tm_s[j] for j in range(RG_WIDTH // LANES)], axis=-1)

    xc = jnp.zeros((tc * nb, RG_WIDTH), F32) + cb_ref[...]
    for j in range(RG_CONV):
        xc = xc + xs[j * nb:(j + tc) * nb, :] * cw_ref[j:j + 1, :]

    last_in = t_valid - (nc - 1) * tc
    last = tc if last_in == tc else jnp.where(c == nc - 1, last_in, tc)

    @pl.when(c == nc - 1)
    def _():
        cbn_ref[...] = xs[last_in * nb:(last_in + hist) * nb, :]

    xs[0:hist * nb, :] = xs[tc * nb:(tc + hist) * nb, :]

    xb = xc.astype(BF16)
    r = jax.nn.sigmoid(jnp.dot(xb, wa_ref[...], preferred_element_type=F32) + ba_ref[...])
    i = jax.nn.sigmoid(jnp.dot(xb, wx_ref[...], preferred_element_type=F32) + bx_ref[...])
    log_a = -RG_C * r * jax.nn.softplus(-lam_ref[...])
    a_s[...] = jnp.exp(log_a)
    th = jnp.tanh(log_a)
    b_s[...] = jnp.sqrt(-2.0 * th / (1.0 - th)) * (i * xc)

    def group(gi, _):
        g8 = gi * SUBLANES
        rows8 = pl.ds(pl.multiple_of(g8, SUBLANES), SUBLANES)

        def step_fn(t, h):
            rows = pl.ds(pl.multiple_of(t * nb + g8, SUBLANES), SUBLANES)
            h = a_s[rows, :] * h + b_s[rows, :]
            b_s[rows, :] = h
            return h

        h_ref[rows8, :] = lax.fori_loop(0, last, step_fn, h_ref[rows8, :])
        return 0

    lax.fori_loop(0, nb // SUBLANES, group, 0)
    for j in range(RG_WIDTH // LANES):
        tm_s[j] = b_s[:, j * LANES:(j + 1) * LANES]
    _to_batch_major(tm_s, bm_s, y_ref, nb, tc)
    y_ref[...] = y_ref[...] * jax.nn.gelu(z_ref[:, :, RG_WIDTH:2 * RG_WIDTH])


def _rglru(z, h0, cb0, cw, cb, wa, ba, wx, bx, lam, *, tc, t_valid):
    nb, t, _ = z.shape
    hist = RG_CONV - 1
    nc = t // tc
    assert (nc - 1) * tc + hist <= t_valid <= t
    body = functools.partial(_rglru_body, nb=nb, tc=tc, nc=nc, t_valid=t_valid)
    y, h, cbn = pl.pallas_call(
        body,
        grid=(t // tc,),
        in_specs=[pl.BlockSpec((nb, tc, 2 * RG_WIDTH), lambda c: (0, c, 0)),
                  _full((nb, RG_WIDTH)), _full((hist * nb, RG_WIDTH)),
                  _full((RG_CONV, RG_WIDTH)), _full((1, RG_WIDTH)),
                  _full((RG_WIDTH, RG_WIDTH)), _full((1, RG_WIDTH)),
                  _full((RG_WIDTH, RG_WIDTH)), _full((1, RG_WIDTH)), _full((1, RG_WIDTH))],
        out_specs=[pl.BlockSpec((nb, tc, RG_WIDTH), lambda c: (0, c, 0)),
                   _full((nb, RG_WIDTH)), _full((hist * nb, RG_WIDTH))],
        out_shape=[jax.ShapeDtypeStruct((nb, t, RG_WIDTH), F32),
                   jax.ShapeDtypeStruct((nb, RG_WIDTH), F32), jax.ShapeDtypeStruct((hist * nb, RG_WIDTH), F32)],
        scratch_shapes=[pltpu.VMEM(((tc + hist) * nb, RG_WIDTH), F32),
                        pltpu.VMEM((nb * tc, RG_WIDTH), F32), pltpu.VMEM((nb * tc, RG_WIDTH), F32),
                        pltpu.VMEM((RG_WIDTH // LANES, nb * tc, LANES), F32),
                        pltpu.VMEM((RG_WIDTH // LANES, nb * tc, LANES), F32)],
        compiler_params=_cparams(("arbitrary",)),
        name="rglru_scan",
    )(z, h0, cb0.transpose(1, 0, 2).reshape(hist * nb, RG_WIDTH), cw, cb, wa, ba, wx, bx, lam)
    return y, h, cbn.reshape(hist, nb, RG_WIDTH).transpose(1, 0, 2)


def _dot_nt(a, b, **kw):
    return lax.dot_general(a, b, (((1,), (1,)), ((), ())), preferred_element_type=F32, **kw)


def _dot_tn(a, b, **kw):
    return lax.dot_general(a, b, (((0,), (0,)), ((), ())), preferred_element_type=F32, **kw)


def _mlstm_body(q_ref, k_ref, v_ref, o_ref, g_ref, gb_ref, ng_ref, c0_ref, n0_ref, m0_ref,
                y_ref, cn_ref, nn_ref, mn_ref, *, bb, cl, t_valid):
    c = pl.program_id(1)

    @pl.when(c == 0)
    def _():
        cn_ref[...] = c0_ref[0]
        nn_ref[...] = n0_ref[...]
        mn_ref[...] = m0_ref[...]

    row = lax.broadcasted_iota(jnp.int32, (cl, cl), 0)
    col = lax.broadcasted_iota(jnp.int32, (cl, cl), 1)
    tril = (col <= row).astype(F32)
    keep = (col <= row) & (col + c * cl < t_valid)
    t_ok = (lax.broadcasted_iota(jnp.int32, (cl, LANES), 0) + c * cl) < t_valid
    lane = lax.broadcasted_iota(jnp.int32, (cl, LANES), 1)
    ones = jnp.ones((cl, LANES), F32)

    chains = [(bi, h) for bi in range(bb) for h in range(ML_HEADS)]
    hsl = lambda h: slice(h * ML_DK, (h + 1) * ML_DK)
    hi = lax.Precision.HIGHEST

    bcum, ig = [], []
    for bi in range(bb):
        graw = g_ref[bi] + gb_ref[...]
        lf = jnp.where(t_ok, jax.nn.log_sigmoid(graw), 0.0)
        bcum.append(jnp.dot(tril, lf, preferred_element_type=F32, precision=hi))
        ig.append(jnp.where(t_ok, graw, NEG_BIG))

    qf = {ch: q_ref[ch[0], :, hsl(ch[1])] for ch in chains}
    qh = {ch: qf[ch].astype(BF16) for ch in chains}
    kh = {ch: k_ref[ch[0], :, hsl(ch[1])] * (ML_DK ** -0.5) for ch in chains}
    vh = {ch: v_ref[ch[0], :, hsl(ch[1])].astype(BF16) for ch in chains}
    cm = {ch: cn_ref[ch[0], ch[1]] for ch in chains}
    nv = {ch: nn_ref[ch[0], ch[1]:ch[1] + 1, :] for ch in chains}
    m_old = {ch: mn_ref[ch[0], :, ch[1]:ch[1] + 1] for ch in chains}
    b_col = {ch: bcum[ch[0]][:, ML_HEADS + ch[1]:ML_HEADS + ch[1] + 1] for ch in chains}
    i_col = {ch: ig[ch[0]][:, ch[1]:ch[1] + 1] for ch in chains}

    w_row = {ch: _dot_nt(ones, jnp.where(lane == ch[1], i_col[ch] - b_col[ch], 0.0), precision=hi) for ch in chains}
    qk = {ch: _dot_nt(qh[ch], kh[ch].astype(BF16)) for ch in chains}
    qc = {ch: jnp.dot(qh[ch], cm[ch].astype(BF16), preferred_element_type=F32) for ch in chains}
    dmat = {ch: jnp.where(keep, b_col[ch] + w_row[ch], -jnp.inf) for ch in chains}
    dmax = {ch: jnp.max(dmat[ch], axis=-1, keepdims=True) for ch in chains}
    inter = {ch: b_col[ch] + m_old[ch] for ch in chains}
    mt = {ch: jnp.maximum(inter[ch], dmax[ch]) for ch in chains}
    s = {ch: qk[ch] * jnp.exp(dmat[ch] - mt[ch]) for ch in chains}
    sc = {ch: jnp.exp(inter[ch] - mt[ch]) for ch in chains}
    sv = {ch: jnp.dot(s[ch].astype(BF16), vh[ch], preferred_element_type=F32) for ch in chains}
    qn = {ch: jnp.sum(qf[ch] * nv[ch], axis=-1, keepdims=True) for ch in chains}
    den = {ch: jnp.sum(s[ch], axis=-1, keepdims=True) + sc[ch] * qn[ch] for ch in chains}
    hh = {ch: (sv[ch] + sc[ch] * qc[ch]) / jnp.maximum(jnp.abs(den[ch]), jnp.exp(-mt[ch])) for ch in chains}
    bl = {ch: b_col[ch][cl - 1:cl, :] for ch in chains}
    m_new = {ch: mt[ch][cl - 1:cl, :] for ch in chains}
    wk = {ch: jnp.exp(bl[ch] - b_col[ch] + i_col[ch] - m_new[ch]) * kh[ch] for ch in chains}
    decay = {ch: jnp.exp(bl[ch] + m_old[ch] - m_new[ch]) for ch in chains}
    kv = {ch: _dot_tn(wk[ch].astype(BF16), vh[ch]) for ch in chains}
    rms = {ch: lax.rsqrt(jnp.mean(hh[ch] * hh[ch], axis=-1, keepdims=True) + EPS) for ch in chains}
    for ch in chains:
        bi, h = ch
        cn_ref[bi, h] = decay[ch] * cm[ch] + kv[ch]
        nn_ref[bi, h:h + 1, :] = decay[ch] * nv[ch] + jnp.sum(wk[ch], axis=0, keepdims=True)
        mn_ref[bi, :, h:h + 1] = m_new[ch]
        y_ref[bi, :, hsl(h)] = hh[ch] * rms[ch] * ng_ref[:, hsl(h)] * jax.nn.sigmoid(o_ref[bi, :, hsl(h)])


def _mlstm(layer, zml, zgt, gbias, ng, c0, n0, m0, *, bb, cl, t_valid):
    nb, t, _ = zml.shape
    nc = t // cl
    body = functools.partial(_mlstm_body, bb=bb, cl=cl, t_valid=t_valid)
    zspec = lambda j: pl.BlockSpec((bb, cl, ML_WIDTH), lambda i, c, j=j: (i, c, j))
    return pl.pallas_call(
        body,
        grid=(nb // bb, nc),
        in_specs=[zspec(0), zspec(1), zspec(2), zspec(3),
                  pl.BlockSpec((bb, cl, LANES), lambda i, c: (i, c, 0)),
                  _full((1, LANES)), _full((1, ML_WIDTH)),
                  pl.BlockSpec((1, bb, ML_HEADS, ML_DK, ML_DV), lambda i, c: (layer, i, 0, 0, 0)),
                  pl.BlockSpec((bb, ML_HEADS, ML_DK), lambda i, c: (i, 0, 0)),
                  pl.BlockSpec((bb, 1, ML_HEADS), lambda i, c: (i, 0, 0))],
        out_specs=[pl.BlockSpec((bb, cl, ML_WIDTH), lambda i, c: (i, c, 0)),
                   pl.BlockSpec((bb, ML_HEADS, ML_DK, ML_DV), lambda i, c: (i, 0, 0, 0)),
                   pl.BlockSpec((bb, ML_HEADS, ML_DK), lambda i, c: (i, 0, 0)),
                   pl.BlockSpec((bb, 1, ML_HEADS), lambda i, c: (i, 0, 0))],
        out_shape=[jax.ShapeDtypeStruct((nb, t, ML_WIDTH), F32),
                   jax.ShapeDtypeStruct((nb, ML_HEADS, ML_DK, ML_DV), F32),
                   jax.ShapeDtypeStruct((nb, ML_HEADS, ML_DK), F32),
                   jax.ShapeDtypeStruct((nb, 1, ML_HEADS), F32)],
        compiler_params=_cparams(("arbitrary", "arbitrary")),
        name="mlstm_chunks",
    )(zml, zml, zml, zml, zgt, gbias, ng, c0, n0, m0)


def _outproj_body(x_ref, y1_ref, y2_ref, y3_ref, wo_ref, g2_ref, rw_ref, rb_ref,
                  xn_ref, hn_ref, ridx_ref, rgt_ref, cnt_ref):
    acc = jnp.dot(y1_ref[...].astype(BF16), wo_ref[0:256, :], preferred_element_type=F32)
    acc = acc + jnp.dot(y2_ref[...].astype(BF16), wo_ref[256:512, :], preferred_element_type=F32)
    acc = acc + jnp.dot(y3_ref[...].astype(BF16), wo_ref[512:1024, :], preferred_element_type=F32)
    x = x_ref[...] + acc
    xn_ref[...] = x
    h = x * lax.rsqrt(jnp.mean(x * x, axis=-1, keepdims=True) + EPS) * g2_ref[...]
    _rows_to_tiles(h, hn_ref)
    hb = h.astype(BF16)
    tm = x.shape[0]
    eidx = lax.broadcasted_iota(jnp.int32, (N_EXPERTS, ROUTE_TILE), 0)
    sub = lax.broadcasted_iota(jnp.int32, (LANES, ROUTE_TILE), 0)
    cnt = jnp.zeros((N_EXPERTS, 1), F32)
    for j in range(tm // ROUTE_TILE):
        rows = slice(j * ROUTE_TILE, (j + 1) * ROUTE_TILE)
        vals = _dot_nt(rw_ref[...], hb[rows, :]) + rb_ref[...]
        tops, idxs = [], []
        for _ in range(TOP_K):
            m = jnp.max(vals, axis=0, keepdims=True)
            idx = jnp.min(jnp.where(vals == m, eidx, N_EXPERTS), axis=0, keepdims=True)
            vals = jnp.where(eidx == idx, -jnp.inf, vals)
            tops.append(m)
            idxs.append(idx)
        exps = [jnp.exp(t - tops[0]) for t in tops]
        inv = 1.0 / functools.reduce(lambda a, b: a + b, exps)
        ridx_ref[:, rows] = jnp.concatenate(idxs + idxs, axis=0)
        for idx in idxs:
            cnt = cnt + jnp.sum((eidx == idx).astype(F32), axis=1, keepdims=True)
        gt = jnp.zeros((LANES, ROUTE_TILE), F32)
        for kk in range(TOP_K):
            gt = jnp.where(sub == kk, exps[kk] * inv, gt)
        rgt_ref[rows, :] = gt.T
    cnt_ref[...] = jnp.broadcast_to(cnt, (N_EXPERTS, LANES))


def _outproj(x, y1, y2, y3, wo, g2, rw, rb):
    n = x.shape[0]
    tm = ROW_TILE
    rowspec = lambda w_: pl.BlockSpec((tm, w_), lambda i: (i, 0))
    return pl.pallas_call(
        _outproj_body,
        grid=(n // tm,),
        in_specs=[rowspec(D_MODEL), rowspec(S5_WIDTH), rowspec(RG_WIDTH), rowspec(ML_WIDTH),
                  _full((D_MODEL, D_MODEL)), _full((1, D_MODEL)),
                  _full((N_EXPERTS, D_MODEL)), _full((N_EXPERTS, 1))],
        out_specs=[rowspec(D_MODEL), pl.BlockSpec((tm * ROW_TILES, LANES), lambda i: (i, 0)),
                   pl.BlockSpec((2 * TOP_K, tm), lambda i: (0, i)), rowspec(LANES),
                   pl.BlockSpec((N_EXPERTS, LANES), lambda i: (0, i))],
        out_shape=[jax.ShapeDtypeStruct((n, D_MODEL), F32), jax.ShapeDtypeStruct((n * ROW_TILES, LANES), F32),
                   jax.ShapeDtypeStruct((2 * TOP_K, n), jnp.int32), jax.ShapeDtypeStruct((n, LANES), F32),
                   jax.ShapeDtypeStruct((N_EXPERTS, n // tm * LANES), F32)],
        compiler_params=_cparams(("arbitrary",)),
        name="outproj_route",
    )(x, y1, y2, y3, wo, g2, rw, rb)


GATHER_AHEAD = 2
N_XBUF = GATHER_AHEAD + 1
N_OBUF = 3


def _moe_body(be_ref, bv_ref, tokn_ref, posp_ref, posc_ref, hn_ref, w1_ref, b1_ref, w2_ref, b2_ref, y4_ref,
              w1b, w2b, xbuf, obuf, gsem, ssem):
    i = pl.program_id(0)
    last = pl.num_programs(0) - 1
    real = bv_ref[i] > 0
    fresh = jnp.logical_or(i == GATHER_AHEAD, be_ref[i] != be_ref[jnp.maximum(i - 1, 0)])
    xs, xn = lax.rem(i, N_XBUF), lax.rem(i + GATHER_AHEAD, N_XBUF)
    os_, op, opp = lax.rem(i, N_OBUF), lax.rem(i + N_OBUF - 1, N_OBUF), lax.rem(i + N_OBUF - 2, N_OBUF)
    x_cur, x_nxt, o_cur, o_prev = xbuf.at[xs], xbuf.at[xn], obuf.at[os_], obuf.at[op]

    def tile(ref, row):
        return ref.at[pl.ds(pl.multiple_of(row, ROW_TILES), ROW_TILES)]

    def start_gathers(rows=range(MOE_TILE)):
        for r in rows:
            pltpu.make_async_copy(tile(hn_ref, tokn_ref[0, 0, r]), tile(x_nxt, r * ROW_TILES),
                                  gsem.at[xn]).start(priority=0)

    def start_scatters(pos_ref, o_buf, sem, rows=range(MOE_TILE)):
        for r in rows:
            pltpu.make_async_copy(tile(o_buf, r * ROW_TILES), tile(y4_ref, pos_ref[0, 0, r]), sem).start(priority=1)

    def start_block_scatter(pos_ref, o_buf, sem):
        first = pl.multiple_of(pos_ref[0, 0, 0], ROW_TILES)
        pltpu.make_async_copy(o_buf, y4_ref.at[pl.ds(first, MOE_TILE * ROW_TILES)], sem).start(priority=1)

    def wait_scatters(o_buf, sem):
        pltpu.make_async_copy(o_buf, y4_ref.at[pl.ds(0, MOE_TILE * ROW_TILES)], sem).wait()

    gathered = jnp.logical_and(i >= GATHER_AHEAD, jnp.logical_or(
        i < 2 * GATHER_AHEAD, bv_ref[jnp.maximum(i - GATHER_AHEAD, 0)] > 0))

    @pl.when(gathered)
    def _():
        pltpu.make_async_copy(hn_ref.at[pl.ds(0, MOE_TILE * ROW_TILES)], x_cur, gsem.at[xs]).wait()

    @pl.when(i >= N_OBUF)
    def _():
        wait_scatters(o_cur, ssem.at[os_])

    @pl.when(jnp.logical_and(fresh, real))
    def _():
        w1b[...] = w1_ref[0, 0].astype(BF16)
        w2b[...] = w2_ref[0, 0].astype(BF16)

    @pl.when(bv_ref[i] >= 1)
    def _():
        start_gathers()

    @pl.when(real)
    def _():
        xb = _tiles_to_rows(x_cur).astype(BF16)
        start_scatters(posp_ref, o_prev, ssem.at[op])
        hb = jnp.dot(xb, w1b[...], preferred_element_type=F32) + b1_ref[0, 0]
        g = jnp.minimum(hb[:, :D_FF], SWIGLU_LIMIT)
        u = jnp.clip(hb[:, D_FF:], -SWIGLU_LIMIT, SWIGLU_LIMIT)
        act = g * jax.nn.sigmoid(SWIGLU_ALPHA * g) * (u + 1.0)
        _rows_to_tiles(jnp.dot(act.astype(BF16), w2b[...], preferred_element_type=F32) + b2_ref[0, 0], o_cur)

    @pl.when(jnp.logical_not(real))
    def _():
        @pl.when(i == 0)
        def _():
            obuf[...] = jnp.zeros_like(obuf)

        @pl.when(i < GATHER_AHEAD)
        def _():
            start_gathers()

        prev_real = bv_ref[jnp.maximum(i - 1, 0)] > 0

        @pl.when(jnp.logical_and(i >= 1, prev_real))
        def _():
            start_scatters(posp_ref, o_prev, ssem.at[op])

        @pl.when(jnp.logical_and(i >= 1, jnp.logical_not(prev_real)))
        def _():
            start_block_scatter(posp_ref, o_prev, ssem.at[op])

    @pl.when(i == last)
    def _():
        start_block_scatter(posc_ref, o_cur, ssem.at[os_])
        wait_scatters(obuf.at[opp], ssem.at[opp])
        wait_scatters(o_prev, ssem.at[op])
        wait_scatters(o_cur, ssem.at[os_])


def _moe(layer, block_e, block_v, slot_tok, slot_pos, hn, w1, b1, w2, b2):
    n_blocks = block_e.shape[0]
    tm = MOE_TILE
    n_rows = n_blocks * tm
    assert n_blocks > 2 * GATHER_AHEAD
    idx_spec = lambda f: pl.BlockSpec((1, 1, tm), f, memory_space=pltpu.SMEM)
    grid_spec = pltpu.PrefetchScalarGridSpec(
        num_scalar_prefetch=2,
        grid=(n_blocks,),
        in_specs=[idx_spec(lambda i, be, bv: (jnp.minimum(i + GATHER_AHEAD, n_blocks - 1), 0, 0)),
                  idx_spec(lambda i, be, bv: (jnp.maximum(i - 1, 0), 0, 0)),
                  idx_spec(lambda i, be, bv: (i, 0, 0)),
                  pl.BlockSpec(memory_space=pl.ANY),
                  pl.BlockSpec((1, 1, D_MODEL, 2 * D_FF), lambda i, be, bv: (layer, be[i], 0, 0)),
                  pl.BlockSpec((1, 1, 1, 2 * D_FF), lambda i, be, bv: (layer, be[i], 0, 0)),
                  pl.BlockSpec((1, 1, D_FF, D_MODEL), lambda i, be, bv: (layer, be[i], 0, 0)),
                  pl.BlockSpec((1, 1, 1, D_MODEL), lambda i, be, bv: (layer, be[i], 0, 0))],
        out_specs=pl.BlockSpec(memory_space=pl.ANY),
        scratch_shapes=[pltpu.VMEM((D_MODEL, 2 * D_FF), BF16), pltpu.VMEM((D_FF, D_MODEL), BF16)]
                       + [pltpu.VMEM((N_XBUF, tm * ROW_TILES, LANES), F32),
                          pltpu.VMEM((N_OBUF, tm * ROW_TILES, LANES), F32)]
                       + [pltpu.SemaphoreType.DMA((N_XBUF,)), pltpu.SemaphoreType.DMA((N_OBUF,))],
    )
    return pl.pallas_call(
        _moe_body,
        grid_spec=grid_spec,
        out_shape=jax.ShapeDtypeStruct((n_rows * ROW_TILES, LANES), F32),
        compiler_params=_cparams(("arbitrary",)),
        name="moe_experts",
    )(block_e, block_v, slot_tok, slot_pos, slot_pos, hn, w1, b1, w2, b2)


def _combine_body(x_ref, rgt_ref, fg_ref, *rest, final):
    y_refs, o_ref = rest[:TOP_K], rest[TOP_K]
    x = x_ref[...]
    for kk in range(TOP_K):
        x = x + rgt_ref[:, kk:kk + 1] * _tiles_to_rows(y_refs[kk])
    if final:
        x = x * lax.rsqrt(jnp.mean(x * x, axis=-1, keepdims=True) + EPS) * fg_ref[...]
    o_ref[...] = x


def _combine(x, y4, rgt, fg, *, tok0, n_all, final):
    n = x.shape[0]
    tm = ROW_TILE
    assert tok0 % tm == 0 and n_all % tm == 0
    yspec = lambda kk: pl.BlockSpec((tm * ROW_TILES, LANES), lambda i, kk=kk: ((kk * n_all + tok0) // tm + i, 0))
    return pl.pallas_call(
        functools.partial(_combine_body, final=final),
        grid=(n // tm,),
        in_specs=[pl.BlockSpec((tm, D_MODEL), lambda i: (i, 0)),
                  pl.BlockSpec((tm, LANES), lambda i: (i, 0)), _full((1, D_MODEL))]
                 + [yspec(kk) for kk in range(TOP_K)],
        out_specs=pl.BlockSpec((tm, D_MODEL), lambda i: (i, 0)),
        out_shape=jax.ShapeDtypeStruct((n, D_MODEL), F32),
        compiler_params=_cparams(("arbitrary",)),
        name="moe_combine",
    )(x, rgt, fg, *([y4] * TOP_K))


def _routing_tables(ridx, n, counts):
    na = n * TOP_K
    tm = MOE_TILE
    assert na % tm == 0
    lead = GATHER_AHEAD * tm
    n_blocks = na // tm + N_EXPERTS + 2 * GATHER_AHEAD
    n_slots = n_blocks * tm
    experts = jnp.arange(N_EXPERTS, dtype=jnp.int32)
    flat_e = ridx[:TOP_K, :].reshape(-1)
    order = jnp.argsort(flat_e).astype(jnp.int32)
    padded = (counts + tm - 1) // tm * tm
    pad_end = jnp.cumsum(padded)
    pad_start = pad_end - padded
    start = jnp.cumsum(counts) - counts
    spare_start = na + pad_start - start
    total = pad_end[-1]
    blk = jnp.arange(n_blocks, dtype=jnp.int32) * tm - lead
    block_v = ((blk >= 0) & (blk < total)).astype(jnp.int32)
    last_e = jnp.max(jnp.where(counts > 0, experts, 0))
    block_e = jnp.minimum(jnp.sum(blk[:, None] >= pad_end[None, :], axis=1), last_e).astype(jnp.int32)
    rank = (blk - pad_start[block_e])[:, None] + jnp.arange(tm, dtype=jnp.int32)[None, :]
    cnt = counts[block_e][:, None]
    slot = blk[:, None] + jnp.arange(tm, dtype=jnp.int32)[None, :]
    used = (slot >= 0) & (slot < total)
    real = (rank < cnt) & used
    src = order[jnp.clip(start[block_e][:, None] + rank, 0, na - 1)]
    slot_tok = (jnp.where(real, src % n, 0) * ROW_TILES).astype(jnp.int32)
    spare = jnp.where(used, spare_start[block_e][:, None] + rank - cnt, jnp.where(slot >= 0, slot, n_slots + slot))
    slot_pos = (jnp.where(real, src, spare) * ROW_TILES).astype(jnp.int32)
    return block_e, block_v, slot_tok.reshape(n_blocks, 1, tm), slot_pos.reshape(n_blocks, 1, tm)


def _blockdiag(m):
    g, a, b = m.shape
    eye = jnp.eye(g, dtype=m.dtype)
    return (eye[:, None, :, None] * m[:, :, None, :]).reshape(g * a, g * b)


def _layer_params(l, p):
    lam = jnp.zeros((SUBLANES, S5_FLAT), F32)
    lam = lam.at[0].set(p['s5_lambda_re'][l].reshape(-1)).at[1].set(p['s5_lambda_im'][l].reshape(-1))
    lam = lam.at[2].set(jnp.repeat(p['s5_log_step'][l], S5_STATE))
    gbias = jnp.zeros((1, LANES), F32).at[0, :2 * ML_HEADS].set(p['ml_gate_bias'][l])
    w_in = jnp.pad(p['w_in'][l], ((0, 0), (0, N_IN_PAD - N_IN))).astype(BF16)
    return dict(
        norm1_g=p['norm1_g'][l].reshape(1, -1), w_in=w_in, s5_lam=lam,
        s5_bre=_blockdiag(p['s5_b_re'][l].transpose(0, 2, 1)), s5_bim=_blockdiag(p['s5_b_im'][l].transpose(0, 2, 1)),
        s5_cre=_blockdiag(p['s5_c_re'][l].transpose(0, 2, 1)).astype(BF16),
        s5_cim=_blockdiag(p['s5_c_im'][l].transpose(0, 2, 1)).astype(BF16),
        s5_d=p['s5_d'][l].reshape(1, -1), s5_gw=p['s5_glu_w'][l].astype(BF16), s5_gb=p['s5_glu_b'][l].reshape(1, -1),
        rg_cw=p['rg_conv_w'][l], rg_cb=p['rg_conv_b'][l].reshape(1, -1),
        rg_wa=_blockdiag(p['rg_wa'][l]).astype(BF16), rg_ba=p['rg_ba'][l].reshape(1, -1),
        rg_wx=_blockdiag(p['rg_wx'][l]).astype(BF16), rg_bx=p['rg_bx'][l].reshape(1, -1),
        rg_lam=p['rg_lambda'][l].reshape(1, -1),
        ml_gb=gbias, ml_ng=p['ml_norm_g'][l].reshape(1, -1),
        w_out=p['w_out'][l].astype(BF16), norm2_g=p['norm2_g'][l].reshape(1, -1),
        router_w=p['router_w'][l].T.astype(BF16), router_b=p['router_b'][l].reshape(-1, 1),
    )


def _mix_and_project(l, lp, xf, states, cfg):
    nb, t, t_valid = cfg['nb'], cfg['t'], cfg['t_valid']
    n = nb * t
    s5r, s5i, rgh, rgc, mlc, mln, mlm = states
    zs5, zrg, zml, zgt = _inproj(xf, lp['norm1_g'], lp['w_in'])
    y1, nr, ni = _s5(zs5.reshape(nb, t, -1), s5r[l].reshape(nb, -1), s5i[l].reshape(nb, -1), lp['s5_lam'],
                     lp['s5_bre'], lp['s5_bim'], lp['s5_cre'], lp['s5_cim'], lp['s5_d'], lp['s5_gw'], lp['s5_gb'],
                     tc=cfg['s5_tc'], t_valid=t_valid)
    y2, nh, ncv = _rglru(zrg.reshape(nb, t, -1), rgh[l], rgc[l], lp['rg_cw'], lp['rg_cb'], lp['rg_wa'],
                         lp['rg_ba'], lp['rg_wx'], lp['rg_bx'], lp['rg_lam'], tc=cfg['s5_tc'], t_valid=t_valid)
    y3, nc_, nn_, nm_ = _mlstm(l, zml.reshape(nb, t, -1), zgt.reshape(nb, t, -1), lp['ml_gb'], lp['ml_ng'],
                               mlc, mln[l], mlm[l].reshape(nb, 1, ML_HEADS),
                               bb=cfg['ml_bb'], cl=cfg['ml_cl'], t_valid=t_valid)
    xn, hn, ridx, rgt, cnt = _outproj(xf, y1.reshape(n, -1), y2.reshape(n, -1), y3.reshape(n, -1),
                                      lp['w_out'], lp['norm2_g'], lp['router_w'], lp['router_b'])
    new_states = (nr.reshape(nb, S5_GROUPS, S5_STATE), ni.reshape(nb, S5_GROUPS, S5_STATE), nh, ncv, nc_, nn_,
                  nm_.reshape(nb, ML_HEADS))
    return xn, hn, ridx, rgt, new_states, cnt


def _trunks(xs, states, cfgs, params, final_g):
    n_all = sum(x.shape[0] for x in xs)
    tok0 = [sum(x.shape[0] for x in xs[:s]) for s in range(len(xs))]
    outs = [[[] for _ in range(7)] for _ in xs]
    b1 = params['exp_b1'].reshape(DEPTH, N_EXPERTS, 1, -1)
    b2 = params['exp_b2'].reshape(DEPTH, N_EXPERTS, 1, -1)
    for l in range(DEPTH):
        lp = _layer_params(l, params)
        mixed = [_mix_and_project(l, lp, xs[s], states[s], cfgs[s]) for s in range(len(xs))]
        hn_all = jnp.concatenate([m[1] for m in mixed], axis=0)
        ridx_all = jnp.concatenate([m[2] for m in mixed], axis=1)
        counts = sum(jnp.sum(m[5][:, ::LANES], axis=1) for m in mixed).astype(jnp.int32)
        block_e, block_v, slot_tok, slot_pos = _routing_tables(ridx_all, n_all, counts)
        y4 = _moe(l, block_e, block_v, slot_tok, slot_pos, hn_all, params['exp_w1'], b1, params['exp_w2'], b2)
        xs = [_combine(mixed[s][0], y4, mixed[s][3], final_g.reshape(1, -1), tok0=tok0[s], n_all=n_all,
                       final=(l == DEPTH - 1)) for s in range(len(xs))]
        for s in range(len(xs)):
            for lst, val in zip(outs[s], mixed[s][4]):
                lst.append(val)
    return xs, [[jnp.stack(v) for v in o] for o in outs]


def kernel(x_prompt, x_sample, state_s5_re, state_s5_im, state_rg_h, state_rg_conv, state_ml_c, state_ml_n, state_ml_m, norm1_g, w_in, s5_lambda_re, s5_lambda_im, s5_log_step, s5_b_re, s5_b_im, s5_c_re, s5_c_im, s5_d, s5_glu_w, s5_glu_b, rg_conv_w, rg_conv_b, rg_wa, rg_ba, rg_wx, rg_bx, rg_lambda, ml_gate_bias, ml_norm_g, w_out, norm2_g, router_w, router_b, exp_w1, exp_b1, exp_w2, exp_b2, final_norm_g):
    params = dict(norm1_g=norm1_g, w_in=w_in, s5_lambda_re=s5_lambda_re, s5_lambda_im=s5_lambda_im,
                  s5_log_step=s5_log_step, s5_b_re=s5_b_re, s5_b_im=s5_b_im, s5_c_re=s5_c_re, s5_c_im=s5_c_im,
                  s5_d=s5_d, s5_glu_w=s5_glu_w, s5_glu_b=s5_glu_b, rg_conv_w=rg_conv_w, rg_conv_b=rg_conv_b,
                  rg_wa=rg_wa, rg_ba=rg_ba, rg_wx=rg_wx, rg_bx=rg_bx, rg_lambda=rg_lambda,
                  ml_gate_bias=ml_gate_bias, ml_norm_g=ml_norm_g, w_out=w_out, norm2_g=norm2_g,
                  router_w=router_w, router_b=router_b, exp_w1=exp_w1, exp_b1=exp_b1, exp_w2=exp_w2, exp_b2=exp_b2)
    bp, tp, _ = x_prompt.shape
    bs, ts, _ = x_sample.shape
    assert tp >= RG_CONV - 1 and ts >= RG_CONV - 1
    zeros = lambda *shape: jnp.zeros((DEPTH, bp) + shape, F32)
    prompt_states = (zeros(S5_GROUPS, S5_STATE), zeros(S5_GROUPS, S5_STATE), zeros(RG_WIDTH),
                     zeros(RG_CONV - 1, RG_WIDTH), zeros(ML_HEADS, ML_DK, ML_DV), zeros(ML_HEADS, ML_DK),
                     zeros(ML_HEADS))
    ts_pad = -(-ts // SUBLANES) * SUBLANES
    xs_pad = jnp.pad(x_sample, ((0, 0), (0, ts_pad - ts), (0, 0)))
    sample_states = (state_s5_re, state_s5_im, state_rg_h, state_rg_conv, state_ml_c, state_ml_n, state_ml_m)
    cfgs = [dict(nb=bp, t=tp, t_valid=tp, s5_tc=128, ml_bb=8, ml_cl=math.gcd(tp, ML_CHUNK_LEN)),
            dict(nb=bs, t=ts_pad, t_valid=ts, s5_tc=ts_pad, ml_bb=8, ml_cl=ts_pad)]
    (yp, ys), (sp, ss) = _trunks([x_prompt.reshape(bp * tp, D_MODEL), xs_pad.reshape(bs * ts_pad, D_MODEL)],
                                 [prompt_states, sample_states], cfgs, params, final_norm_g)
    return (yp.reshape(bp, tp, D_MODEL), ys.reshape(bs, ts_pad, D_MODEL)[:, :ts], *sp, *ss)
```

```python
import functools
import math

import jax
import jax.numpy as jnp
from jax import lax
from jax.experimental import pallas as pl
from jax.experimental.pallas import tpu as pltpu

F32 = jnp.float32
BF16 = jnp.bfloat16

SUBLANES = 8
LANES = 128
VMEM_LIMIT_BYTES = 56 * 1024 * 1024

D_MODEL = 1024
DEPTH = 2
S5_WIDTH = 256
S5_GROUP = 16
S5_GROUPS = 16
S5_STATE = 64
S5_FLAT = S5_GROUPS * S5_STATE
RG_WIDTH = 256
RG_BLOCKS = 8
RG_CONV = 4
RG_C = 8.0
ML_WIDTH = 512
ML_HEADS = 4
ML_DK = 128
ML_DV = 128
ML_CHUNK_LEN = 128
N_EXPERTS = 32
TOP_K = 4
D_FF = 1024
SWIGLU_LIMIT = 7.0
SWIGLU_ALPHA = 1.702
EPS = 1e-5

IN_S5 = (0, 256)
IN_RG = (256, 768)
IN_ML = (768, 2816)
IN_GATE = (2816, 2944)
N_IN = 2824
N_IN_PAD = 2944

ROW_TILE = 512
ROUTE_TILE = 128
MOE_TILE = 256
NEG_BIG = -1e30


def _cparams(sem):
    return pltpu.CompilerParams(dimension_semantics=sem, vmem_limit_bytes=VMEM_LIMIT_BYTES)


def _full(shape):
    n = len(shape)
    return pl.BlockSpec(shape, lambda *_: (0,) * n)


ROW_TILES = D_MODEL // LANES
assert ROW_TILES == SUBLANES


def _rows_to_tiles(x, dst_ref):
    m = x.shape[0]
    for j in range(ROW_TILES):
        dst_ref[pl.ds(j, m, stride=ROW_TILES), :] = x[:, j * LANES:(j + 1) * LANES]


def _tiles_to_rows(src_ref):
    m = src_ref.shape[0] // ROW_TILES
    return jnp.concatenate([src_ref[pl.ds(j, m, stride=ROW_TILES), :] for j in range(ROW_TILES)], axis=-1)


def _inproj_body(x_ref, g_ref, w_ref, zs5_ref, zrg_ref, zml_ref, zgt_ref):
    x = x_ref[...]
    h = x * lax.rsqrt(jnp.mean(x * x, axis=-1, keepdims=True) + EPS) * g_ref[...]
    hb = h.astype(BF16)
    for ref, (lo, hi) in ((zs5_ref, IN_S5), (zrg_ref, IN_RG), (zml_ref, IN_ML), (zgt_ref, IN_GATE)):
        ref[...] = jnp.dot(hb, w_ref[:, lo:hi], preferred_element_type=F32)


def _inproj(x, g, w):
    n = x.shape[0]
    tm = ROW_TILE
    widths = [hi - lo for lo, hi in (IN_S5, IN_RG, IN_ML, IN_GATE)]
    return pl.pallas_call(
        _inproj_body,
        grid=(n // tm,),
        in_specs=[pl.BlockSpec((tm, D_MODEL), lambda i: (i, 0)), _full((1, D_MODEL)), _full((D_MODEL, N_IN_PAD))],
        out_specs=[pl.BlockSpec((tm, w_), lambda i: (i, 0)) for w_ in widths],
        out_shape=[jax.ShapeDtypeStruct((n, w_), F32) for w_ in widths],
        compiler_params=_cparams(("arbitrary",)),
        name="inproj",
    )(x, g, w)


def _s5_body(u_ref, s0r_ref, s0i_ref, lam_ref, bre_ref, bim_ref, cre_ref, cim_ref, d_ref, gw_ref, gb_ref,
             y_ref, sr_ref, si_ref, bur, bui, ab, bbr, bbi, tm_s, bm_s, *, nb, tc, nc, t_valid):
    c = pl.program_id(0)

    @pl.when(c == 0)
    def _():
        sr_ref[...] = s0r_ref[...]
        si_ref[...] = s0i_ref[...]
        lr = lam_ref[0:1, :]
        li = lam_ref[1:2, :]
        step = jnp.exp(lam_ref[2:3, :])
        mag = jnp.exp(lr * step)
        ab_re = mag * jnp.cos(li * step)
        ab_im = mag * jnp.sin(li * step)
        den = lr * lr + li * li
        num_re = ab_re - 1.0
        z_re = (num_re * lr + ab_im * li) / den
        z_im = (ab_im * lr - num_re * li) / den
        ab[0:1, :] = ab_re
        ab[1:2, :] = ab_im
        bbr[...] = (z_re * bre_ref[...] - z_im * bim_ref[...]).astype(BF16)
        bbi[...] = (z_re * bim_ref[...] + z_im * bre_ref[...]).astype(BF16)

    _to_time_major(u_ref, tm_s, bm_s, nb, tc)
    u = jnp.concatenate([tm_s[j] for j in range(S5_WIDTH // LANES)], axis=-1)
    ub = u.astype(BF16)
    bur[...] = jnp.dot(ub, bbr[...], preferred_element_type=F32)
    bui[...] = jnp.dot(ub, bbi[...], preferred_element_type=F32)

    a_re = jnp.broadcast_to(ab[0:1, :], (SUBLANES, S5_FLAT))
    a_im = jnp.broadcast_to(ab[1:2, :], (SUBLANES, S5_FLAT))
    last_in = t_valid - (nc - 1) * tc
    steps = tc if last_in == tc else jnp.where(c == nc - 1, last_in, tc)

    def group(gi, _):
        g8 = gi * SUBLANES
        rows8 = pl.ds(pl.multiple_of(g8, SUBLANES), SUBLANES)

        def step_fn(t, carry):
            s_re, s_im = carry
            rows = pl.ds(pl.multiple_of(t * nb + g8, SUBLANES), SUBLANES)
            n_re = a_re * s_re - a_im * s_im + bur[rows, :]
            n_im = a_re * s_im + a_im * s_re + bui[rows, :]
            bur[rows, :] = n_re
            bui[rows, :] = n_im
            return n_re, n_im

        s_re, s_im = lax.fori_loop(0, steps, step_fn, (sr_ref[rows8, :], si_ref[rows8, :]))
        sr_ref[rows8, :] = s_re
        si_ref[rows8, :] = s_im
        return 0

    lax.fori_loop(0, nb // SUBLANES, group, 0)

    y = (jnp.dot(bur[...].astype(BF16), cre_ref[...], preferred_element_type=F32)
         - jnp.dot(bui[...].astype(BF16), cim_ref[...], preferred_element_type=F32))
    y = y + d_ref[...] * u
    g = jax.nn.gelu(y)
    gate = jax.nn.sigmoid(jnp.dot(g.astype(BF16), gw_ref[...], preferred_element_type=F32) + gb_ref[...])
    out = g * gate
    for j in range(S5_WIDTH // LANES):
        tm_s[j] = out[:, j * LANES:(j + 1) * LANES]
    _to_batch_major(tm_s, bm_s, y_ref, nb, tc)


def _to_time_major(src_ref, tm_s, bm_s, nb, tc):
    for j in range(tm_s.shape[0]):
        lanes = slice(j * LANES, (j + 1) * LANES)
        if nb <= tc:
            for b in range(nb):
                tm_s[j, pl.ds(b, tc, stride=nb), :] = src_ref[b, :, lanes]
        else:
            bm_s[j] = src_ref[:, :, lanes].reshape(nb * tc, LANES)
            for t in range(tc):
                tm_s[j, t * nb:(t + 1) * nb, :] = bm_s[j, pl.ds(t, nb, stride=tc), :]


def _to_batch_major(tm_s, bm_s, dst_ref, nb, tc):
    for j in range(tm_s.shape[0]):
        lanes = slice(j * LANES, (j + 1) * LANES)
        if nb <= tc:
            for b in range(nb):
                dst_ref[b, :, lanes] = tm_s[j, pl.ds(b, tc, stride=nb), :]
        else:
            for t in range(tc):
                bm_s[j, pl.ds(t, nb, stride=tc), :] = tm_s[j, t * nb:(t + 1) * nb, :]
            dst_ref[:, :, lanes] = bm_s[j].reshape(nb, tc, LANES)


def _s5(u, s0r, s0i, lam, bre, bim, cre, cim, d, gw, gb, *, tc, t_valid):
    nb, t, _ = u.shape
    nc = t // tc
    assert (nc - 1) * tc < t_valid <= t
    body = functools.partial(_s5_body, nb=nb, tc=tc, nc=nc, t_valid=t_valid)
    return pl.pallas_call(
        body,
        grid=(t // tc,),
        in_specs=[pl.BlockSpec((nb, tc, S5_WIDTH), lambda c: (0, c, 0)),
                  _full((nb, S5_FLAT)), _full((nb, S5_FLAT)), _full((SUBLANES, S5_FLAT)),
                  _full((S5_WIDTH, S5_FLAT)), _full((S5_WIDTH, S5_FLAT)),
                  _full((S5_FLAT, S5_WIDTH)), _full((S5_FLAT, S5_WIDTH)),
                  _full((1, S5_WIDTH)), _full((S5_WIDTH, S5_WIDTH)), _full((1, S5_WIDTH))],
        out_specs=[pl.BlockSpec((nb, tc, S5_WIDTH), lambda c: (0, c, 0)),
                   _full((nb, S5_FLAT)), _full((nb, S5_FLAT))],
        out_shape=[jax.ShapeDtypeStruct((nb, t, S5_WIDTH), F32),
                   jax.ShapeDtypeStruct((nb, S5_FLAT), F32), jax.ShapeDtypeStruct((nb, S5_FLAT), F32)],
        scratch_shapes=[pltpu.VMEM((nb * tc, S5_FLAT), F32), pltpu.VMEM((nb * tc, S5_FLAT), F32),
                        pltpu.VMEM((SUBLANES, S5_FLAT), F32),
                        pltpu.VMEM((S5_WIDTH, S5_FLAT), BF16), pltpu.VMEM((S5_WIDTH, S5_FLAT), BF16),
                        pltpu.VMEM((S5_WIDTH // LANES, nb * tc, LANES), F32),
                        pltpu.VMEM((S5_WIDTH // LANES, nb * tc, LANES), F32)],
        compiler_params=_cparams(("arbitrary",)),
        name="s5_scan",
    )(u, s0r, s0i, lam, bre, bim, cre, cim, d, gw, gb)


def _rglru_body(z_ref, h0_ref, cb0_ref, cw_ref, cb_ref, wa_ref, ba_ref, wx_ref, bx_ref, lam_ref,
                y_ref, h_ref, cbn_ref, xs, a_s, b_s, tm_s, bm_s, *, nb, tc, nc, t_valid):
    c = pl.program_id(0)
    hist = RG_CONV - 1

    @pl.when(c == 0)
    def _():
        h_ref[...] = h0_ref[...]
        xs[0:hist * nb, :] = cb0_ref[...]

    _to_time_major(z_ref, tm_s, bm_s, nb, tc)
    xs[hist * nb:(hist + tc) * nb, :] = jnp.concatenate([tm_s[j] for j in range(RG_WIDTH // LANES)], axis=-1)

    xc = jnp.zeros((tc * nb, RG_WIDTH), F32) + cb_ref[...]
    for j in range(RG_CONV):
        xc = xc + xs[j * nb:(j + tc) * nb, :] * cw_ref[j:j + 1, :]

    last_in = t_valid - (nc - 1) * tc
    last = tc if last_in == tc else jnp.where(c == nc - 1, last_in, tc)

    @pl.when(c == nc - 1)
    def _():
        cbn_ref[...] = xs[last_in * nb:(last_in + hist) * nb, :]

    xs[0:hist * nb, :] = xs[tc * nb:(tc + hist) * nb, :]

    xb = xc.astype(BF16)
    r = jax.nn.sigmoid(jnp.dot(xb, wa_ref[...], preferred_element_type=F32) + ba_ref[...])
    i = jax.nn.sigmoid(jnp.dot(xb, wx_ref[...], preferred_element_type=F32) + bx_ref[...])
    log_a = -RG_C * r * jax.nn.softplus(-lam_ref[...])
    a_s[...] = jnp.exp(log_a)
    th = jnp.tanh(log_a)
    b_s[...] = jnp.sqrt(-2.0 * th / (1.0 - th)) * (i * xc)

    def group(gi, _):
        g8 = gi * SUBLANES
        rows8 = pl.ds(pl.multiple_of(g8, SUBLANES), SUBLANES)

        def step_fn(t, h):
            rows = pl.ds(pl.multiple_of(t * nb + g8, SUBLANES), SUBLANES)
            h = a_s[rows, :] * h + b_s[rows, :]
            b_s[rows, :] = h
            return h

        h_ref[rows8, :] = lax.fori_loop(0, last, step_fn, h_ref[rows8, :])
        return 0

    lax.fori_loop(0, nb // SUBLANES, group, 0)
    for j in range(RG_WIDTH // LANES):
        tm_s[j] = b_s[:, j * LANES:(j + 1) * LANES]
    _to_batch_major(tm_s, bm_s, y_ref, nb, tc)
    y_ref[...] = y_ref[...] * jax.nn.gelu(z_ref[:, :, RG_WIDTH:2 * RG_WIDTH])


def _rglru(z, h0, cb0, cw, cb, wa, ba, wx, bx, lam, *, tc, t_valid):
    nb, t, _ = z.shape
    hist = RG_CONV - 1
    nc = t // tc
    assert (nc - 1) * tc + hist <= t_valid <= t
    body = functools.partial(_rglru_body, nb=nb, tc=tc, nc=nc, t_valid=t_valid)
    y, h, cbn = pl.pallas_call(
        body,
        grid=(t // tc,),
        in_specs=[pl.BlockSpec((nb, tc, 2 * RG_WIDTH), lambda c: (0, c, 0)),
                  _full((nb, RG_WIDTH)), _full((hist * nb, RG_WIDTH)),
                  _full((RG_CONV, RG_WIDTH)), _full((1, RG_WIDTH)),
                  _full((RG_WIDTH, RG_WIDTH)), _full((1, RG_WIDTH)),
                  _full((RG_WIDTH, RG_WIDTH)), _full((1, RG_WIDTH)), _full((1, RG_WIDTH))],
        out_specs=[pl.BlockSpec((nb, tc, RG_WIDTH), lambda c: (0, c, 0)),
                   _full((nb, RG_WIDTH)), _full((hist * nb, RG_WIDTH))],
        out_shape=[jax.ShapeDtypeStruct((nb, t, RG_WIDTH), F32),
                   jax.ShapeDtypeStruct((nb, RG_WIDTH), F32), jax.ShapeDtypeStruct((hist * nb, RG_WIDTH), F32)],
        scratch_shapes=[pltpu.VMEM(((tc + hist) * nb, RG_WIDTH), F32),
                        pltpu.VMEM((nb * tc, RG_WIDTH), F32), pltpu.VMEM((nb * tc, RG_WIDTH), F32),
                        pltpu.VMEM((RG_WIDTH // LANES, nb * tc, LANES), F32),
                        pltpu.VMEM((RG_WIDTH // LANES, nb * tc, LANES), F32)],
        compiler_params=_cparams(("arbitrary",)),
        name="rglru_scan",
    )(z, h0, cb0.transpose(1, 0, 2).reshape(hist * nb, RG_WIDTH), cw, cb, wa, ba, wx, bx, lam)
    return y, h, cbn.reshape(hist, nb, RG_WIDTH).transpose(1, 0, 2)


def _dot_nt(a, b, **kw):
    return lax.dot_general(a, b, (((1,), (1,)), ((), ())), preferred_element_type=F32, **kw)


def _dot_tn(a, b, **kw):
    return lax.dot_general(a, b, (((0,), (0,)), ((), ())), preferred_element_type=F32, **kw)


def _mlstm_body(q_ref, k_ref, v_ref, o_ref, g_ref, gb_ref, ng_ref, c0_ref, n0_ref, m0_ref,
                y_ref, cn_ref, nn_ref, mn_ref, *, bb, cl, t_valid):
    c = pl.program_id(1)

    @pl.when(c == 0)
    def _():
        cn_ref[...] = c0_ref[0]
        nn_ref[...] = n0_ref[...]
        mn_ref[...] = m0_ref[...]

    row = lax.broadcasted_iota(jnp.int32, (cl, cl), 0)
    col = lax.broadcasted_iota(jnp.int32, (cl, cl), 1)
    tril = (col <= row).astype(F32)
    keep = (col <= row) & (col + c * cl < t_valid)
    t_ok = (lax.broadcasted_iota(jnp.int32, (cl, LANES), 0) + c * cl) < t_valid
    lane = lax.broadcasted_iota(jnp.int32, (cl, LANES), 1)
    ones = jnp.ones((cl, LANES), F32)

    chains = [(bi, h) for bi in range(bb) for h in range(ML_HEADS)]
    hsl = lambda h: slice(h * ML_DK, (h + 1) * ML_DK)
    hi = lax.Precision.HIGHEST

    bcum, ig = [], []
    for bi in range(bb):
        graw = g_ref[bi] + gb_ref[...]
        lf = jnp.where(t_ok, jax.nn.log_sigmoid(graw), 0.0)
        bcum.append(jnp.dot(tril, lf, preferred_element_type=F32, precision=hi))
        ig.append(jnp.where(t_ok, graw, NEG_BIG))

    qf = {ch: q_ref[ch[0], :, hsl(ch[1])] for ch in chains}
    qh = {ch: qf[ch].astype(BF16) for ch in chains}
    kh = {ch: k_ref[ch[0], :, hsl(ch[1])] * (ML_DK ** -0.5) for ch in chains}
    vh = {ch: v_ref[ch[0], :, hsl(ch[1])].astype(BF16) for ch in chains}
    cm = {ch: cn_ref[ch[0], ch[1]] for ch in chains}
    nv = {ch: nn_ref[ch[0], ch[1]:ch[1] + 1, :] for ch in chains}
    m_old = {ch: mn_ref[ch[0], :, ch[1]:ch[1] + 1] for ch in chains}
    b_col = {ch: bcum[ch[0]][:, ML_HEADS + ch[1]:ML_HEADS + ch[1] + 1] for ch in chains}
    i_col = {ch: ig[ch[0]][:, ch[1]:ch[1] + 1] for ch in chains}

    w_row = {ch: _dot_nt(ones, jnp.where(lane == ch[1], i_col[ch] - b_col[ch], 0.0), precision=hi) for ch in chains}
    qk = {ch: _dot_nt(qh[ch], kh[ch].astype(BF16)) for ch in chains}
    qc = {ch: jnp.dot(qh[ch], cm[ch].astype(BF16), preferred_element_type=F32) for ch in chains}
    dmat = {ch: jnp.where(keep, b_col[ch] + w_row[ch], -jnp.inf) for ch in chains}
    dmax = {ch: jnp.max(dmat[ch], axis=-1, keepdims=True) for ch in chains}
    inter = {ch: b_col[ch] + m_old[ch] for ch in chains}
    mt = {ch: jnp.maximum(inter[ch], dmax[ch]) for ch in chains}
    s = {ch: qk[ch] * jnp.exp(dmat[ch] - mt[ch]) for ch in chains}
    sc = {ch: jnp.exp(inter[ch] - mt[ch]) for ch in chains}
    sv = {ch: jnp.dot(s[ch].astype(BF16), vh[ch], preferred_element_type=F32) for ch in chains}
    qn = {ch: jnp.sum(qf[ch] * nv[ch], axis=-1, keepdims=True) for ch in chains}
    den = {ch: jnp.sum(s[ch], axis=-1, keepdims=True) + sc[ch] * qn[ch] for ch in chains}
    hh = {ch: (sv[ch] + sc[ch] * qc[ch]) / jnp.maximum(jnp.abs(den[ch]), jnp.exp(-mt[ch])) for ch in chains}
    bl = {ch: b_col[ch][cl - 1:cl, :] for ch in chains}
    m_new = {ch: mt[ch][cl - 1:cl, :] for ch in chains}
    wk = {ch: jnp.exp(bl[ch] - b_col[ch] + i_col[ch] - m_new[ch]) * kh[ch] for ch in chains}
    decay = {ch: jnp.exp(bl[ch] + m_old[ch] - m_new[ch]) for ch in chains}
    kv = {ch: _dot_tn(wk[ch].astype(BF16), vh[ch]) for ch in chains}
    rms = {ch: lax.rsqrt(jnp.mean(hh[ch] * hh[ch], axis=-1, keepdims=True) + EPS) for ch in chains}
    for ch in chains:
        bi, h = ch
        cn_ref[bi, h] = decay[ch] * cm[ch] + kv[ch]
        nn_ref[bi, h:h + 1, :] = decay[ch] * nv[ch] + jnp.sum(wk[ch], axis=0, keepdims=True)
        mn_ref[bi, :, h:h + 1] = m_new[ch]
        y_ref[bi, :, hsl(h)] = hh[ch] * rms[ch] * ng_ref[:, hsl(h)] * jax.nn.sigmoid(o_ref[bi, :, hsl(h)])


def _mlstm(layer, zml, zgt, gbias, ng, c0, n0, m0, *, bb, cl, t_valid):
    nb, t, _ = zml.shape
    nc = t // cl
    body = functools.partial(_mlstm_body, bb=bb, cl=cl, t_valid=t_valid)
    zspec = lambda j: pl.BlockSpec((bb, cl, ML_WIDTH), lambda i, c, j=j: (i, c, j))
    return pl.pallas_call(
        body,
        grid=(nb // bb, nc),
        in_specs=[zspec(0), zspec(1), zspec(2), zspec(3),
                  pl.BlockSpec((bb, cl, LANES), lambda i, c: (i, c, 0)),
                  _full((1, LANES)), _full((1, ML_WIDTH)),
                  pl.BlockSpec((1, bb, ML_HEADS, ML_DK, ML_DV), lambda i, c: (layer, i, 0, 0, 0)),
                  pl.BlockSpec((bb, ML_HEADS, ML_DK), lambda i, c: (i, 0, 0)),
                  pl.BlockSpec((bb, 1, ML_HEADS), lambda i, c: (i, 0, 0))],
        out_specs=[pl.BlockSpec((bb, cl, ML_WIDTH), lambda i, c: (i, c, 0)),
                   pl.BlockSpec((bb, ML_HEADS, ML_DK, ML_DV), lambda i, c: (i, 0, 0, 0)),
                   pl.BlockSpec((bb, ML_HEADS, ML_DK), lambda i, c: (i, 0, 0)),
                   pl.BlockSpec((bb, 1, ML_HEADS), lambda i, c: (i, 0, 0))],
        out_shape=[jax.ShapeDtypeStruct((nb, t, ML_WIDTH), F32),
                   jax.ShapeDtypeStruct((nb, ML_HEADS, ML_DK, ML_DV), F32),
                   jax.ShapeDtypeStruct((nb, ML_HEADS, ML_DK), F32),
                   jax.ShapeDtypeStruct((nb, 1, ML_HEADS), F32)],
        compiler_params=_cparams(("arbitrary", "arbitrary")),
        name="mlstm_chunks",
    )(zml, zml, zml, zml, zgt, gbias, ng, c0, n0, m0)


def _outproj_body(x_ref, y1_ref, y2_ref, y3_ref, wo_ref, g2_ref, rw_ref, rb_ref,
                  xn_ref, hn_ref, ridx_ref, rgt_ref, cnt_ref):
    acc = jnp.dot(y1_ref[...].astype(BF16), wo_ref[0:256, :], preferred_element_type=F32)
    acc = acc + jnp.dot(y2_ref[...].astype(BF16), wo_ref[256:512, :], preferred_element_type=F32)
    acc = acc + jnp.dot(y3_ref[...].astype(BF16), wo_ref[512:1024, :], preferred_element_type=F32)
    x = x_ref[...] + acc
    xn_ref[...] = x
    h = x * lax.rsqrt(jnp.mean(x * x, axis=-1, keepdims=True) + EPS) * g2_ref[...]
    _rows_to_tiles(h, hn_ref)
    hb = h.astype(BF16)
    tm = x.shape[0]
    eidx = lax.broadcasted_iota(jnp.int32, (N_EXPERTS, ROUTE_TILE), 0)
    sub = lax.broadcasted_iota(jnp.int32, (LANES, ROUTE_TILE), 0)
    cnt = jnp.zeros((N_EXPERTS, 1), F32)
    for j in range(tm // ROUTE_TILE):
        rows = slice(j * ROUTE_TILE, (j + 1) * ROUTE_TILE)
        vals = _dot_nt(rw_ref[...], hb[rows, :]) + rb_ref[...]
        tops, idxs = [], []
        for _ in range(TOP_K):
            m = jnp.max(vals, axis=0, keepdims=True)
            idx = jnp.min(jnp.where(vals == m, eidx, N_EXPERTS), axis=0, keepdims=True)
            vals = jnp.where(eidx == idx, -jnp.inf, vals)
            tops.append(m)
            idxs.append(idx)
        exps = [jnp.exp(t - tops[0]) for t in tops]
        inv = 1.0 / functools.reduce(lambda a, b: a + b, exps)
        ridx_ref[:, rows] = jnp.concatenate(idxs + idxs, axis=0)
        for idx in idxs:
            cnt = cnt + jnp.sum((eidx == idx).astype(F32), axis=1, keepdims=True)
        gt = jnp.zeros((LANES, ROUTE_TILE), F32)
        for kk in range(TOP_K):
            gt = jnp.where(sub == kk, exps[kk] * inv, gt)
        rgt_ref[rows, :] = gt.T
    cnt_ref[...] = jnp.broadcast_to(cnt, (N_EXPERTS, LANES))


def _outproj_two_body(*refs, tiles0):
    ins0, ins1 = refs[0:4], refs[4:8]
    shared = refs[8:12]
    xn0, rgt0, xn1, rgt1, hn_ref, ridx_ref, cnt_ref = refs[12:19]
    i = pl.program_id(0)

    @pl.when(i < tiles0)
    def _():
        _outproj_body(*ins0, *shared, xn0, hn_ref, ridx_ref, rgt0, cnt_ref)

    @pl.when(i >= tiles0)
    def _():
        _outproj_body(*ins1, *shared, xn1, hn_ref, ridx_ref, rgt1, cnt_ref)


def _outproj(streams, wo, g2, rw, rb):
    (x0, *_), (x1, *_) = streams
    tm = ROW_TILE
    t0, t1 = x0.shape[0] // tm, x1.shape[0] // tm
    n_all = x0.shape[0] + x1.shape[0]
    maps = (lambda i: (jnp.minimum(i, t0 - 1), 0), lambda i: (jnp.maximum(i - t0, 0), 0))
    row_specs = lambda m: [pl.BlockSpec((tm, w_), m) for w_ in (D_MODEL, S5_WIDTH, RG_WIDTH, ML_WIDTH)]
    own_out = lambda m: [pl.BlockSpec((tm, D_MODEL), m), pl.BlockSpec((tm, LANES), m)]
    own_shape = lambda x: [jax.ShapeDtypeStruct((x.shape[0], D_MODEL), F32),
                           jax.ShapeDtypeStruct((x.shape[0], LANES), F32)]
    xn0, rgt0, xn1, rgt1, hn, ridx, cnt = pl.pallas_call(
        functools.partial(_outproj_two_body, tiles0=t0),
        grid=(t0 + t1,),
        in_specs=row_specs(maps[0]) + row_specs(maps[1])
                 + [_full((D_MODEL, D_MODEL)), _full((1, D_MODEL)), _full((N_EXPERTS, D_MODEL)), _full((N_EXPERTS, 1))],
        out_specs=own_out(maps[0]) + own_out(maps[1])
                  + [pl.BlockSpec((tm * ROW_TILES, LANES), lambda i: (i, 0)),
                     pl.BlockSpec((2 * TOP_K, tm), lambda i: (0, i)),
                     pl.BlockSpec((N_EXPERTS, LANES), lambda i: (0, i))],
        out_shape=own_shape(x0) + own_shape(x1)
                  + [jax.ShapeDtypeStruct((n_all * ROW_TILES, LANES), F32),
                     jax.ShapeDtypeStruct((2 * TOP_K, n_all), jnp.int32),
                     jax.ShapeDtypeStruct((N_EXPERTS, (t0 + t1) * LANES), F32)],
        compiler_params=_cparams(("arbitrary",)),
        name="outproj_route",
    )(*streams[0], *streams[1], wo, g2, rw, rb)
    return (xn0, xn1), (rgt0, rgt1), hn, ridx, cnt


GATHER_AHEAD = 2
N_XBUF = GATHER_AHEAD + 1
N_OBUF = 3


def _moe_body(be_ref, bv_ref, tokn_ref, posp_ref, posc_ref, hn_ref, w1_ref, b1_ref, w2_ref, b2_ref, y4_ref,
              w1b, w2b, xbuf, obuf, gsem, ssem):
    i = pl.program_id(0)
    last = pl.num_programs(0) - 1
    real = bv_ref[i] > 0
    fresh = jnp.logical_or(i == GATHER_AHEAD, be_ref[i] != be_ref[jnp.maximum(i - 1, 0)])
    xs, xn = lax.rem(i, N_XBUF), lax.rem(i + GATHER_AHEAD, N_XBUF)
    os_, op, opp = lax.rem(i, N_OBUF), lax.rem(i + N_OBUF - 1, N_OBUF), lax.rem(i + N_OBUF - 2, N_OBUF)
    x_cur, x_nxt, o_cur, o_prev = xbuf.at[xs], xbuf.at[xn], obuf.at[os_], obuf.at[op]

    def tile(ref, row):
        return ref.at[pl.ds(pl.multiple_of(row, ROW_TILES), ROW_TILES)]

    def start_gathers(rows=range(MOE_TILE)):
        for r in rows:
            pltpu.make_async_copy(tile(hn_ref, tokn_ref[0, 0, r]), tile(x_nxt, r * ROW_TILES),
                                  gsem.at[xn]).start(priority=0)

    def start_scatters(pos_ref, o_buf, sem, rows=range(MOE_TILE)):
        for r in rows:
            pltpu.make_async_copy(tile(o_buf, r * ROW_TILES), tile(y4_ref, pos_ref[0, 0, r]), sem).start(priority=1)

    def start_block_scatter(pos_ref, o_buf, sem):
        first = pl.multiple_of(pos_ref[0, 0, 0], ROW_TILES)
        pltpu.make_async_copy(o_buf, y4_ref.at[pl.ds(first, MOE_TILE * ROW_TILES)], sem).start(priority=1)

    def wait_scatters(o_buf, sem):
        pltpu.make_async_copy(o_buf, y4_ref.at[pl.ds(0, MOE_TILE * ROW_TILES)], sem).wait()

    gathered = jnp.logical_and(i >= GATHER_AHEAD, jnp.logical_or(
        i < 2 * GATHER_AHEAD, bv_ref[jnp.maximum(i - GATHER_AHEAD, 0)] > 0))

    @pl.when(gathered)
    def _():
        pltpu.make_async_copy(hn_ref.at[pl.ds(0, MOE_TILE * ROW_TILES)], x_cur, gsem.at[xs]).wait()

    @pl.when(i >= N_OBUF)
    def _():
        wait_scatters(o_cur, ssem.at[os_])

    @pl.when(jnp.logical_and(fresh, real))
    def _():
        w1b[...] = w1_ref[0, 0].astype(BF16)
        w2b[...] = w2_ref[0, 0].astype(BF16)

    @pl.when(bv_ref[i] >= 1)
    def _():
        start_gathers()

    @pl.when(real)
    def _():
        xb = _tiles_to_rows(x_cur).astype(BF16)
        start_scatters(posp_ref, o_prev, ssem.at[op])
        hb = jnp.dot(xb, w1b[...], preferred_element_type=F32) + b1_ref[0, 0]
        g = jnp.minimum(hb[:, :D_FF], SWIGLU_LIMIT)
        u = jnp.clip(hb[:, D_FF:], -SWIGLU_LIMIT, SWIGLU_LIMIT)
        act = g * jax.nn.sigmoid(SWIGLU_ALPHA * g) * (u + 1.0)
        _rows_to_tiles(jnp.dot(act.astype(BF16), w2b[...], preferred_element_type=F32) + b2_ref[0, 0], o_cur)

    @pl.when(jnp.logical_not(real))
    def _():
        @pl.when(i == 0)
        def _():
            obuf[...] = jnp.zeros_like(obuf)

        @pl.when(i < GATHER_AHEAD)
        def _():
            start_gathers()

        prev_real = bv_ref[jnp.maximum(i - 1, 0)] > 0

        @pl.when(jnp.logical_and(i >= 1, prev_real))
        def _():
            start_scatters(posp_ref, o_prev, ssem.at[op])

        @pl.when(jnp.logical_and(i >= 1, jnp.logical_not(prev_real)))
        def _():
            start_block_scatter(posp_ref, o_prev, ssem.at[op])

    @pl.when(i == last)
    def _():
        start_block_scatter(posc_ref, o_cur, ssem.at[os_])
        wait_scatters(obuf.at[opp], ssem.at[opp])
        wait_scatters(o_prev, ssem.at[op])
        wait_scatters(o_cur, ssem.at[os_])


def _moe(layer, block_e, block_v, slot_tok, slot_pos, hn, w1, b1, w2, b2):
    n_blocks = block_e.shape[0]
    tm = MOE_TILE
    n_rows = n_blocks * tm
    assert n_blocks > 2 * GATHER_AHEAD
    idx_spec = lambda f: pl.BlockSpec((1, 1, tm), f, memory_space=pltpu.SMEM)
    grid_spec = pltpu.PrefetchScalarGridSpec(
        num_scalar_prefetch=2,
        grid=(n_blocks,),
        in_specs=[idx_spec(lambda i, be, bv: (jnp.minimum(i + GATHER_AHEAD, n_blocks - 1), 0, 0)),
                  idx_spec(lambda i, be, bv: (jnp.maximum(i - 1, 0), 0, 0)),
                  idx_spec(lambda i, be, bv: (i, 0, 0)),
                  pl.BlockSpec(memory_space=pl.ANY),
                  pl.BlockSpec((1, 1, D_MODEL, 2 * D_FF), lambda i, be, bv: (layer, be[i], 0, 0)),
                  pl.BlockSpec((1, 1, 1, 2 * D_FF), lambda i, be, bv: (layer, be[i], 0, 0)),
                  pl.BlockSpec((1, 1, D_FF, D_MODEL), lambda i, be, bv: (layer, be[i], 0, 0)),
                  pl.BlockSpec((1, 1, 1, D_MODEL), lambda i, be, bv: (layer, be[i], 0, 0))],
        out_specs=pl.BlockSpec(memory_space=pl.ANY),
        scratch_shapes=[pltpu.VMEM((D_MODEL, 2 * D_FF), BF16), pltpu.VMEM((D_FF, D_MODEL), BF16)]
                       + [pltpu.VMEM((N_XBUF, tm * ROW_TILES, LANES), F32),
                          pltpu.VMEM((N_OBUF, tm * ROW_TILES, LANES), F32)]
                       + [pltpu.SemaphoreType.DMA((N_XBUF,)), pltpu.SemaphoreType.DMA((N_OBUF,))],
    )
    return pl.pallas_call(
        _moe_body,
        grid_spec=grid_spec,
        out_shape=jax.ShapeDtypeStruct((n_rows * ROW_TILES, LANES), F32),
        compiler_params=_cparams(("arbitrary",)),
        name="moe_experts",
    )(block_e, block_v, slot_tok, slot_pos, slot_pos, hn, w1, b1, w2, b2)


def _combine_body(x_ref, rgt_ref, fg_ref, *rest, final):
    y_refs, o_ref = rest[:TOP_K], rest[TOP_K]
    x = x_ref[...]
    for kk in range(TOP_K):
        x = x + rgt_ref[:, kk:kk + 1] * _tiles_to_rows(y_refs[kk])
    if final:
        x = x * lax.rsqrt(jnp.mean(x * x, axis=-1, keepdims=True) + EPS) * fg_ref[...]
    o_ref[...] = x


def _combine(x, y4, rgt, fg, *, tok0, n_all, final):
    n = x.shape[0]
    tm = ROW_TILE
    assert tok0 % tm == 0 and n_all % tm == 0
    yspec = lambda kk: pl.BlockSpec((tm * ROW_TILES, LANES), lambda i, kk=kk: ((kk * n_all + tok0) // tm + i, 0))
    return pl.pallas_call(
        functools.partial(_combine_body, final=final),
        grid=(n // tm,),
        in_specs=[pl.BlockSpec((tm, D_MODEL), lambda i: (i, 0)),
                  pl.BlockSpec((tm, LANES), lambda i: (i, 0)), _full((1, D_MODEL))]
                 + [yspec(kk) for kk in range(TOP_K)],
        out_specs=pl.BlockSpec((tm, D_MODEL), lambda i: (i, 0)),
        out_shape=jax.ShapeDtypeStruct((n, D_MODEL), F32),
        compiler_params=_cparams(("arbitrary",)),
        name="moe_combine",
    )(x, rgt, fg, *([y4] * TOP_K))


def _routing_tables(ridx, n, counts):
    na = n * TOP_K
    tm = MOE_TILE
    assert na % tm == 0
    lead = GATHER_AHEAD * tm
    n_blocks = na // tm + N_EXPERTS + 2 * GATHER_AHEAD
    n_slots = n_blocks * tm
    experts = jnp.arange(N_EXPERTS, dtype=jnp.int32)
    flat_e = ridx[:TOP_K, :].reshape(-1)
    order = jnp.argsort(flat_e).astype(jnp.int32)
    padded = (counts + tm - 1) // tm * tm
    pad_end = jnp.cumsum(padded)
    pad_start = pad_end - padded
    start = jnp.cumsum(counts) - counts
    spare_start = na + pad_start - start
    total = pad_end[-1]
    blk = jnp.arange(n_blocks, dtype=jnp.int32) * tm - lead
    block_v = ((blk >= 0) & (blk < total)).astype(jnp.int32)
    last_e = jnp.max(jnp.where(counts > 0, experts, 0))
    block_e = jnp.minimum(jnp.sum(blk[:, None] >= pad_end[None, :], axis=1), last_e).astype(jnp.int32)
    rank = (blk - pad_start[block_e])[:, None] + jnp.arange(tm, dtype=jnp.int32)[None, :]
    cnt = counts[block_e][:, None]
    slot = blk[:, None] + jnp.arange(tm, dtype=jnp.int32)[None, :]
    used = (slot >= 0) & (slot < total)
    real = (rank < cnt) & used
    src = order[jnp.clip(start[block_e][:, None] + rank, 0, na - 1)]
    slot_tok = (jnp.where(real, src % n, 0) * ROW_TILES).astype(jnp.int32)
    spare = jnp.where(used, spare_start[block_e][:, None] + rank - cnt, jnp.where(slot >= 0, slot, n_slots + slot))
    slot_pos = (jnp.where(real, src, spare) * ROW_TILES).astype(jnp.int32)
    return block_e, block_v, slot_tok.reshape(n_blocks, 1, tm), slot_pos.reshape(n_blocks, 1, tm)


def _blockdiag(m):
    g, a, b = m.shape
    eye = jnp.eye(g, dtype=m.dtype)
    return (eye[:, None, :, None] * m[:, :, None, :]).reshape(g * a, g * b)


def _layer_params(l, p):
    lam = jnp.zeros((SUBLANES, S5_FLAT), F32)
    lam = lam.at[0].set(p['s5_lambda_re'][l].reshape(-1)).at[1].set(p['s5_lambda_im'][l].reshape(-1))
    lam = lam.at[2].set(jnp.repeat(p['s5_log_step'][l], S5_STATE))
    gbias = jnp.zeros((1, LANES), F32).at[0, :2 * ML_HEADS].set(p['ml_gate_bias'][l])
    w_in = jnp.pad(p['w_in'][l], ((0, 0), (0, N_IN_PAD - N_IN))).astype(BF16)
    return dict(
        norm1_g=p['norm1_g'][l].reshape(1, -1), w_in=w_in, s5_lam=lam,
        s5_bre=_blockdiag(p['s5_b_re'][l].transpose(0, 2, 1)), s5_bim=_blockdiag(p['s5_b_im'][l].transpose(0, 2, 1)),
        s5_cre=_blockdiag(p['s5_c_re'][l].transpose(0, 2, 1)).astype(BF16),
        s5_cim=_blockdiag(p['s5_c_im'][l].transpose(0, 2, 1)).astype(BF16),
        s5_d=p['s5_d'][l].reshape(1, -1), s5_gw=p['s5_glu_w'][l].astype(BF16), s5_gb=p['s5_glu_b'][l].reshape(1, -1),
        rg_cw=p['rg_conv_w'][l], rg_cb=p['rg_conv_b'][l].reshape(1, -1),
        rg_wa=_blockdiag(p['rg_wa'][l]).astype(BF16), rg_ba=p['rg_ba'][l].reshape(1, -1),
        rg_wx=_blockdiag(p['rg_wx'][l]).astype(BF16), rg_bx=p['rg_bx'][l].reshape(1, -1),
        rg_lam=p['rg_lambda'][l].reshape(1, -1),
        ml_gb=gbias, ml_ng=p['ml_norm_g'][l].reshape(1, -1),
        w_out=p['w_out'][l].astype(BF16), norm2_g=p['norm2_g'][l].reshape(1, -1),
        router_w=p['router_w'][l].T.astype(BF16), router_b=p['router_b'][l].reshape(-1, 1),
    )


def _mix(l, lp, xf, states, cfg):
    nb, t, t_valid = cfg['nb'], cfg['t'], cfg['t_valid']
    n = nb * t
    s5r, s5i, rgh, rgc, mlc, mln, mlm = states
    zs5, zrg, zml, zgt = _inproj(xf, lp['norm1_g'], lp['w_in'])
    y1, nr, ni = _s5(zs5.reshape(nb, t, -1), s5r[l].reshape(nb, -1), s5i[l].reshape(nb, -1), lp['s5_lam'],
                     lp['s5_bre'], lp['s5_bim'], lp['s5_cre'], lp['s5_cim'], lp['s5_d'], lp['s5_gw'], lp['s5_gb'],
                     tc=cfg['s5_tc'], t_valid=t_valid)
    y2, nh, ncv = _rglru(zrg.reshape(nb, t, -1), rgh[l], rgc[l], lp['rg_cw'], lp['rg_cb'], lp['rg_wa'],
                         lp['rg_ba'], lp['rg_wx'], lp['rg_bx'], lp['rg_lam'], tc=cfg['s5_tc'], t_valid=t_valid)
    y3, nc_, nn_, nm_ = _mlstm(l, zml.reshape(nb, t, -1), zgt.reshape(nb, t, -1), lp['ml_gb'], lp['ml_ng'],
                               mlc, mln[l], mlm[l].reshape(nb, 1, ML_HEADS),
                               bb=cfg['ml_bb'], cl=cfg['ml_cl'], t_valid=t_valid)
    new_states = (nr.reshape(nb, S5_GROUPS, S5_STATE), ni.reshape(nb, S5_GROUPS, S5_STATE), nh, ncv, nc_, nn_,
                  nm_.reshape(nb, ML_HEADS))
    return (xf, y1.reshape(n, -1), y2.reshape(n, -1), y3.reshape(n, -1)), new_states


def _trunks(xs, states, cfgs, params, final_g):
    n_all = sum(x.shape[0] for x in xs)
    tok0 = [sum(x.shape[0] for x in xs[:s]) for s in range(len(xs))]
    outs = [[[] for _ in range(7)] for _ in xs]
    b1 = params['exp_b1'].reshape(DEPTH, N_EXPERTS, 1, -1)
    b2 = params['exp_b2'].reshape(DEPTH, N_EXPERTS, 1, -1)
    for l in range(DEPTH):
        lp = _layer_params(l, params)
        mixed = [_mix(l, lp, xs[s], states[s], cfgs[s]) for s in range(len(xs))]
        xn, rgt, hn_all, ridx_all, cnt = _outproj([m[0] for m in mixed], lp['w_out'], lp['norm2_g'],
                                                  lp['router_w'], lp['router_b'])
        counts = jnp.sum(cnt[:, ::LANES], axis=1).astype(jnp.int32)
        block_e, block_v, slot_tok, slot_pos = _routing_tables(ridx_all, n_all, counts)
        y4 = _moe(l, block_e, block_v, slot_tok, slot_pos, hn_all, params['exp_w1'], b1, params['exp_w2'], b2)
        xs = [_combine(xn[s], y4, rgt[s], final_g.reshape(1, -1), tok0=tok0[s], n_all=n_all,
                       final=(l == DEPTH - 1)) for s in range(len(xs))]
        for s in range(len(xs)):
            for lst, val in zip(outs[s], mixed[s][1]):
                lst.append(val)
    return xs, [[jnp.stack(v) for v in o] for o in outs]


def kernel(x_prompt, x_sample, state_s5_re, state_s5_im, state_rg_h, state_rg_conv, state_ml_c, state_ml_n, state_ml_m, norm1_g, w_in, s5_lambda_re, s5_lambda_im, s5_log_step, s5_b_re, s5_b_im, s5_c_re, s5_c_im, s5_d, s5_glu_w, s5_glu_b, rg_conv_w, rg_conv_b, rg_wa, rg_ba, rg_wx, rg_bx, rg_lambda, ml_gate_bias, ml_norm_g, w_out, norm2_g, router_w, router_b, exp_w1, exp_b1, exp_w2, exp_b2, final_norm_g):
    params = dict(norm1_g=norm1_g, w_in=w_in, s5_lambda_re=s5_lambda_re, s5_lambda_im=s5_lambda_im,
                  s5_log_step=s5_log_step, s5_b_re=s5_b_re, s5_b_im=s5_b_im, s5_c_re=s5_c_re, s5_c_im=s5_c_im,
                  s5_d=s5_d, s5_glu_w=s5_glu_w, s5_glu_b=s5_glu_b, rg_conv_w=rg_conv_w, rg_conv_b=rg_conv_b,
                  rg_wa=rg_wa, rg_ba=rg_ba, rg_wx=rg_wx, rg_bx=rg_bx, rg_lambda=rg_lambda,
                  ml_gate_bias=ml_gate_bias, ml_norm_g=ml_norm_g, w_out=w_out, norm2_g=norm2_g,
                  router_w=router_w, router_b=router_b, exp_w1=exp_w1, exp_b1=exp_b1, exp_w2=exp_w2, exp_b2=exp_b2)
    bp, tp, _ = x_prompt.shape
    bs, ts, _ = x_sample.shape
    assert tp >= RG_CONV - 1 and ts >= RG_CONV - 1
    zeros = lambda *shape: jnp.zeros((DEPTH, bp) + shape, F32)
    prompt_states = (zeros(S5_GROUPS, S5_STATE), zeros(S5_GROUPS, S5_STATE), zeros(RG_WIDTH),
                     zeros(RG_CONV - 1, RG_WIDTH), zeros(ML_HEADS, ML_DK, ML_DV), zeros(ML_HEADS, ML_DK),
                     zeros(ML_HEADS))
    ts_pad = -(-ts // SUBLANES) * SUBLANES
    xs_pad = jnp.pad(x_sample, ((0, 0), (0, ts_pad - ts), (0, 0)))
    sample_states = (state_s5_re, state_s5_im, state_rg_h, state_rg_conv, state_ml_c, state_ml_n, state_ml_m)
    cfgs = [dict(nb=bp, t=tp, t_valid=tp, s5_tc=128, ml_bb=8, ml_cl=math.gcd(tp, ML_CHUNK_LEN)),
            dict(nb=bs, t=ts_pad, t_valid=ts, s5_tc=ts_pad, ml_bb=8, ml_cl=ts_pad)]
    (yp, ys), (sp, ss) = _trunks([x_prompt.reshape(bp * tp, D_MODEL), xs_pad.reshape(bs * ts_pad, D_MODEL)],
                                 [prompt_states, sample_states], cfgs, params, final_norm_g)
    return (yp.reshape(bp, tp, D_MODEL), ys.reshape(bs, ts_pad, D_MODEL)[:, :ts], *sp, *ss)
```

```python
import functools
import math

import jax
import jax.numpy as jnp
from jax import lax
from jax.experimental import pallas as pl
from jax.experimental.pallas import tpu as pltpu

F32 = jnp.float32
BF16 = jnp.bfloat16

SUBLANES = 8
LANES = 128
VMEM_LIMIT_BYTES = 56 * 1024 * 1024

D_MODEL = 1024
DEPTH = 2
S5_WIDTH = 256
S5_GROUP = 16
S5_GROUPS = 16
S5_STATE = 64
S5_FLAT = S5_GROUPS * S5_STATE
RG_WIDTH = 256
RG_BLOCKS = 8
RG_CONV = 4
RG_C = 8.0
ML_WIDTH = 512
ML_HEADS = 4
ML_DK = 128
ML_DV = 128
ML_CHUNK_LEN = 128
N_EXPERTS = 32
TOP_K = 4
D_FF = 1024
SWIGLU_LIMIT = 7.0
SWIGLU_ALPHA = 1.702
EPS = 1e-5

IN_S5 = (0, 256)
IN_RG = (256, 768)
IN_ML = (768, 2816)
IN_GATE = (2816, 2944)
N_IN = 2824
N_IN_PAD = 2944

ROW_TILE = 512
ROUTE_TILE = 128
MOE_TILE = 256
NEG_BIG = -1e30


def _cparams(sem):
    return pltpu.CompilerParams(dimension_semantics=sem, vmem_limit_bytes=VMEM_LIMIT_BYTES)


def _full(shape):
    n = len(shape)
    return pl.BlockSpec(shape, lambda *_: (0,) * n)


ROW_TILES = D_MODEL // LANES
assert ROW_TILES == SUBLANES


def _rows_to_tiles(x, dst_ref):
    m = x.shape[0]
    for j in range(ROW_TILES):
        dst_ref[pl.ds(j, m, stride=ROW_TILES), :] = x[:, j * LANES:(j + 1) * LANES]


def _tiles_to_rows(src_ref):
    m = src_ref.shape[0] // ROW_TILES
    return jnp.concatenate([src_ref[pl.ds(j, m, stride=ROW_TILES), :] for j in range(ROW_TILES)], axis=-1)


def _inproj_body(x_ref, g_ref, w_ref, zs5_ref, zrg_ref, zml_ref, zgt_ref):
    x = x_ref[...]
    h = x * lax.rsqrt(jnp.mean(x * x, axis=-1, keepdims=True) + EPS) * g_ref[...]
    hb = h.astype(BF16)
    for ref, (lo, hi) in ((zs5_ref, IN_S5), (zrg_ref, IN_RG), (zml_ref, IN_ML), (zgt_ref, IN_GATE)):
        ref[...] = jnp.dot(hb, w_ref[:, lo:hi], preferred_element_type=F32)


def _inproj(x, g, w):
    n = x.shape[0]
    tm = ROW_TILE
    widths = [hi - lo for lo, hi in (IN_S5, IN_RG, IN_ML, IN_GATE)]
    return pl.pallas_call(
        _inproj_body,
        grid=(n // tm,),
        in_specs=[pl.BlockSpec((tm, D_MODEL), lambda i: (i, 0)), _full((1, D_MODEL)), _full((D_MODEL, N_IN_PAD))],
        out_specs=[pl.BlockSpec((tm, w_), lambda i: (i, 0)) for w_ in widths],
        out_shape=[jax.ShapeDtypeStruct((n, w_), F32) for w_ in widths],
        compiler_params=_cparams(("arbitrary",)),
        name="inproj",
    )(x, g, w)


def _s5_body(u_ref, s0r_ref, s0i_ref, lam_ref, bre_ref, bim_ref, cre_ref, cim_ref, d_ref, gw_ref, gb_ref,
             y_ref, sr_ref, si_ref, bur, bui, ab, bbr, bbi, tm_s, bm_s, *, nb, tc, nc, t_valid):
    c = pl.program_id(0)

    @pl.when(c == 0)
    def _():
        sr_ref[...] = s0r_ref[...]
        si_ref[...] = s0i_ref[...]
        lr = lam_ref[0:1, :]
        li = lam_ref[1:2, :]
        step = jnp.exp(lam_ref[2:3, :])
        mag = jnp.exp(lr * step)
        ab_re = mag * jnp.cos(li * step)
        ab_im = mag * jnp.sin(li * step)
        den = lr * lr + li * li
        num_re = ab_re - 1.0
        z_re = (num_re * lr + ab_im * li) / den
        z_im = (ab_im * lr - num_re * li) / den
        ab[0:1, :] = ab_re
        ab[1:2, :] = ab_im
        bbr[...] = (z_re * bre_ref[...] - z_im * bim_ref[...]).astype(BF16)
        bbi[...] = (z_re * bim_ref[...] + z_im * bre_ref[...]).astype(BF16)

    _to_time_major(u_ref, tm_s, bm_s, nb, tc)
    u = jnp.concatenate([tm_s[j] for j in range(S5_WIDTH // LANES)], axis=-1)
    ub = u.astype(BF16)
    bur[...] = jnp.dot(ub, bbr[...], preferred_element_type=F32)
    bui[...] = jnp.dot(ub, bbi[...], preferred_element_type=F32)

    a_re = jnp.broadcast_to(ab[0:1, :], (SUBLANES, S5_FLAT))
    a_im = jnp.broadcast_to(ab[1:2, :], (SUBLANES, S5_FLAT))
    last_in = t_valid - (nc - 1) * tc
    steps = tc if last_in == tc else jnp.where(c == nc - 1, last_in, tc)

    def group(gi, _):
        g8 = gi * SUBLANES
        rows8 = pl.ds(pl.multiple_of(g8, SUBLANES), SUBLANES)

        def step_fn(t, carry):
            s_re, s_im = carry
            rows = pl.ds(pl.multiple_of(t * nb + g8, SUBLANES), SUBLANES)
            n_re = a_re * s_re - a_im * s_im + bur[rows, :]
            n_im = a_re * s_im + a_im * s_re + bui[rows, :]
            bur[rows, :] = n_re
            bui[rows, :] = n_im
            return n_re, n_im

        s_re, s_im = lax.fori_loop(0, steps, step_fn, (sr_ref[rows8, :], si_ref[rows8, :]))
        sr_ref[rows8, :] = s_re
        si_ref[rows8, :] = s_im
        return 0

    lax.fori_loop(0, nb // SUBLANES, group, 0)

    y = (jnp.dot(bur[...].astype(BF16), cre_ref[...], preferred_element_type=F32)
         - jnp.dot(bui[...].astype(BF16), cim_ref[...], preferred_element_type=F32))
    y = y + d_ref[...] * u
    g = jax.nn.gelu(y)
    gate = jax.nn.sigmoid(jnp.dot(g.astype(BF16), gw_ref[...], preferred_element_type=F32) + gb_ref[...])
    out = g * gate
    for j in range(S5_WIDTH // LANES):
        tm_s[j] = out[:, j * LANES:(j + 1) * LANES]
    _to_batch_major(tm_s, bm_s, y_ref, nb, tc)


def _to_time_major(src_ref, tm_s, bm_s, nb, tc):
    for j in range(tm_s.shape[0]):
        lanes = slice(j * LANES, (j + 1) * LANES)
        if nb <= tc:
            for b in range(nb):
                tm_s[j, pl.ds(b, tc, stride=nb), :] = src_ref[b, :, lanes]
        else:
            bm_s[j] = src_ref[:, :, lanes].reshape(nb * tc, LANES)
            for t in range(tc):
                tm_s[j, t * nb:(t + 1) * nb, :] = bm_s[j, pl.ds(t, nb, stride=tc), :]


def _to_batch_major(tm_s, bm_s, dst_ref, nb, tc):
    for j in range(tm_s.shape[0]):
        lanes = slice(j * LANES, (j + 1) * LANES)
        if nb <= tc:
            for b in range(nb):
                dst_ref[b, :, lanes] = tm_s[j, pl.ds(b, tc, stride=nb), :]
        else:
            for t in range(tc):
                bm_s[j, pl.ds(t, nb, stride=tc), :] = tm_s[j, t * nb:(t + 1) * nb, :]
            dst_ref[:, :, lanes] = bm_s[j].reshape(nb, tc, LANES)


def _s5(u, s0r, s0i, lam, bre, bim, cre, cim, d, gw, gb, *, tc, t_valid):
    nb, t, _ = u.shape
    nc = t // tc
    assert (nc - 1) * tc < t_valid <= t
    body = functools.partial(_s5_body, nb=nb, tc=tc, nc=nc, t_valid=t_valid)
    return pl.pallas_call(
        body,
        grid=(t // tc,),
        in_specs=[pl.BlockSpec((nb, tc, S5_WIDTH), lambda c: (0, c, 0)),
                  _full((nb, S5_FLAT)), _full((nb, S5_FLAT)), _full((SUBLANES, S5_FLAT)),
                  _full((S5_WIDTH, S5_FLAT)), _full((S5_WIDTH, S5_FLAT)),
                  _full((S5_FLAT, S5_WIDTH)), _full((S5_FLAT, S5_WIDTH)),
                  _full((1, S5_WIDTH)), _full((S5_WIDTH, S5_WIDTH)), _full((1, S5_WIDTH))],
        out_specs=[pl.BlockSpec((nb, tc, S5_WIDTH), lambda c: (0, c, 0)),
                   _full((nb, S5_FLAT)), _full((nb, S5_FLAT))],
        out_shape=[jax.ShapeDtypeStruct((nb, t, S5_WIDTH), F32),
                   jax.ShapeDtypeStruct((nb, S5_FLAT), F32), jax.ShapeDtypeStruct((nb, S5_FLAT), F32)],
        scratch_shapes=[pltpu.VMEM((nb * tc, S5_FLAT), F32), pltpu.VMEM((nb * tc, S5_FLAT), F32),
                        pltpu.VMEM((SUBLANES, S5_FLAT), F32),
                        pltpu.VMEM((S5_WIDTH, S5_FLAT), BF16), pltpu.VMEM((S5_WIDTH, S5_FLAT), BF16),
                        pltpu.VMEM((S5_WIDTH // LANES, nb * tc, LANES), F32),
                        pltpu.VMEM((S5_WIDTH // LANES, nb * tc, LANES), F32)],
        compiler_params=_cparams(("arbitrary",)),
        name="s5_scan",
    )(u, s0r, s0i, lam, bre, bim, cre, cim, d, gw, gb)


def _rglru_body(z_ref, h0_ref, cb0_ref, cw_ref, cb_ref, wa_ref, ba_ref, wx_ref, bx_ref, lam_ref,
                y_ref, h_ref, cbn_ref, xs, a_s, b_s, tm_s, bm_s, *, nb, tc, nc, t_valid):
    c = pl.program_id(0)
    hist = RG_CONV - 1

    @pl.when(c == 0)
    def _():
        h_ref[...] = h0_ref[...]
        xs[0:hist * nb, :] = cb0_ref[...]

    _to_time_major(z_ref, tm_s, bm_s, nb, tc)
    xs[hist * nb:(hist + tc) * nb, :] = jnp.concatenate([tm_s[j] for j in range(RG_WIDTH // LANES)], axis=-1)

    xc = jnp.zeros((tc * nb, RG_WIDTH), F32) + cb_ref[...]
    for j in range(RG_CONV):
        xc = xc + xs[j * nb:(j + tc) * nb, :] * cw_ref[j:j + 1, :]

    last_in = t_valid - (nc - 1) * tc
    last = tc if last_in == tc else jnp.where(c == nc - 1, last_in, tc)

    @pl.when(c == nc - 1)
    def _():
        cbn_ref[...] = xs[last_in * nb:(last_in + hist) * nb, :]

    xs[0:hist * nb, :] = xs[tc * nb:(tc + hist) * nb, :]

    xb = xc.astype(BF16)
    r = jax.nn.sigmoid(jnp.dot(xb, wa_ref[...], preferred_element_type=F32) + ba_ref[...])
    i = jax.nn.sigmoid(jnp.dot(xb, wx_ref[...], preferred_element_type=F32) + bx_ref[...])
    log_a = -RG_C * r * jax.nn.softplus(-lam_ref[...])
    a_s[...] = jnp.exp(log_a)
    th = jnp.tanh(log_a)
    b_s[...] = jnp.sqrt(-2.0 * th / (1.0 - th)) * (i * xc)

    def group(gi, _):
        g8 = gi * SUBLANES
        rows8 = pl.ds(pl.multiple_of(g8, SUBLANES), SUBLANES)

        def step_fn(t, h):
            rows = pl.ds(pl.multiple_of(t * nb + g8, SUBLANES), SUBLANES)
            h = a_s[rows, :] * h + b_s[rows, :]
            b_s[rows, :] = h
            return h

        h_ref[rows8, :] = lax.fori_loop(0, last, step_fn, h_ref[rows8, :])
        return 0

    lax.fori_loop(0, nb // SUBLANES, group, 0)
    for j in range(RG_WIDTH // LANES):
        tm_s[j] = b_s[:, j * LANES:(j + 1) * LANES]
    _to_batch_major(tm_s, bm_s, y_ref, nb, tc)
    y_ref[...] = y_ref[...] * jax.nn.gelu(z_ref[:, :, RG_WIDTH:2 * RG_WIDTH])


def _rglru(z, h0, cb0, cw, cb, wa, ba, wx, bx, lam, *, tc, t_valid):
    nb, t, _ = z.shape
    hist = RG_CONV - 1
    nc = t // tc
    assert (nc - 1) * tc + hist <= t_valid <= t
    body = functools.partial(_rglru_body, nb=nb, tc=tc, nc=nc, t_valid=t_valid)
    y, h, cbn = pl.pallas_call(
        body,
        grid=(t // tc,),
        in_specs=[pl.BlockSpec((nb, tc, 2 * RG_WIDTH), lambda c: (0, c, 0)),
                  _full((nb, RG_WIDTH)), _full((hist * nb, RG_WIDTH)),
                  _full((RG_CONV, RG_WIDTH)), _full((1, RG_WIDTH)),
                  _full((RG_WIDTH, RG_WIDTH)), _full((1, RG_WIDTH)),
                  _full((RG_WIDTH, RG_WIDTH)), _full((1, RG_WIDTH)), _full((1, RG_WIDTH))],
        out_specs=[pl.BlockSpec((nb, tc, RG_WIDTH), lambda c: (0, c, 0)),
                   _full((nb, RG_WIDTH)), _full((hist * nb, RG_WIDTH))],
        out_shape=[jax.ShapeDtypeStruct((nb, t, RG_WIDTH), F32),
                   jax.ShapeDtypeStruct((nb, RG_WIDTH), F32), jax.ShapeDtypeStruct((hist * nb, RG_WIDTH), F32)],
        scratch_shapes=[pltpu.VMEM(((tc + hist) * nb, RG_WIDTH), F32),
                        pltpu.VMEM((nb * tc, RG_WIDTH), F32), pltpu.VMEM((nb * tc, RG_WIDTH), F32),
                        pltpu.VMEM((RG_WIDTH // LANES, nb * tc, LANES), F32),
                        pltpu.VMEM((RG_WIDTH // LANES, nb * tc, LANES), F32)],
        compiler_params=_cparams(("arbitrary",)),
        name="rglru_scan",
    )(z, h0, cb0.transpose(1, 0, 2).reshape(hist * nb, RG_WIDTH), cw, cb, wa, ba, wx, bx, lam)
    return y, h, cbn.reshape(hist, nb, RG_WIDTH).transpose(1, 0, 2)


def _dot_nt(a, b, **kw):
    return lax.dot_general(a, b, (((1,), (1,)), ((), ())), preferred_element_type=F32, **kw)


def _dot_tn(a, b, **kw):
    return lax.dot_general(a, b, (((0,), (0,)), ((), ())), preferred_element_type=F32, **kw)


def _mlstm_body(q_ref, k_ref, v_ref, o_ref, g_ref, gb_ref, ng_ref, c0_ref, n0_ref, m0_ref,
                y_ref, cn_ref, nn_ref, mn_ref, *, bb, cl, t_valid):
    c = pl.program_id(1)

    @pl.when(c == 0)
    def _():
        cn_ref[...] = c0_ref[0]
        nn_ref[...] = n0_ref[...]
        mn_ref[...] = m0_ref[...]

    row = lax.broadcasted_iota(jnp.int32, (cl, cl), 0)
    col = lax.broadcasted_iota(jnp.int32, (cl, cl), 1)
    tril = (col <= row).astype(F32)
    keep = (col <= row) & (col + c * cl < t_valid)
    t_ok = (lax.broadcasted_iota(jnp.int32, (cl, LANES), 0) + c * cl) < t_valid
    lane = lax.broadcasted_iota(jnp.int32, (cl, LANES), 1)
    ones = jnp.ones((cl, LANES), F32)

    chains = [(bi, h) for bi in range(bb) for h in range(ML_HEADS)]
    hsl = lambda h: slice(h * ML_DK, (h + 1) * ML_DK)
    hi = lax.Precision.HIGHEST

    bcum, ig = [], []
    for bi in range(bb):
        graw = g_ref[bi] + gb_ref[...]
        lf = jnp.where(t_ok, jax.nn.log_sigmoid(graw), 0.0)
        bcum.append(jnp.dot(tril, lf, preferred_element_type=F32, precision=hi))
        ig.append(jnp.where(t_ok, graw, NEG_BIG))

    qf = {ch: q_ref[ch[0], :, hsl(ch[1])] for ch in chains}
    qh = {ch: qf[ch].astype(BF16) for ch in chains}
    kh = {ch: k_ref[ch[0], :, hsl(ch[1])] * (ML_DK ** -0.5) for ch in chains}
    vh = {ch: v_ref[ch[0], :, hsl(ch[1])].astype(BF16) for ch in chains}
    cm = {ch: cn_ref[ch[0], ch[1]] for ch in chains}
    nv = {ch: nn_ref[ch[0], ch[1]:ch[1] + 1, :] for ch in chains}
    m_old = {ch: mn_ref[ch[0], :, ch[1]:ch[1] + 1] for ch in chains}
    b_col = {ch: bcum[ch[0]][:, ML_HEADS + ch[1]:ML_HEADS + ch[1] + 1] for ch in chains}
    i_col = {ch: ig[ch[0]][:, ch[1]:ch[1] + 1] for ch in chains}

    w_row = {ch: _dot_nt(ones, jnp.where(lane == ch[1], i_col[ch] - b_col[ch], 0.0), precision=hi) for ch in chains}
    qk = {ch: _dot_nt(qh[ch], kh[ch].astype(BF16)) for ch in chains}
    qc = {ch: jnp.dot(qh[ch], cm[ch].astype(BF16), preferred_element_type=F32) for ch in chains}
    dmat = {ch: jnp.where(keep, b_col[ch] + w_row[ch], -jnp.inf) for ch in chains}
    dmax = {ch: jnp.max(dmat[ch], axis=-1, keepdims=True) for ch in chains}
    inter = {ch: b_col[ch] + m_old[ch] for ch in chains}
    mt = {ch: jnp.maximum(inter[ch], dmax[ch]) for ch in chains}
    s = {ch: qk[ch] * jnp.exp(dmat[ch] - mt[ch]) for ch in chains}
    sc = {ch: jnp.exp(inter[ch] - mt[ch]) for ch in chains}
    sv = {ch: jnp.dot(s[ch].astype(BF16), vh[ch], preferred_element_type=F32) for ch in chains}
    qn = {ch: jnp.sum(qf[ch] * nv[ch], axis=-1, keepdims=True) for ch in chains}
    den = {ch: jnp.sum(s[ch], axis=-1, keepdims=True) + sc[ch] * qn[ch] for ch in chains}
    hh = {ch: (sv[ch] + sc[ch] * qc[ch]) / jnp.maximum(jnp.abs(den[ch]), jnp.exp(-mt[ch])) for ch in chains}
    bl = {ch: b_col[ch][cl - 1:cl, :] for ch in chains}
    m_new = {ch: mt[ch][cl - 1:cl, :] for ch in chains}
    wk = {ch: jnp.exp(bl[ch] - b_col[ch] + i_col[ch] - m_new[ch]) * kh[ch] for ch in chains}
    decay = {ch: jnp.exp(bl[ch] + m_old[ch] - m_new[ch]) for ch in chains}
    kv = {ch: _dot_tn(wk[ch].astype(BF16), vh[ch]) for ch in chains}
    rms = {ch: lax.rsqrt(jnp.mean(hh[ch] * hh[ch], axis=-1, keepdims=True) + EPS) for ch in chains}
    for ch in chains:
        bi, h = ch
        cn_ref[bi, h] = decay[ch] * cm[ch] + kv[ch]
        nn_ref[bi, h:h + 1, :] = decay[ch] * nv[ch] + jnp.sum(wk[ch], axis=0, keepdims=True)
        mn_ref[bi, :, h:h + 1] = m_new[ch]
        y_ref[bi, :, hsl(h)] = hh[ch] * rms[ch] * ng_ref[:, hsl(h)] * jax.nn.sigmoid(o_ref[bi, :, hsl(h)])


def _mlstm(layer, zml, zgt, gbias, ng, c0, n0, m0, *, bb, cl, t_valid):
    nb, t, _ = zml.shape
    nc = t // cl
    body = functools.partial(_mlstm_body, bb=bb, cl=cl, t_valid=t_valid)
    zspec = lambda j: pl.BlockSpec((bb, cl, ML_WIDTH), lambda i, c, j=j: (i, c, j))
    return pl.pallas_call(
        body,
        grid=(nb // bb, nc),
        in_specs=[zspec(0), zspec(1), zspec(2), zspec(3),
                  pl.BlockSpec((bb, cl, LANES), lambda i, c: (i, c, 0)),
                  _full((1, LANES)), _full((1, ML_WIDTH)),
                  pl.BlockSpec((1, bb, ML_HEADS, ML_DK, ML_DV), lambda i, c: (layer, i, 0, 0, 0)),
                  pl.BlockSpec((bb, ML_HEADS, ML_DK), lambda i, c: (i, 0, 0)),
                  pl.BlockSpec((bb, 1, ML_HEADS), lambda i, c: (i, 0, 0))],
        out_specs=[pl.BlockSpec((bb, cl, ML_WIDTH), lambda i, c: (i, c, 0)),
                   pl.BlockSpec((bb, ML_HEADS, ML_DK, ML_DV), lambda i, c: (i, 0, 0, 0)),
                   pl.BlockSpec((bb, ML_HEADS, ML_DK), lambda i, c: (i, 0, 0)),
                   pl.BlockSpec((bb, 1, ML_HEADS), lambda i, c: (i, 0, 0))],
        out_shape=[jax.ShapeDtypeStruct((nb, t, ML_WIDTH), F32),
                   jax.ShapeDtypeStruct((nb, ML_HEADS, ML_DK, ML_DV), F32),
                   jax.ShapeDtypeStruct((nb, ML_HEADS, ML_DK), F32),
                   jax.ShapeDtypeStruct((nb, 1, ML_HEADS), F32)],
        compiler_params=_cparams(("arbitrary", "arbitrary")),
        name="mlstm_chunks",
    )(zml, zml, zml, zml, zgt, gbias, ng, c0, n0, m0)


def _outproj_body(x_ref, y1_ref, y2_ref, y3_ref, wo_ref, g2_ref, rw_ref, rb_ref,
                  xn_ref, hn_ref, ridx_ref, rgt_ref, cnt_ref):
    acc = jnp.dot(y1_ref[...].astype(BF16), wo_ref[0:256, :], preferred_element_type=F32)
    acc = acc + jnp.dot(y2_ref[...].astype(BF16), wo_ref[256:512, :], preferred_element_type=F32)
    acc = acc + jnp.dot(y3_ref[...].astype(BF16), wo_ref[512:1024, :], preferred_element_type=F32)
    x = x_ref[...] + acc
    xn_ref[...] = x
    h = x * lax.rsqrt(jnp.mean(x * x, axis=-1, keepdims=True) + EPS) * g2_ref[...]
    _rows_to_tiles(h, hn_ref)
    hb = h.astype(BF16)
    tm = x.shape[0]
    eidx = lax.broadcasted_iota(jnp.int32, (N_EXPERTS, ROUTE_TILE), 0)
    sub = lax.broadcasted_iota(jnp.int32, (LANES, ROUTE_TILE), 0)
    cnt = jnp.zeros((N_EXPERTS, 1), F32)
    for j in range(tm // ROUTE_TILE):
        rows = slice(j * ROUTE_TILE, (j + 1) * ROUTE_TILE)
        vals = _dot_nt(rw_ref[...], hb[rows, :]) + rb_ref[...]
        tops, idxs = [], []
        for _ in range(TOP_K):
            m = jnp.max(vals, axis=0, keepdims=True)
            idx = jnp.min(jnp.where(vals == m, eidx, N_EXPERTS), axis=0, keepdims=True)
            vals = jnp.where(eidx == idx, -jnp.inf, vals)
            tops.append(m)
            idxs.append(idx)
        exps = [jnp.exp(t - tops[0]) for t in tops]
        inv = 1.0 / functools.reduce(lambda a, b: a + b, exps)
        ridx_ref[:, rows] = jnp.concatenate(idxs + idxs, axis=0)
        for idx in idxs:
            cnt = cnt + jnp.sum((eidx == idx).astype(F32), axis=1, keepdims=True)
        gt = jnp.zeros((LANES, ROUTE_TILE), F32)
        for kk in range(TOP_K):
            gt = jnp.where(sub == kk, exps[kk] * inv, gt)
        rgt_ref[rows, :] = gt.T
    cnt_ref[...] = jnp.broadcast_to(cnt, (N_EXPERTS, LANES))


def _outproj_two_body(*refs, tiles0):
    ins0, ins1 = refs[0:4], refs[4:8]
    shared = refs[8:12]
    xn0, rgt0, xn1, rgt1, hn_ref, ridx_ref, cnt_ref = refs[12:19]
    i = pl.program_id(0)

    @pl.when(i < tiles0)
    def _():
        _outproj_body(*ins0, *shared, xn0, hn_ref, ridx_ref, rgt0, cnt_ref)

    @pl.when(i >= tiles0)
    def _():
        _outproj_body(*ins1, *shared, xn1, hn_ref, ridx_ref, rgt1, cnt_ref)


def _outproj(streams, wo, g2, rw, rb):
    (x0, *_), (x1, *_) = streams
    tm = ROW_TILE
    t0, t1 = x0.shape[0] // tm, x1.shape[0] // tm
    n_all = x0.shape[0] + x1.shape[0]
    maps = (lambda i: (jnp.minimum(i, t0 - 1), 0), lambda i: (jnp.maximum(i - t0, 0), 0))
    row_specs = lambda m: [pl.BlockSpec((tm, w_), m) for w_ in (D_MODEL, S5_WIDTH, RG_WIDTH, ML_WIDTH)]
    own_out = lambda m: [pl.BlockSpec((tm, D_MODEL), m), pl.BlockSpec((tm, LANES), m)]
    own_shape = lambda x: [jax.ShapeDtypeStruct((x.shape[0], D_MODEL), F32),
                           jax.ShapeDtypeStruct((x.shape[0], LANES), F32)]
    xn0, rgt0, xn1, rgt1, hn, ridx, cnt = pl.pallas_call(
        functools.partial(_outproj_two_body, tiles0=t0),
        grid=(t0 + t1,),
        in_specs=row_specs(maps[0]) + row_specs(maps[1])
                 + [_full((D_MODEL, D_MODEL)), _full((1, D_MODEL)), _full((N_EXPERTS, D_MODEL)), _full((N_EXPERTS, 1))],
        out_specs=own_out(maps[0]) + own_out(maps[1])
                  + [pl.BlockSpec((tm * ROW_TILES, LANES), lambda i: (i, 0)),
                     pl.BlockSpec((2 * TOP_K, tm), lambda i: (0, i)),
                     pl.BlockSpec((N_EXPERTS, LANES), lambda i: (0, i))],
        out_shape=own_shape(x0) + own_shape(x1)
                  + [jax.ShapeDtypeStruct((n_all * ROW_TILES, LANES), F32),
                     jax.ShapeDtypeStruct((2 * TOP_K, n_all), jnp.int32),
                     jax.ShapeDtypeStruct((N_EXPERTS, (t0 + t1) * LANES), F32)],
        compiler_params=_cparams(("arbitrary",)),
        name="outproj_route",
    )(*streams[0], *streams[1], wo, g2, rw, rb)
    return (xn0, xn1), (rgt0, rgt1), hn, ridx, cnt


GATHER_AHEAD = 2
N_XBUF = GATHER_AHEAD + 1
N_OBUF = 3


def _moe_body(be_ref, bv_ref, tokn_ref, posp_ref, posc_ref, hn_ref, w1_ref, b1_ref, w2_ref, b2_ref, y4_ref,
              w1b, w2b, xbuf, obuf, gsem, ssem):
    i = pl.program_id(0)
    last = pl.num_programs(0) - 1
    real = bv_ref[i] > 0
    fresh = jnp.logical_or(i == GATHER_AHEAD, be_ref[i] != be_ref[jnp.maximum(i - 1, 0)])
    xs, xn = lax.rem(i, N_XBUF), lax.rem(i + GATHER_AHEAD, N_XBUF)
    os_, op, opp = lax.rem(i, N_OBUF), lax.rem(i + N_OBUF - 1, N_OBUF), lax.rem(i + N_OBUF - 2, N_OBUF)
    x_cur, x_nxt, o_cur, o_prev = xbuf.at[xs], xbuf.at[xn], obuf.at[os_], obuf.at[op]

    def tile(ref, row):
        return ref.at[pl.ds(pl.multiple_of(row, ROW_TILES), ROW_TILES)]

    def start_gathers(rows=range(MOE_TILE)):
        for r in rows:
            pltpu.make_async_copy(tile(hn_ref, tokn_ref[0, 0, r]), tile(x_nxt, r * ROW_TILES),
                                  gsem.at[xn]).start(priority=0)

    def start_scatters(pos_ref, o_buf, sem, rows=range(MOE_TILE)):
        for r in rows:
            pltpu.make_async_copy(tile(o_buf, r * ROW_TILES), tile(y4_ref, pos_ref[0, 0, r]), sem).start(priority=1)

    def start_block_scatter(pos_ref, o_buf, sem):
        first = pl.multiple_of(pos_ref[0, 0, 0], ROW_TILES)
        pltpu.make_async_copy(o_buf, y4_ref.at[pl.ds(first, MOE_TILE * ROW_TILES)], sem).start(priority=1)

    def wait_scatters(o_buf, sem):
        pltpu.make_async_copy(o_buf, y4_ref.at[pl.ds(0, MOE_TILE * ROW_TILES)], sem).wait()

    gathered = jnp.logical_and(i >= GATHER_AHEAD, jnp.logical_or(
        i < 2 * GATHER_AHEAD, bv_ref[jnp.maximum(i - GATHER_AHEAD, 0)] > 0))

    @pl.when(gathered)
    def _():
        pltpu.make_async_copy(hn_ref.at[pl.ds(0, MOE_TILE * ROW_TILES)], x_cur, gsem.at[xs]).wait()

    @pl.when(i >= N_OBUF)
    def _():
        wait_scatters(o_cur, ssem.at[os_])

    @pl.when(jnp.logical_and(fresh, real))
    def _():
        w1b[...] = w1_ref[0, 0].astype(BF16)
        w2b[...] = w2_ref[0, 0].astype(BF16)

    @pl.when(bv_ref[i] >= 1)
    def _():
        start_gathers()

    @pl.when(real)
    def _():
        xb = _tiles_to_rows(x_cur).astype(BF16)
        start_scatters(posp_ref, o_prev, ssem.at[op])
        hb = jnp.dot(xb, w1b[...], preferred_element_type=F32) + b1_ref[0, 0]
        g = jnp.minimum(hb[:, :D_FF], SWIGLU_LIMIT)
        u = jnp.clip(hb[:, D_FF:], -SWIGLU_LIMIT, SWIGLU_LIMIT)
        act = g * jax.nn.sigmoid(SWIGLU_ALPHA * g) * (u + 1.0)
        _rows_to_tiles(jnp.dot(act.astype(BF16), w2b[...], preferred_element_type=F32) + b2_ref[0, 0], o_cur)

    @pl.when(jnp.logical_not(real))
    def _():
        @pl.when(i == 0)
        def _():
            obuf[...] = jnp.zeros_like(obuf)

        @pl.when(i < GATHER_AHEAD)
        def _():
            start_gathers()

        prev_real = bv_ref[jnp.maximum(i - 1, 0)] > 0

        @pl.when(jnp.logical_and(i >= 1, prev_real))
        def _():
            start_scatters(posp_ref, o_prev, ssem.at[op])

        @pl.when(jnp.logical_and(i >= 1, jnp.logical_not(prev_real)))
        def _():
            start_block_scatter(posp_ref, o_prev, ssem.at[op])

    @pl.when(i == last)
    def _():
        start_block_scatter(posc_ref, o_cur, ssem.at[os_])
        wait_scatters(obuf.at[opp], ssem.at[opp])
        wait_scatters(o_prev, ssem.at[op])
        wait_scatters(o_cur, ssem.at[os_])


def _moe(layer, block_e, block_v, slot_tok, slot_pos, hn, w1, b1, w2, b2):
    n_blocks = block_e.shape[0]
    tm = MOE_TILE
    n_rows = n_blocks * tm
    assert n_blocks > 2 * GATHER_AHEAD
    idx_spec = lambda f: pl.BlockSpec((1, 1, tm), f, memory_space=pltpu.SMEM)
    grid_spec = pltpu.PrefetchScalarGridSpec(
        num_scalar_prefetch=2,
        grid=(n_blocks,),
        in_specs=[idx_spec(lambda i, be, bv: (jnp.minimum(i + GATHER_AHEAD, n_blocks - 1), 0, 0)),
                  idx_spec(lambda i, be, bv: (jnp.maximum(i - 1, 0), 0, 0)),
                  idx_spec(lambda i, be, bv: (i, 0, 0)),
                  pl.BlockSpec(memory_space=pl.ANY),
                  pl.BlockSpec((1, 1, D_MODEL, 2 * D_FF), lambda i, be, bv: (layer, be[i], 0, 0)),
                  pl.BlockSpec((1, 1, 1, 2 * D_FF), lambda i, be, bv: (layer, be[i], 0, 0)),
                  pl.BlockSpec((1, 1, D_FF, D_MODEL), lambda i, be, bv: (layer, be[i], 0, 0)),
                  pl.BlockSpec((1, 1, 1, D_MODEL), lambda i, be, bv: (layer, be[i], 0, 0))],
        out_specs=pl.BlockSpec(memory_space=pl.ANY),
        scratch_shapes=[pltpu.VMEM((D_MODEL, 2 * D_FF), BF16), pltpu.VMEM((D_FF, D_MODEL), BF16)]
                       + [pltpu.VMEM((N_XBUF, tm * ROW_TILES, LANES), F32),
                          pltpu.VMEM((N_OBUF, tm * ROW_TILES, LANES), F32)]
                       + [pltpu.SemaphoreType.DMA((N_XBUF,)), pltpu.SemaphoreType.DMA((N_OBUF,))],
    )
    return pl.pallas_call(
        _moe_body,
        grid_spec=grid_spec,
        out_shape=jax.ShapeDtypeStruct((n_rows * ROW_TILES, LANES), F32),
        compiler_params=_cparams(("arbitrary",)),
        name="moe_experts",
    )(block_e, block_v, slot_tok, slot_pos, slot_pos, hn, w1, b1, w2, b2)


def _combine_body(x_ref, rgt_ref, fg_ref, *rest, final):
    y_refs, o_ref = rest[:TOP_K], rest[TOP_K]
    x = x_ref[...]
    for kk in range(TOP_K):
        x = x + rgt_ref[:, kk:kk + 1] * _tiles_to_rows(y_refs[kk])
    if final:
        x = x * lax.rsqrt(jnp.mean(x * x, axis=-1, keepdims=True) + EPS) * fg_ref[...]
    o_ref[...] = x


def _combine(x, y4, rgt, fg, *, tok0, n_all, final):
    n = x.shape[0]
    tm = ROW_TILE
    assert tok0 % tm == 0 and n_all % tm == 0
    yspec = lambda kk: pl.BlockSpec((tm * ROW_TILES, LANES), lambda i, kk=kk: ((kk * n_all + tok0) // tm + i, 0))
    return pl.pallas_call(
        functools.partial(_combine_body, final=final),
        grid=(n // tm,),
        in_specs=[pl.BlockSpec((tm, D_MODEL), lambda i: (i, 0)),
                  pl.BlockSpec((tm, LANES), lambda i: (i, 0)), _full((1, D_MODEL))]
                 + [yspec(kk) for kk in range(TOP_K)],
        out_specs=pl.BlockSpec((tm, D_MODEL), lambda i: (i, 0)),
        out_shape=jax.ShapeDtypeStruct((n, D_MODEL), F32),
        compiler_params=_cparams(("arbitrary",)),
        name="moe_combine",
    )(x, rgt, fg, *([y4] * TOP_K))


def _routing_tables(ridx, n, counts):
    na = n * TOP_K
    tm = MOE_TILE
    assert na % tm == 0
    lead = GATHER_AHEAD * tm
    n_blocks = na // tm + N_EXPERTS + 2 * GATHER_AHEAD
    n_slots = n_blocks * tm
    experts = jnp.arange(N_EXPERTS, dtype=jnp.int32)
    flat_e = ridx[:TOP_K, :].reshape(-1)
    order = jnp.argsort(flat_e, stable=False).astype(jnp.int32)
    padded = (counts + tm - 1) // tm * tm
    pad_end = jnp.cumsum(padded)
    pad_start = pad_end - padded
    start = jnp.cumsum(counts) - counts
    spare_start = na + pad_start - start
    total = pad_end[-1]
    blk = jnp.arange(n_blocks, dtype=jnp.int32) * tm - lead
    block_v = ((blk >= 0) & (blk < total)).astype(jnp.int32)
    last_e = jnp.max(jnp.where(counts > 0, experts, 0))
    block_e = jnp.minimum(jnp.sum(blk[:, None] >= pad_end[None, :], axis=1), last_e).astype(jnp.int32)
    rank = (blk - pad_start[block_e])[:, None] + jnp.arange(tm, dtype=jnp.int32)[None, :]
    cnt = counts[block_e][:, None]
    slot = blk[:, None] + jnp.arange(tm, dtype=jnp.int32)[None, :]
    used = (slot >= 0) & (slot < total)
    real = (rank < cnt) & used
    src = order[jnp.clip(start[block_e][:, None] + rank, 0, na - 1)]
    slot_tok = (jnp.where(real, src % n, 0) * ROW_TILES).astype(jnp.int32)
    spare = jnp.where(used, spare_start[block_e][:, None] + rank - cnt, jnp.where(slot >= 0, slot, n_slots + slot))
    slot_pos = (jnp.where(real, src, spare) * ROW_TILES).astype(jnp.int32)
    return block_e, block_v, slot_tok.reshape(n_blocks, 1, tm), slot_pos.reshape(n_blocks, 1, tm)


def _blockdiag(m):
    g, a, b = m.shape
    eye = jnp.eye(g, dtype=m.dtype)
    return (eye[:, None, :, None] * m[:, :, None, :]).reshape(g * a, g * b)


def _layer_params(l, p):
    lam = jnp.zeros((SUBLANES, S5_FLAT), F32)
    lam = lam.at[0].set(p['s5_lambda_re'][l].reshape(-1)).at[1].set(p['s5_lambda_im'][l].reshape(-1))
    lam = lam.at[2].set(jnp.repeat(p['s5_log_step'][l], S5_STATE))
    gbias = jnp.zeros((1, LANES), F32).at[0, :2 * ML_HEADS].set(p['ml_gate_bias'][l])
    w_in = jnp.pad(p['w_in'][l], ((0, 0), (0, N_IN_PAD - N_IN))).astype(BF16)
    return dict(
        norm1_g=p['norm1_g'][l].reshape(1, -1), w_in=w_in, s5_lam=lam,
        s5_bre=_blockdiag(p['s5_b_re'][l].transpose(0, 2, 1)), s5_bim=_blockdiag(p['s5_b_im'][l].transpose(0, 2, 1)),
        s5_cre=_blockdiag(p['s5_c_re'][l].transpose(0, 2, 1)).astype(BF16),
        s5_cim=_blockdiag(p['s5_c_im'][l].transpose(0, 2, 1)).astype(BF16),
        s5_d=p['s5_d'][l].reshape(1, -1), s5_gw=p['s5_glu_w'][l].astype(BF16), s5_gb=p['s5_glu_b'][l].reshape(1, -1),
        rg_cw=p['rg_conv_w'][l], rg_cb=p['rg_conv_b'][l].reshape(1, -1),
        rg_wa=_blockdiag(p['rg_wa'][l]).astype(BF16), rg_ba=p['rg_ba'][l].reshape(1, -1),
        rg_wx=_blockdiag(p['rg_wx'][l]).astype(BF16), rg_bx=p['rg_bx'][l].reshape(1, -1),
        rg_lam=p['rg_lambda'][l].reshape(1, -1),
        ml_gb=gbias, ml_ng=p['ml_norm_g'][l].reshape(1, -1),
        w_out=p['w_out'][l].astype(BF16), norm2_g=p['norm2_g'][l].reshape(1, -1),
        router_w=p['router_w'][l].T.astype(BF16), router_b=p['router_b'][l].reshape(-1, 1),
    )


def _mix(l, lp, xf, states, cfg):
    nb, t, t_valid = cfg['nb'], cfg['t'], cfg['t_valid']
    n = nb * t
    s5r, s5i, rgh, rgc, mlc, mln, mlm = states
    zs5, zrg, zml, zgt = _inproj(xf, lp['norm1_g'], lp['w_in'])
    y1, nr, ni = _s5(zs5.reshape(nb, t, -1), s5r[l].reshape(nb, -1), s5i[l].reshape(nb, -1), lp['s5_lam'],
                     lp['s5_bre'], lp['s5_bim'], lp['s5_cre'], lp['s5_cim'], lp['s5_d'], lp['s5_gw'], lp['s5_gb'],
                     tc=cfg['s5_tc'], t_valid=t_valid)
    y2, nh, ncv = _rglru(zrg.reshape(nb, t, -1), rgh[l], rgc[l], lp['rg_cw'], lp['rg_cb'], lp['rg_wa'],
                         lp['rg_ba'], lp['rg_wx'], lp['rg_bx'], lp['rg_lam'], tc=cfg['s5_tc'], t_valid=t_valid)
    y3, nc_, nn_, nm_ = _mlstm(l, zml.reshape(nb, t, -1), zgt.reshape(nb, t, -1), lp['ml_gb'], lp['ml_ng'],
                               mlc, mln[l], mlm[l].reshape(nb, 1, ML_HEADS),
                               bb=cfg['ml_bb'], cl=cfg['ml_cl'], t_valid=t_valid)
    new_states = (nr.reshape(nb, S5_GROUPS, S5_STATE), ni.reshape(nb, S5_GROUPS, S5_STATE), nh, ncv, nc_, nn_,
                  nm_.reshape(nb, ML_HEADS))
    return (xf, y1.reshape(n, -1), y2.reshape(n, -1), y3.reshape(n, -1)), new_states


def _trunks(xs, states, cfgs, params, final_g):
    n_all = sum(x.shape[0] for x in xs)
    tok0 = [sum(x.shape[0] for x in xs[:s]) for s in range(len(xs))]
    outs = [[[] for _ in range(7)] for _ in xs]
    b1 = params['exp_b1'].reshape(DEPTH, N_EXPERTS, 1, -1)
    b2 = params['exp_b2'].reshape(DEPTH, N_EXPERTS, 1, -1)
    for l in range(DEPTH):
        lp = _layer_params(l, params)
        mixed = [_mix(l, lp, xs[s], states[s], cfgs[s]) for s in range(len(xs))]
        xn, rgt, hn_all, ridx_all, cnt = _outproj([m[0] for m in mixed], lp['w_out'], lp['norm2_g'],
                                                  lp['router_w'], lp['router_b'])
        counts = jnp.sum(cnt[:, ::LANES], axis=1).astype(jnp.int32)
        block_e, block_v, slot_tok, slot_pos = _routing_tables(ridx_all, n_all, counts)
        y4 = _moe(l, block_e, block_v, slot_tok, slot_pos, hn_all, params['exp_w1'], b1, params['exp_w2'], b2)
        xs = [_combine(xn[s], y4, rgt[s], final_g.reshape(1, -1), tok0=tok0[s], n_all=n_all,
                       final=(l == DEPTH - 1)) for s in range(len(xs))]
        for s in range(len(xs)):
            for lst, val in zip(outs[s], mixed[s][1]):
                lst.append(val)
    return xs, [[jnp.stack(v) for v in o] for o in outs]


def kernel(x_prompt, x_sample, state_s5_re, state_s5_im, state_rg_h, state_rg_conv, state_ml_c, state_ml_n, state_ml_m, norm1_g, w_in, s5_lambda_re, s5_lambda_im, s5_log_step, s5_b_re, s5_b_im, s5_c_re, s5_c_im, s5_d, s5_glu_w, s5_glu_b, rg_conv_w, rg_conv_b, rg_wa, rg_ba, rg_wx, rg_bx, rg_lambda, ml_gate_bias, ml_norm_g, w_out, norm2_g, router_w, router_b, exp_w1, exp_b1, exp_w2, exp_b2, final_norm_g):
    params = dict(norm1_g=norm1_g, w_in=w_in, s5_lambda_re=s5_lambda_re, s5_lambda_im=s5_lambda_im,
                  s5_log_step=s5_log_step, s5_b_re=s5_b_re, s5_b_im=s5_b_im, s5_c_re=s5_c_re, s5_c_im=s5_c_im,
                  s5_d=s5_d, s5_glu_w=s5_glu_w, s5_glu_b=s5_glu_b, rg_conv_w=rg_conv_w, rg_conv_b=rg_conv_b,
                  rg_wa=rg_wa, rg_ba=rg_ba, rg_wx=rg_wx, rg_bx=rg_bx, rg_lambda=rg_lambda,
                  ml_gate_bias=ml_gate_bias, ml_norm_g=ml_norm_g, w_out=w_out, norm2_g=norm2_g,
                  router_w=router_w, router_b=router_b, exp_w1=exp_w1, exp_b1=exp_b1, exp_w2=exp_w2, exp_b2=exp_b2)
    bp, tp, _ = x_prompt.shape
    bs, ts, _ = x_sample.shape
    assert tp >= RG_CONV - 1 and ts >= RG_CONV - 1
    zeros = lambda *shape: jnp.zeros((DEPTH, bp) + shape, F32)
    prompt_states = (zeros(S5_GROUPS, S5_STATE), zeros(S5_GROUPS, S5_STATE), zeros(RG_WIDTH),
                     zeros(RG_CONV - 1, RG_WIDTH), zeros(ML_HEADS, ML_DK, ML_DV), zeros(ML_HEADS, ML_DK),
                     zeros(ML_HEADS))
    ts_pad = -(-ts // SUBLANES) * SUBLANES
    xs_pad = jnp.pad(x_sample, ((0, 0), (0, ts_pad - ts), (0, 0)))
    sample_states = (state_s5_re, state_s5_im, state_rg_h, state_rg_conv, state_ml_c, state_ml_n, state_ml_m)
    cfgs = [dict(nb=bp, t=tp, t_valid=tp, s5_tc=128, ml_bb=8, ml_cl=math.gcd(tp, ML_CHUNK_LEN)),
            dict(nb=bs, t=ts_pad, t_valid=ts, s5_tc=ts_pad, ml_bb=8, ml_cl=ts_pad)]
    (yp, ys), (sp, ss) = _trunks([x_prompt.reshape(bp * tp, D_MODEL), xs_pad.reshape(bs * ts_pad, D_MODEL)],
                                 [prompt_states, sample_states], cfgs, params, final_norm_g)
    return (yp.reshape(bp, tp, D_MODEL), ys.reshape(bs, ts_pad, D_MODEL)[:, :ts], *sp, *ss)
```
